```python
import math
import jax, jax.numpy as jnp
from jax import lax
import numpy as np

D_MODEL = 1024
BATCH = 4
SEQ = 8192
DEPTH = 1

M_WIDTH = D_MODEL // 2
M_HEADS = 4
M_HEAD_DIM = M_WIDTH // M_HEADS
CONV_K = 4
MLSTM_CHUNK = 128
G_WIDTH = D_MODEL // 2
G_GROUPS = 4
G_GROUP_DIM = G_WIDTH // G_GROUPS
SG_CHUNK = 128
D_MIX = M_WIDTH + G_WIDTH
IN_COLS = 2 * M_WIDTH + 2 * G_WIDTH
N_EXPERTS = 256
TOP_K = 8
N_GROUPS = 8
TOPK_GROUPS = 4
EXP_PER_GROUP = N_EXPERTS // N_GROUPS
D_EXPERT = D_MODEL // 4
ROUTE_SCALE = 2.5
DISPATCH_BLOCK = 128
DN_ALPHA = (2 * DEPTH) ** 0.25
DN_BETA = (8 * DEPTH) ** -0.25
LN_EPS = 1e-5

kernel_name = "hybrid_mlstm_sgmlp_moe_deepnorm"


def _layernorm(x):
    xf = x.astype(jnp.float32)
    mu = jnp.mean(xf, axis=-1, keepdims=True)
    var = jnp.mean(jnp.square(xf - mu), axis=-1, keepdims=True)
    return ((xf - mu) * lax.rsqrt(var + LN_EPS)).astype(x.dtype)


def _causal_dwconv(x, w, b):
    y = lax.conv_general_dilated(x, w[:, None, :], window_strides=(1,), padding=[(CONV_K - 1, 0)],
                                 dimension_numbers=('NWC', 'WIO', 'NWC'), feature_group_count=x.shape[-1])
    return y + b


def _mlstm_chunkwise(q, k, v, i_pre, f_pre):
    bsz, nh, seq, dh = q.shape
    L = MLSTM_CHUNK
    nc = seq // L
    q = q.reshape(bsz, nh, nc, L, dh)
    k = k.reshape(bsz, nh, nc, L, dh)
    v = v.reshape(bsz, nh, nc, L, dh)
    log_i = i_pre.reshape(bsz, nh, nc, L)
    b = jnp.cumsum(jax.nn.log_sigmoid(f_pre).reshape(bsz, nh, nc, L), axis=-1)
    g = b[..., -1]
    a = g[..., None] - b + log_i

    def step(carry, xs):
        c_st, n_st, m_st = carry
        g_j, a_j, k_j, v_j = xs
        m_new = jnp.maximum(g_j + m_st, jnp.max(a_j, axis=-1))
        decay = jnp.exp(g_j + m_st - m_new)
        w_j = jnp.exp(a_j - m_new[..., None])
        c_new = decay[..., None, None] * c_st + jnp.einsum('bhl,bhld,bhle->bhde', w_j, v_j, k_j)
        n_new = decay[..., None] * n_st + jnp.einsum('bhl,bhle->bhe', w_j, k_j)
        return (c_new, n_new, m_new), (c_st, n_st, m_st)

    init = (jnp.zeros((bsz, nh, dh, dh), jnp.float32), jnp.zeros((bsz, nh, dh), jnp.float32),
            jnp.zeros((bsz, nh), jnp.float32))
    front = lambda t: jnp.moveaxis(t, 2, 0)
    _, (c_prev, n_prev, m_prev) = lax.scan(step, init, (front(g), front(a), front(k), front(v)))
    c_prev = jnp.moveaxis(c_prev, 0, 2)
    n_prev = jnp.moveaxis(n_prev, 0, 2)
    m_prev = jnp.moveaxis(m_prev, 0, 2)

    causal = jnp.tril(jnp.ones((L, L), bool))
    log_d = jnp.where(causal, b[..., :, None] - b[..., None, :] + log_i[..., None, :], -jnp.inf)
    log_inter = b + m_prev[..., None]
    m_t = jnp.maximum(log_inter, jnp.max(log_d, axis=-1))
    p = jnp.exp(log_d - m_t[..., None]) * jnp.einsum('bhctd,bhcsd->bhcts', q, k)
    w_inter = jnp.exp(log_inter - m_t)
    num = w_inter[..., None] * jnp.einsum('bhcde,bhcte->bhctd', c_prev, q) + jnp.einsum('bhcts,bhcsd->bhctd', p, v)
    nq = w_inter * jnp.einsum('bhce,bhcte->bhct', n_prev, q) + jnp.sum(p, axis=-1)
    h = num / jnp.maximum(jnp.abs(nq), jnp.exp(-m_t))[..., None]
    return h.reshape(bsz, nh, seq, dh)


def _hybrid_mixer(h, w_in, conv_w, conv_b, w_q, w_k, w_v, w_if, b_if, mh_g, skip, sg_g, sg_b, w_sp, b_sp, w_out):
    bsz, seq, _ = h.shape
    proj = h @ w_in
    xm, z, u, v = jnp.split(proj, [M_WIDTH, 2 * M_WIDTH, 2 * M_WIDTH + G_WIDTH], axis=-1)
    xc = jax.nn.silu(_causal_dwconv(xm, conv_w, conv_b))
    heads = lambda t: t.reshape(bsz, seq, M_HEADS, M_HEAD_DIM)
    q = jnp.einsum('bshd,hde->bshe', heads(xc), w_q)
    k = jnp.einsum('bshd,hde->bshe', heads(xc), w_k) * (M_HEAD_DIM ** -0.5)
    vv = jnp.einsum('bshd,hde->bshe', heads(xm), w_v)
    flat = lambda t: t.reshape(bsz, seq, M_WIDTH)
    gate_pre = (jnp.concatenate([flat(q), flat(k), flat(vv)], axis=-1) @ w_if + b_if).astype(jnp.float32)
    i_pre = jnp.transpose(gate_pre[..., :M_HEADS], (0, 2, 1))
    f_pre = jnp.transpose(gate_pre[..., M_HEADS:], (0, 2, 1))
    to_bhsd = lambda t: jnp.transpose(t, (0, 2, 1, 3)).astype(jnp.float32)
    hc = _mlstm_chunkwise(to_bhsd(q), to_bhsd(k), to_bhsd(vv), i_pre, f_pre)
    hc = _layernorm(jnp.transpose(hc, (0, 2, 1, 3))) * mh_g.reshape(M_HEADS, M_HEAD_DIM)
    hm = jax.nn.sigmoid(z) * (hc.reshape(bsz, seq, M_WIDTH).astype(h.dtype) + skip * xc)
    nc = seq // SG_CHUNK
    ug = jax.nn.gelu(u, approximate=False).reshape(bsz, nc, SG_CHUNK, G_GROUPS, G_GROUP_DIM)
    vg = jax.nn.gelu(v, approximate=False).reshape(bsz, nc, SG_CHUNK, G_GROUPS, G_GROUP_DIM)
    vn = _layernorm(vg) * sg_g.reshape(G_GROUPS, G_GROUP_DIM) + sg_b.reshape(G_GROUPS, G_GROUP_DIM)
    ws = jnp.tril(w_sp)
    s = jnp.einsum('gts,bcsgd->bctgd', ws, vn) + b_sp.T[:, :, None]
    hg = (ug * s).reshape(bsz, seq, G_WIDTH)
    return jnp.concatenate([hm, hg], axis=-1) @ w_out


def _route(h_flat, w_router, e_bias):
    t = h_flat.shape[0]
    scores = jax.nn.sigmoid(h_flat.astype(jnp.float32) @ w_router.astype(jnp.float32))
    sel = scores + e_bias.astype(jnp.float32)
    grp_score = jnp.sum(lax.top_k(sel.reshape(t, N_GROUPS, EXP_PER_GROUP), 2)[0], axis=-1)
    _, grp_idx = lax.top_k(grp_score, TOPK_GROUPS)
    grp_mask = jnp.any(grp_idx[:, :, None] == jnp.arange(N_GROUPS)[None, None, :], axis=1)
    sel = jnp.where(jnp.repeat(grp_mask, EXP_PER_GROUP, axis=1), sel, -jnp.inf)
    _, idx = lax.top_k(sel, TOP_K)
    w = jnp.take_along_axis(scores, idx, axis=1)
    w = w / jnp.sum(w, axis=-1, keepdims=True) * ROUTE_SCALE
    return idx, w


def _routed_experts(h_flat, idx, wts, w_gate, w_up, w_down):
    t, d = h_flat.shape
    n_assign = t * TOP_K
    flat_e = idx.reshape(n_assign)
    flat_tok = jnp.arange(n_assign, dtype=jnp.int32) // TOP_K
    flat_w = wts.reshape(n_assign).astype(h_flat.dtype)
    order = jnp.argsort(flat_e, stable=True)
    e_sorted = flat_e[order]
    counts = jnp.bincount(flat_e, length=N_EXPERTS)
    padded = (counts + DISPATCH_BLOCK - 1) // DISPATCH_BLOCK * DISPATCH_BLOCK
    pad_end = jnp.cumsum(padded)
    pad_start = pad_end - padded
    start = jnp.cumsum(counts) - counts
    dest = pad_start[e_sorted] + jnp.arange(n_assign, dtype=jnp.int32) - start[e_sorted]
    n_blocks = (n_assign + N_EXPERTS * (DISPATCH_BLOCK - 1) + DISPATCH_BLOCK - 1) // DISPATCH_BLOCK
    n_slots = n_blocks * DISPATCH_BLOCK
    slot_tok = jnp.full((n_slots,), t, jnp.int32).at[dest].set(flat_tok[order])
    slot_w = jnp.zeros((n_slots,), h_flat.dtype).at[dest].set(flat_w[order])
    block_e = jnp.minimum(jnp.searchsorted(pad_end, jnp.arange(n_blocks) * DISPATCH_BLOCK, side='right'),
                          N_EXPERTS - 1)
    h_pad = jnp.concatenate([h_flat, jnp.zeros((1, d), h_flat.dtype)], axis=0)

    def body(acc, xs):
        tok, w, e = xs
        xb = h_pad[tok]
        y = (jax.nn.silu(xb @ w_gate[e]) * (xb @ w_up[e])) @ w_down[e]
        return acc.at[tok].add(y * w[:, None]), None

    acc, _ = lax.scan(body, jnp.zeros((t + 1, d), h_flat.dtype),
                      (slot_tok.reshape(n_blocks, DISPATCH_BLOCK), slot_w.reshape(n_blocks, DISPATCH_BLOCK), block_e))
    return acc[:t]


def _moe(h, w_router, e_bias, w_gate, w_up, w_down, ws_gate, ws_up, ws_down):
    bsz, seq, d = h.shape
    h_flat = h.reshape(bsz * seq, d)
    idx, wts = _route(h_flat, w_router, e_bias)
    routed = _routed_experts(h_flat, idx, wts, w_gate, w_up, w_down)
    shared = (jax.nn.silu(h_flat @ ws_gate) * (h_flat @ ws_up)) @ ws_down
    return (routed + shared).reshape(bsz, seq, d)


def setup_inputs(seed: int = 0) -> dict:
    key = jax.random.key(seed)
    ks = jax.random.split(key, 40)
    nrm = lambda k, shape, scale: jax.random.normal(k, shape, jnp.float32) * scale
    L = DEPTH
    b_i = nrm(ks[11], (L, M_HEADS), 0.1)
    b_f = jnp.linspace(3.0, 6.0, M_HEADS, dtype=jnp.float32)[None, :] + nrm(ks[12], (L, M_HEADS), 0.1)
    return {
        "x": nrm(ks[0], (BATCH, SEQ, D_MODEL), 1.0),
        "c": nrm(ks[1], (BATCH, D_MODEL), 1.0),
        "w_ada": nrm(ks[2], (L, D_MODEL, 6 * D_MODEL), 0.1 * D_MODEL ** -0.5),
        "b_ada": nrm(ks[3], (L, 6 * D_MODEL), 0.02),
        "w_in": nrm(ks[4], (L, D_MODEL, IN_COLS), D_MODEL ** -0.5),
        "conv_w": nrm(ks[5], (L, CONV_K, M_WIDTH), CONV_K ** -0.5),
        "conv_b": nrm(ks[6], (L, M_WIDTH), 0.02),
        "w_q": nrm(ks[7], (L, M_HEADS, M_HEAD_DIM, M_HEAD_DIM), M_HEAD_DIM ** -0.5),
        "w_k": nrm(ks[8], (L, M_HEADS, M_HEAD_DIM, M_HEAD_DIM), M_HEAD_DIM ** -0.5),
        "w_v": nrm(ks[9], (L, M_HEADS, M_HEAD_DIM, M_HEAD_DIM), M_HEAD_DIM ** -0.5),
        "w_if": nrm(ks[10], (L, 3 * M_WIDTH, 2 * M_HEADS), 0.02),
        "b_if": jnp.concatenate([b_i, b_f], axis=-1),
        "mh_g": 1.0 + nrm(ks[13], (L, M_WIDTH), 0.02),
        "skip": 1.0 + nrm(ks[14], (L, M_WIDTH), 0.02),
        "sg_g": 1.0 + nrm(ks[15], (L, G_WIDTH), 0.02),
        "sg_b": nrm(ks[16], (L, G_WIDTH), 0.02),
        "w_sp": nrm(ks[17], (L, G_GROUPS, SG_CHUNK, SG_CHUNK), SG_CHUNK ** -0.5),
        "b_sp": 1.0 + nrm(ks[18], (L, G_GROUPS, SG_CHUNK), 0.02),
        "w_out": nrm(ks[19], (L, D_MIX, D_MODEL), DN_BETA * D_MIX ** -0.5),
        "ln1_g": 1.0 + nrm(ks[20], (L, D_MODEL), 0.02),
        "ln1_b": nrm(ks[21], (L, D_MODEL), 0.02),
        "w_router": nrm(ks[22], (L, D_MODEL, N_EXPERTS), D_MODEL ** -0.5),
        "e_bias": nrm(ks[23], (L, N_EXPERTS), 0.01),
        "w_gate": nrm(ks[24], (L, N_EXPERTS, D_MODEL, D_EXPERT), D_MODEL ** -0.5),
        "w_up": nrm(ks[25], (L, N_EXPERTS, D_MODEL, D_EXPERT), D_MODEL ** -0.5),
        "w_down": nrm(ks[26], (L, N_EXPERTS, D_EXPERT, D_MODEL), DN_BETA * D_EXPERT ** -0.5),
        "ws_gate": nrm(ks[27], (L, D_MODEL, D_EXPERT), D_MODEL ** -0.5),
        "ws_up": nrm(ks[28], (L, D_MODEL, D_EXPERT), D_MODEL ** -0.5),
        "ws_down": nrm(ks[29], (L, D_EXPERT, D_MODEL), DN_BETA * D_EXPERT ** -0.5),
        "ln2_g": 1.0 + nrm(ks[30], (L, D_MODEL), 0.02),
        "ln2_b": nrm(ks[31], (L, D_MODEL), 0.02),
    }


def reference(x, c, w_ada, b_ada, w_in, conv_w, conv_b, w_q, w_k, w_v, w_if, b_if, mh_g, skip, sg_g, sg_b,
              w_sp, b_sp, w_out, ln1_g, ln1_b, w_router, e_bias, w_gate, w_up, w_down, ws_gate, ws_up, ws_down,
              ln2_g, ln2_b):
    for l in range(DEPTH):
        mod = jax.nn.silu(c) @ w_ada[l] + b_ada[l]
        sh_a, sc_a, gt_a, sh_f, sc_f, gt_f = [m[:, None, :] for m in jnp.split(mod, 6, axis=-1)]
        h = _layernorm(x) * (1.0 + sc_a) + sh_a
        mix = _hybrid_mixer(h, w_in[l], conv_w[l], conv_b[l], w_q[l], w_k[l], w_v[l], w_if[l], b_if[l],
                            mh_g[l], skip[l], sg_g[l], sg_b[l], w_sp[l], b_sp[l], w_out[l])
        x = _layernorm(DN_ALPHA * x + (1.0 + gt_a) * mix) * ln1_g[l] + ln1_b[l]
        h = _layernorm(x) * (1.0 + sc_f) + sh_f
        ffn = _moe(h, w_router[l], e_bias[l], w_gate[l], w_up[l], w_down[l], ws_gate[l], ws_up[l], ws_down[l])
        x = _layernorm(DN_ALPHA * x + (1.0 + gt_f) * ffn) * ln2_g[l] + ln2_b[l]
    return x
```

```python
import functools
import math

import jax
import jax.numpy as jnp
from jax import lax
from jax.experimental import pallas as pl
from jax.experimental.pallas import tpu as pltpu

F32 = jnp.float32
BF16 = jnp.bfloat16
I32 = jnp.int32

LN_EPS = 1e-5
M_HEADS = 4
G_GROUPS = 4
CHUNK = 128
N_GROUPS = 8
TOPK_GROUPS = 4
TOP_K = 8
ROUTE_SCALE = 2.5
LANES = 128
SUBLANES = 8

MIXER_TOKENS = 512
ROUTE_TOKENS = 1024
DEST_TOKENS = 2048
DISPATCH_TOKENS = 256
EXPERT_ROWS = 256
COMBINE_TOKENS = 256
VMEM_LIMIT = 56 * 1024 * 1024

NEG_INF = float("-inf")


def _ln(x):
    mu = jnp.mean(x, axis=-1, keepdims=True)
    xc = x - mu
    var = jnp.mean(xc * xc, axis=-1, keepdims=True)
    return xc * lax.rsqrt(var + LN_EPS)


def _dot(a, b):
    return jnp.dot(a, b, preferred_element_type=F32)


def _dot_nt(a, b):
    return lax.dot_general(a, b, (((1,), (1,)), ((), ())), preferred_element_type=F32)


def _dot_exact(a, b):
    return jnp.dot(a, b, preferred_element_type=F32, precision=lax.Precision.HIGHEST)


def _silu(x):
    return x * jax.nn.sigmoid(x)


def _gelu(x):
    return 0.5 * x * (1.0 + lax.erf(x * math.sqrt(0.5)))


def _log_sigmoid(x):
    return jnp.minimum(x, 0.0) - jnp.log1p(jnp.exp(-jnp.abs(x)))


def _ada_kernel(c_ref, w_ref, b_ref, o_ref):
    o_ref[...] = _dot_exact(_silu(c_ref[...]), w_ref[...]) + b_ref[...]


def _ada(c_pad, w_ada, b_ada):
    rows, d = c_pad.shape
    n = w_ada.shape[1]
    return pl.pallas_call(
        _ada_kernel,
        out_shape=jax.ShapeDtypeStruct((rows, n), F32),
        grid=(n // d,),
        in_specs=[
            pl.BlockSpec((rows, d), lambda j: (0, 0)),
            pl.BlockSpec((d, d), lambda j: (0, j)),
            pl.BlockSpec((1, d), lambda j: (0, j)),
        ],
        out_specs=pl.BlockSpec((rows, d), lambda j: (0, j)),
        compiler_params=pltpu.CompilerParams(vmem_limit_bytes=VMEM_LIMIT),
        name="ada",
    )(c_pad, w_ada, b_ada)


def _mixer_kernel(x_ref, mod_ref, w_in_ref, conv_w_ref, conv_b_ref, wq_ref, wk_ref, wv_ref,
                  wif_ref, bif_ref, mhg_ref, skip_ref, sgg_ref, sgb_ref, wsp_ref, bspt_ref,
                  w_out_ref, ln1g_ref, ln1b_ref, wr_hi_ref, wr_lo_ref,
                  x1_ref, h2_ref, logt_ref,
                  state_ref, m_ref, xm_ref, cat_ref, *, alpha):
    tm = x_ref.shape[1]
    mw = conv_w_ref.shape[1]
    dh = mw // M_HEADS
    gw = sgg_ref.shape[1]
    gd = gw // G_GROUPS
    conv_k = conv_w_ref.shape[0]
    nch = tm // CHUNK
    L = CHUNK

    @pl.when(pl.program_id(1) == 0)
    def _():
        state_ref[...] = jnp.zeros_like(state_ref)
        m_ref[...] = jnp.zeros_like(m_ref)
        xm_ref[0:SUBLANES, :] = jnp.zeros((SUBLANES, mw), F32)

    x = x_ref[0]
    mod = mod_ref[0]
    sh_a, sc_a, gt_a = mod[0:1], mod[1:2], mod[2:3]
    sh_f, sc_f = mod[3:4], mod[4:5]

    h = _ln(x) * (1.0 + sc_a) + sh_a
    proj = _dot(h.astype(BF16), w_in_ref[...])
    xm = proj[:, :mw]
    z = proj[:, mw:2 * mw]
    u = proj[:, 2 * mw:2 * mw + gw]
    v = proj[:, 2 * mw + gw:]

    xm_ref[SUBLANES:SUBLANES + tm, :] = xm
    conv = jnp.broadcast_to(conv_b_ref[...], (tm, mw))
    for j in range(conv_k):
        off = SUBLANES - (conv_k - 1) + j
        conv = conv + conv_w_ref[j:j + 1, :] * xm_ref[off:off + tm, :]
    xm_ref[0:SUBLANES, :] = xm_ref[tm:tm + SUBLANES, :]
    xc = _silu(conv)

    scale = dh ** -0.5
    qs, ks, vs = [], [], []
    for hd in range(M_HEADS):
        sl = slice(hd * dh, (hd + 1) * dh)
        xch = xc[:, sl].astype(BF16)
        qs.append(_dot(xch, wq_ref[hd]))
        ks.append(_dot(xch, wk_ref[hd]) * scale)
        vs.append(_dot(xm[:, sl].astype(BF16), wv_ref[hd]))
    qkv = jnp.concatenate(qs + ks + vs, axis=1).astype(BF16)
    gate = _dot(qkv, wif_ref[...]) + bif_ref[...]
    gi = gate[:, :LANES]
    lf = _log_sigmoid(gate[:, LANES:])

    row = lax.broadcasted_iota(I32, (L, L), 0)
    col = lax.broadcasted_iota(I32, (L, L), 1)
    causal = col <= row
    tri = jnp.where(causal, 1.0, 0.0).astype(F32)
    ones_l = jnp.ones((L, dh), F32)

    for c in range(nch):
        rs = slice(c * L, (c + 1) * L)
        bmat = _dot_exact(tri, lf[rs])
        rmat = gi[rs] - bmat
        rmat_t = rmat.T
        for hd in range(M_HEADS):
            sl = slice(hd * dh, (hd + 1) * dh)
            bcol = bmat[:, hd:hd + 1]
            rcol = rmat[:, hd:hd + 1]
            rrow = rmat_t[hd:hd + 1, :]
            g = bmat[L - 1:L, hd:hd + 1]
            m_prev = m_ref[hd:hd + 1, 0:1]
            q = qs[hd][rs]
            k = ks[hd][rs]
            vv = vs[hd][rs]
            qb = q.astype(BF16)

            log_d = jnp.where(causal, bcol + rrow, NEG_INF)
            log_inter = bcol + m_prev
            m_t = jnp.maximum(log_inter, jnp.max(log_d, axis=1, keepdims=True))
            p = jnp.exp(log_d - m_t) * _dot_nt(qb, k.astype(BF16))
            w_inter = jnp.exp(log_inter - m_t)
            vext = jnp.concatenate([vv, ones_l], axis=1).astype(BF16)
            sx = state_ref[hd]
            out_ext = w_inter * _dot(qb, sx.astype(BF16)) + _dot(p.astype(BF16), vext)
            num = out_ext[:, :dh]
            nq = out_ext[:, dh:]
            hh = num / jnp.maximum(jnp.abs(nq), jnp.exp(-m_t))

            a = g + rcol
            m_new = jnp.maximum(g + m_prev, jnp.max(a, axis=0, keepdims=True))
            decay = jnp.exp(g + m_prev - m_new)
            kw = k * jnp.exp(a - m_new)
            state_ref[hd] = decay * sx + _dot(kw.T.astype(BF16), vext)
            m_ref[hd:hd + 1, :] = jnp.broadcast_to(m_new, (1, LANES))

            hc = _ln(hh) * mhg_ref[:, sl]
            hm = jax.nn.sigmoid(z[rs, sl]) * (hc + skip_ref[:, sl] * xc[rs, sl])
            cat_ref[rs, sl] = hm.astype(BF16)

    ug = _gelu(u)
    vg = _gelu(v)
    for gg in range(G_GROUPS):
        sl = slice(gg * gd, (gg + 1) * gd)
        vn = (_ln(vg[:, sl]) * sgg_ref[:, sl] + sgb_ref[:, sl]).astype(BF16)
        wsg = jnp.where(causal, wsp_ref[gg], 0.0).astype(BF16)
        bcol = bspt_ref[:, gg:gg + 1]
        for c in range(nch):
            rs = slice(c * L, (c + 1) * L)
            s = _dot(wsg, vn[rs]) + bcol
            cat_ref[rs, mw + gg * gd:mw + (gg + 1) * gd] = (ug[rs, sl] * s).astype(BF16)

    mix = _dot(cat_ref[...], w_out_ref[...])
    x1 = _ln(alpha * x + (1.0 + gt_a) * mix) * ln1g_ref[...] + ln1b_ref[...]
    x1_ref[0] = x1
    h2 = _ln(x1) * (1.0 + sc_f) + sh_f
    h2_ref[0] = h2
    h_hi = h2.astype(BF16)
    h_lo = (h2 - h_hi.astype(F32)).astype(BF16)
    wr_hi = wr_hi_ref[...]
    logt_ref[...] = _dot_nt(wr_hi, h_hi) + _dot_nt(wr_hi, h_lo) + _dot_nt(wr_lo_ref[...], h_hi)


def _mixer(x, mod, p, alpha):
    bsz, seq, d = x.shape
    tm = min(MIXER_TOKENS, seq)
    nt = seq // tm
    mw = p["conv_w"].shape[1]
    gw = p["sg_g"].shape[1]
    dh = mw // M_HEADS
    n_exp = p["wr_hi"].shape[0]
    const2 = lambda b, i: (0, 0)
    const3 = lambda b, i: (0, 0, 0)
    full = lambda a: pl.BlockSpec(a.shape, const2 if a.ndim == 2 else const3)
    names = ["w_in", "conv_w", "conv_b", "w_q", "w_k", "w_v", "w_if", "b_if", "mh_g", "skip", "sg_g",
             "sg_b", "w_sp", "b_sp_t", "w_out", "ln1_g", "ln1_b", "wr_hi", "wr_lo"]
    weights = [p[n] for n in names]
    return pl.pallas_call(
        functools.partial(_mixer_kernel, alpha=alpha),
        out_shape=(
            jax.ShapeDtypeStruct((bsz, seq, d), F32),
            jax.ShapeDtypeStruct((bsz, seq, d), F32),
            jax.ShapeDtypeStruct((n_exp, bsz * seq), F32),
        ),
        grid=(bsz, nt),
        in_specs=[
            pl.BlockSpec((1, tm, d), lambda b, i: (b, i, 0)),
            pl.BlockSpec((1, SUBLANES, d), lambda b, i: (b, 0, 0)),
        ] + [full(w) for w in weights],
        out_specs=(
            pl.BlockSpec((1, tm, d), lambda b, i: (b, i, 0)),
            pl.BlockSpec((1, tm, d), lambda b, i: (b, i, 0)),
            pl.BlockSpec((n_exp, tm), lambda b, i: (0, b * nt + i)),
        ),
        scratch_shapes=[
            pltpu.VMEM((M_HEADS, dh, 2 * dh), F32),
            pltpu.VMEM((SUBLANES, LANES), F32),
            pltpu.VMEM((tm + SUBLANES, mw), F32),
            pltpu.VMEM((tm, mw + gw), BF16),
        ],
        compiler_params=pltpu.CompilerParams(
            dimension_semantics=("arbitrary", "arbitrary"), vmem_limit_bytes=VMEM_LIMIT),
        name="mixer",
    )(x, mod, *weights)


def _route_kernel(logt_ref, bias_ref, upper_ref, idx_ref, w_ref, rank_ref, wt_ref, cnt_ref, carry_ref):
    n_exp, tr = logt_ref.shape
    epg = n_exp // N_GROUPS

    @pl.when(pl.program_id(0) == 0)
    def _():
        carry_ref[...] = jnp.zeros_like(carry_ref)

    scores = jax.nn.sigmoid(logt_ref[...])
    sel = scores + bias_ref[...]

    sel3 = sel.reshape(N_GROUPS, epg, tr)
    io3 = lax.broadcasted_iota(I32, (N_GROUPS, epg, tr), 1)
    m1 = jnp.max(sel3, axis=1, keepdims=True)
    first = jnp.min(jnp.where(sel3 == m1, io3, epg), axis=1, keepdims=True)
    m2 = jnp.max(jnp.where(io3 == first, NEG_INF, sel3), axis=1, keepdims=True)
    gs = (m1 + m2).reshape(N_GROUPS, tr)

    gio = lax.broadcasted_iota(I32, (N_GROUPS, tr), 0)
    gmask = jnp.zeros((N_GROUPS, tr), F32)
    for _ in range(TOPK_GROUPS):
        m = jnp.max(gs, axis=0, keepdims=True)
        gi = jnp.min(jnp.where(gs == m, gio, N_GROUPS), axis=0, keepdims=True)
        hit = gio == gi
        gmask = jnp.where(hit, 1.0, gmask)
        gs = jnp.where(hit, NEG_INF, gs)
    emask = jnp.broadcast_to(gmask.reshape(N_GROUPS, 1, tr), (N_GROUPS, epg, tr)).reshape(n_exp, tr)
    selm = jnp.where(emask > 0.0, sel, NEG_INF)

    eio = lax.broadcasted_iota(I32, (n_exp, tr), 0)
    chosen = jnp.zeros((n_exp, tr), F32)
    idx_rows, w_rows = [], []
    for _ in range(TOP_K):
        m = jnp.max(selm, axis=0, keepdims=True)
        ei = jnp.min(jnp.where(selm == m, eio, n_exp), axis=0, keepdims=True)
        hit = eio == ei
        w_rows.append(jnp.sum(jnp.where(hit, scores, 0.0), axis=0, keepdims=True))
        idx_rows.append(ei)
        selm = jnp.where(hit, NEG_INF, selm)
        chosen = jnp.where(hit, 1.0, chosen)

    chosen_b = chosen.astype(BF16)
    carry = carry_ref[...]
    ranks = carry[:, 0:1] + _dot(chosen_b, upper_ref[...])
    carry_new = carry + _dot(chosen_b, jnp.ones((tr, LANES), BF16))
    carry_ref[...] = carry_new
    cnt_ref[...] = carry_new
    rank_rows = [jnp.sum(jnp.where(eio == ei, ranks, 0.0), axis=0, keepdims=True) for ei in idx_rows]

    wsum = w_rows[0]
    for wk in w_rows[1:]:
        wsum = wsum + wk
    w8 = jnp.concatenate([wk / wsum * ROUTE_SCALE for wk in w_rows], axis=0)
    idx_ref[...] = jnp.concatenate(idx_rows, axis=0)
    rank_ref[...] = jnp.concatenate(rank_rows, axis=0).astype(I32)
    w_ref[...] = w8
    wpad = jnp.concatenate([w8, jnp.zeros((LANES - TOP_K, tr), F32)], axis=0)
    wt_ref[...] = wpad.T


def _route(logt, e_bias_col, upper):
    n_exp, t = logt.shape
    tr = upper.shape[0]
    return pl.pallas_call(
        _route_kernel,
        out_shape=(
            jax.ShapeDtypeStruct((TOP_K, t), I32),
            jax.ShapeDtypeStruct((TOP_K, t), F32),
            jax.ShapeDtypeStruct((TOP_K, t), I32),
            jax.ShapeDtypeStruct((t, LANES), F32),
            jax.ShapeDtypeStruct((n_exp, LANES), F32),
        ),
        grid=(t // tr,),
        in_specs=[
            pl.BlockSpec((n_exp, tr), lambda i: (0, i)),
            pl.BlockSpec((n_exp, 1), lambda i: (0, 0)),
            pl.BlockSpec((tr, tr), lambda i: (0, 0)),
        ],
        out_specs=(
            pl.BlockSpec((TOP_K, tr), lambda i: (0, i)),
            pl.BlockSpec((TOP_K, tr), lambda i: (0, i)),
            pl.BlockSpec((TOP_K, tr), lambda i: (0, i)),
            pl.BlockSpec((tr, LANES), lambda i: (i, 0)),
            pl.BlockSpec((n_exp, LANES), lambda i: (0, 0)),
        ),
        scratch_shapes=[pltpu.VMEM((n_exp, LANES), F32)],
        compiler_params=pltpu.CompilerParams(
            dimension_semantics=("arbitrary",), vmem_limit_bytes=VMEM_LIMIT),
        name="route",
    )(logt, e_bias_col, upper)


def _dest_kernel(idx_ref, rank_ref, start_ref, dest_ref):
    n_exp = start_ref.shape[0]
    tt = idx_ref.shape[1]
    eio = lax.broadcasted_iota(I32, (n_exp, tt), 0)
    start = start_ref[...]
    rows = []
    for k in range(TOP_K):
        hit = eio == idx_ref[k:k + 1, :]
        rows.append(jnp.sum(jnp.where(hit, start, 0), axis=0, keepdims=True) + rank_ref[k:k + 1, :])
    dest_ref[...] = jnp.concatenate(rows, axis=0)


def _dest(idx, rank, start_col):
    t = idx.shape[1]
    tt = min(DEST_TOKENS, t)
    n_exp = start_col.shape[0]
    return pl.pallas_call(
        _dest_kernel,
        out_shape=jax.ShapeDtypeStruct((TOP_K, t), I32),
        grid=(t // tt,),
        in_specs=[
            pl.BlockSpec((TOP_K, tt), lambda i: (0, i)),
            pl.BlockSpec((TOP_K, tt), lambda i: (0, i)),
            pl.BlockSpec((n_exp, 1), lambda i: (0, 0)),
        ],
        out_specs=pl.BlockSpec((TOP_K, tt), lambda i: (0, i)),
        compiler_params=pltpu.CompilerParams(vmem_limit_bytes=VMEM_LIMIT),
        name="dest",
    )(idx, rank, start_col)


def _rows_copy(hbm, sem, n_rows):
    return pltpu.make_async_copy(hbm.at[pl.ds(0, n_rows), :], hbm.at[pl.ds(0, n_rows), :], sem)


def _dispatch_kernel(dest_ref, h_hbm, xs_hbm, sem):
    td = dest_ref.shape[1]
    i = pl.program_id(0)
    n = pl.num_programs(0)
    slot = i % 2

    def body(t, carry):
        tok = i * td + t
        for k in range(TOP_K):
            pltpu.make_async_copy(h_hbm.at[pl.ds(tok, 1), :], xs_hbm.at[pl.ds(dest_ref[k, t], 1), :],
                                  sem.at[slot]).start()
        return carry

    lax.fori_loop(0, td, body, 0)

    @pl.when(i > 0)
    def _():
        _rows_copy(xs_hbm, sem.at[1 - slot], td * TOP_K).wait()

    @pl.when(i == n - 1)
    def _():
        _rows_copy(xs_hbm, sem.at[slot], td * TOP_K).wait()


def _dispatch(dest, h_flat):
    t, d = h_flat.shape
    td = min(DISPATCH_TOKENS, t)
    return pl.pallas_call(
        _dispatch_kernel,
        out_shape=jax.ShapeDtypeStruct((t * TOP_K, d), h_flat.dtype),
        grid=(t // td,),
        in_specs=[
            pl.BlockSpec((TOP_K, td), lambda i: (0, i), memory_space=pltpu.SMEM),
            pl.BlockSpec(memory_space=pl.ANY),
        ],
        out_specs=pl.BlockSpec(memory_space=pl.ANY),
        scratch_shapes=[pltpu.SemaphoreType.DMA((2,))],
        compiler_params=pltpu.CompilerParams(dimension_semantics=("arbitrary",)),
        name="dispatch",
    )(dest, h_flat)


def _experts_kernel(item_block, item_expert, item_first, n_items, starts,
                    xs_ref, wg_ref, wu_ref, wd_ref, ys_ref):
    i = pl.program_id(0)
    rows = xs_ref.shape[0]

    @pl.when(i < n_items[0])
    def _():
        e = item_expert[i]
        xb = xs_ref[...].astype(BF16)
        g = _dot(xb, wg_ref[0].astype(BF16))
        u = _dot(xb, wu_ref[0].astype(BF16))
        y = _dot((_silu(g) * u).astype(BF16), wd_ref[0].astype(BF16))
        r = item_block[i] * rows + lax.broadcasted_iota(I32, (rows, 1), 0)
        mine = (r >= starts[e]) & (r < starts[e + 1])

        @pl.when(item_first[i] == 1)
        def _():
            ys_ref[...] = jnp.where(mine, y, 0.0)

        @pl.when(item_first[i] == 0)
        def _():
            ys_ref[...] = jnp.where(mine, y, ys_ref[...])


def _experts(meta, xs, w_gate, w_up, w_down):
    item_block, item_expert, item_first, n_items, starts = meta
    n_rows, d = xs.shape
    de = w_gate.shape[2]
    n_max = item_block.shape[0]
    grid_spec = pltpu.PrefetchScalarGridSpec(
        num_scalar_prefetch=5,
        grid=(n_max,),
        in_specs=[
            pl.BlockSpec((EXPERT_ROWS, d), lambda i, ib, ie, fi, ni, st: (ib[i], 0)),
            pl.BlockSpec((1, d, de), lambda i, ib, ie, fi, ni, st: (ie[i], 0, 0)),
            pl.BlockSpec((1, d, de), lambda i, ib, ie, fi, ni, st: (ie[i], 0, 0)),
            pl.BlockSpec((1, de, d), lambda i, ib, ie, fi, ni, st: (ie[i], 0, 0)),
        ],
        out_specs=pl.BlockSpec((EXPERT_ROWS, d), lambda i, ib, ie, fi, ni, st: (ib[i], 0)),
    )
    return pl.pallas_call(
        _experts_kernel,
        out_shape=jax.ShapeDtypeStruct((n_rows, d), F32),
        grid_spec=grid_spec,
        compiler_params=pltpu.CompilerParams(
            dimension_semantics=("arbitrary",), vmem_limit_bytes=VMEM_LIMIT),
        name="experts",
    )(item_block, item_expert, item_first, n_items, starts, xs, w_gate, w_up, w_down)


def _expert_items(counts, n_rows):
    n_exp = counts.shape[0]
    n_blocks = n_rows // EXPERT_ROWS
    n_max = n_blocks + n_exp
    ends = jnp.cumsum(counts)
    starts = ends - counts
    first_b = starts // EXPERT_ROWS
    last_b = (ends - 1) // EXPERT_ROWS
    n_it = jnp.where(counts > 0, last_b - first_b + 1, 0)
    it_end = jnp.cumsum(n_it)
    it_start = it_end - n_it
    n_items = it_end[-1]
    ids = jnp.arange(n_max, dtype=I32)
    ids_c = jnp.minimum(ids, n_items - 1)
    e_of = jnp.minimum(jnp.searchsorted(it_end, ids_c, side="right"), n_exp - 1).astype(I32)
    b_of = (first_b[e_of] + ids_c - it_start[e_of]).astype(I32)
    prev_b = jnp.concatenate([jnp.full((1,), -1, I32), b_of[:-1]])
    first = (b_of != prev_b).astype(I32)
    starts_ext = jnp.concatenate([starts, ends[-1:]]).astype(I32)
    return b_of, e_of, first, n_items.reshape(1).astype(I32), starts_ext


def _combine_kernel(dest_cur, dest_nxt, x1_ref, mod_ref, wt_ref, wsg_ref, wsu_ref, wsd_ref,
                    ln2g_ref, ln2b_ref, ys_hbm, out_ref, buf, sem, *, alpha):
    tc = x1_ref.shape[0]
    i = pl.program_id(0)
    n = pl.num_programs(0)
    slot = i % 2

    def issue(dest_ref, s):
        def body(t, carry):
            for k in range(TOP_K):
                pltpu.make_async_copy(ys_hbm.at[pl.ds(dest_ref[k, t], 1), :],
                                      buf.at[s, k, pl.ds(t, 1), :], sem.at[s]).start()
            return carry
        lax.fori_loop(0, tc, body, 0)

    @pl.when(i == 0)
    def _():
        issue(dest_cur, 0)

    @pl.when(i + 1 < n)
    def _():
        issue(dest_nxt, 1 - slot)

    mod = mod_ref[0]
    sh_f, sc_f, gt_f = mod[3:4], mod[4:5], mod[5:6]
    x1 = x1_ref[...]
    hb = (_ln(x1) * (1.0 + sc_f) + sh_f).astype(BF16)
    mid = _silu(_dot(hb, wsg_ref[...])) * _dot(hb, wsu_ref[...])
    ffn = _dot(mid.astype(BF16), wsd_ref[...])

    for k in range(TOP_K):
        pltpu.make_async_copy(ys_hbm.at[pl.ds(0, tc), :], buf.at[slot, k], sem.at[slot]).wait()
    wt = wt_ref[...]
    routed = wt[:, 0:1] * buf[slot, 0]
    for k in range(1, TOP_K):
        routed = routed + wt[:, k:k + 1] * buf[slot, k]
    ffn = routed + ffn
    out_ref[...] = _ln(alpha * x1 + (1.0 + gt_f) * ffn) * ln2g_ref[...] + ln2b_ref[...]


def _combine(dest, x1_flat, mod, wt, ws_gate, ws_up, ws_down, ln2_g, ln2_b, ys, seq, alpha):
    t, d = x1_flat.shape
    tc = min(COMBINE_TOKENS, seq)
    n = t // tc
    per_seq = seq // tc
    const = lambda i: (0, 0)
    return pl.pallas_call(
        functools.partial(_combine_kernel, alpha=alpha),
        out_shape=jax.ShapeDtypeStruct((t, d), F32),
        grid=(n,),
        in_specs=[
            pl.BlockSpec((TOP_K, tc), lambda i: (0, i), memory_space=pltpu.SMEM),
            pl.BlockSpec((TOP_K, tc), lambda i: (0, jnp.minimum(i + 1, n - 1)), memory_space=pltpu.SMEM),
            pl.BlockSpec((tc, d), lambda i: (i, 0)),
            pl.BlockSpec((1, SUBLANES, d), lambda i: (i // per_seq, 0, 0)),
            pl.BlockSpec((tc, LANES), lambda i: (i, 0)),
            pl.BlockSpec(ws_gate.shape, const),
            pl.BlockSpec(ws_up.shape, const),
            pl.BlockSpec(ws_down.shape, const),
            pl.BlockSpec(ln2_g.shape, const),
            pl.BlockSpec(ln2_b.shape, const),
            pl.BlockSpec(memory_space=pl.ANY),
        ],
        out_specs=pl.BlockSpec((tc, d), lambda i: (i, 0)),
        scratch_shapes=[
            pltpu.VMEM((2, TOP_K, tc, d), F32),
            pltpu.SemaphoreType.DMA((2,)),
        ],
        compiler_params=pltpu.CompilerParams(
            dimension_semantics=("arbitrary",), vmem_limit_bytes=VMEM_LIMIT),
        name="combine",
    )(dest, dest, x1_flat, mod, wt, ws_gate, ws_up, ws_down, ln2_g, ln2_b, ys)


def _layer(x, c_pad, lw, alpha):
    bsz, seq, d = x.shape
    t = bsz * seq
    n_exp = lw["w_router"].shape[1]
    h_count = lw["w_if"].shape[1] // 2

    mod = _ada(c_pad, lw["w_ada"], lw["b_ada"].reshape(1, -1))[:bsz].reshape(bsz, 6, d)
    mod = jnp.pad(mod, ((0, 0), (0, SUBLANES - 6), (0, 0)))

    row2 = lambda a: a.reshape(1, -1)
    w_if = lw["w_if"]
    w_if_pad = jnp.zeros((w_if.shape[0], 2 * LANES), F32)
    w_if_pad = w_if_pad.at[:, :h_count].set(w_if[:, :h_count]).at[:, LANES:LANES + h_count].set(w_if[:, h_count:])
    b_if_pad = jnp.zeros((1, 2 * LANES), F32)
    b_if_pad = b_if_pad.at[0, :h_count].set(lw["b_if"][:h_count]).at[0, LANES:LANES + h_count].set(lw["b_if"][h_count:])
    w_rt = lw["w_router"].astype(F32).T
    wr_hi = w_rt.astype(BF16)
    wr_lo = (w_rt - wr_hi.astype(F32)).astype(BF16)
    p = {
        "w_in": lw["w_in"].astype(BF16), "conv_w": lw["conv_w"], "conv_b": row2(lw["conv_b"]),
        "w_q": lw["w_q"].astype(BF16), "w_k": lw["w_k"].astype(BF16), "w_v": lw["w_v"].astype(BF16),
        "w_if": w_if_pad.astype(BF16), "b_if": b_if_pad,
        "mh_g": row2(lw["mh_g"]), "skip": row2(lw["skip"]), "sg_g": row2(lw["sg_g"]), "sg_b": row2(lw["sg_b"]),
        "w_sp": lw["w_sp"], "b_sp_t": lw["b_sp"].T, "w_out": lw["w_out"].astype(BF16),
        "ln1_g": row2(lw["ln1_g"]), "ln1_b": row2(lw["ln1_b"]), "wr_hi": wr_hi, "wr_lo": wr_lo,
    }
    x1, h2, logt = _mixer(x, mod, p, alpha)

    tr = min(ROUTE_TOKENS, t)
    upper = (jnp.arange(tr)[:, None] < jnp.arange(tr)[None, :]).astype(BF16)
    idx, _, rank, wt, cnt = _route(logt, lw["e_bias"].astype(F32).reshape(n_exp, 1), upper)
    counts = cnt[:, 0].astype(I32)
    meta = _expert_items(counts, t * TOP_K)
    starts = meta[4]
    dest = _dest(idx, rank, starts[:n_exp].reshape(n_exp, 1))

    xs = _dispatch(dest, h2.reshape(t, d))
    ys = _experts(meta, xs, lw["w_gate"], lw["w_up"], lw["w_down"])
    out = _combine(dest, x1.reshape(t, d), mod, wt, lw["ws_gate"].astype(BF16), lw["ws_up"].astype(BF16),
                   lw["ws_down"].astype(BF16), row2(lw["ln2_g"]), row2(lw["ln2_b"]), ys, seq, alpha)
    return out.reshape(bsz, seq, d)


def kernel(x, c, w_ada, b_ada, w_in, conv_w, conv_b, w_q, w_k, w_v, w_if, b_if, mh_g, skip, sg_g, sg_b, w_sp, b_sp, w_out, ln1_g, ln1_b, w_router, e_bias, w_gate, w_up, w_down, ws_gate, ws_up, ws_down, ln2_g, ln2_b):
    stacked = dict(w_ada=w_ada, b_ada=b_ada, w_in=w_in, conv_w=conv_w, conv_b=conv_b, w_q=w_q, w_k=w_k,
                   w_v=w_v, w_if=w_if, b_if=b_if, mh_g=mh_g, skip=skip, sg_g=sg_g, sg_b=sg_b, w_sp=w_sp,
                   b_sp=b_sp, w_out=w_out, ln1_g=ln1_g, ln1_b=ln1_b, w_router=w_router, e_bias=e_bias,
                   w_gate=w_gate, w_up=w_up, w_down=w_down, ws_gate=ws_gate, ws_up=ws_up, ws_down=ws_down,
                   ln2_g=ln2_g, ln2_b=ln2_b)
    depth = w_ada.shape[0]
    alpha = float((2 * depth) ** 0.25)
    bsz = x.shape[0]
    c_pad = jnp.pad(c, ((0, -bsz % SUBLANES), (0, 0)))
    for l in range(depth):
        x = _layer(x, c_pad, {k: v[l] for k, v in stacked.items()}, alpha)
    return x
```

```python
import functools
import math

import jax
import jax.numpy as jnp
from jax import lax
from jax.experimental import pallas as pl
from jax.experimental.pallas import tpu as pltpu

F32 = jnp.float32
BF16 = jnp.bfloat16
I32 = jnp.int32

LN_EPS = 1e-5
M_HEADS = 4
G_GROUPS = 4
CHUNK = 128
N_GROUPS = 8
TOPK_GROUPS = 4
TOP_K = 8
ROUTE_SCALE = 2.5
LANES = 128
SUBLANES = 8

MIXER_TOKENS = 512
ROUTE_TOKENS = 1024
DEST_TOKENS = 2048
DISPATCH_TOKENS = 512
EXPERT_ROWS = 256
COMBINE_TOKENS = 256
VMEM_LIMIT = 56 * 1024 * 1024

NEG_INF = float("-inf")


def _ln(x):
    mu = jnp.mean(x, axis=-1, keepdims=True)
    xc = x - mu
    var = jnp.mean(xc * xc, axis=-1, keepdims=True)
    return xc * lax.rsqrt(var + LN_EPS)


def _dot(a, b):
    return jnp.dot(a, b, preferred_element_type=F32)


def _dot_nt(a, b):
    return lax.dot_general(a, b, (((1,), (1,)), ((), ())), preferred_element_type=F32)


def _dot_exact(a, b):
    return jnp.dot(a, b, preferred_element_type=F32, precision=lax.Precision.HIGHEST)


def _silu(x):
    return x * jax.nn.sigmoid(x)


def _gelu(x):
    return 0.5 * x * (1.0 + lax.erf(x * math.sqrt(0.5)))


def _log_sigmoid(x):
    return jnp.minimum(x, 0.0) - jnp.log1p(jnp.exp(-jnp.abs(x)))


def _store_slabs(ref, x):
    rows, d = x.shape
    assert d == SUBLANES * LANES
    for s in range(SUBLANES):
        ref[pl.ds(s, rows, stride=SUBLANES), :] = x[:, s * LANES:(s + 1) * LANES]


def _load_slabs(ref, rows):
    return [ref[pl.ds(s, rows, stride=SUBLANES), :] for s in range(SUBLANES)]


def _ada_kernel(c_ref, w_ref, b_ref, o_ref):
    o_ref[...] = _dot_exact(_silu(c_ref[...]), w_ref[...]) + b_ref[...]


def _ada(c_pad, w_ada, b_ada):
    rows, d = c_pad.shape
    n = w_ada.shape[1]
    return pl.pallas_call(
        _ada_kernel,
        out_shape=jax.ShapeDtypeStruct((rows, n), F32),
        grid=(n // d,),
        in_specs=[
            pl.BlockSpec((rows, d), lambda j: (0, 0)),
            pl.BlockSpec((d, d), lambda j: (0, j)),
            pl.BlockSpec((1, d), lambda j: (0, j)),
        ],
        out_specs=pl.BlockSpec((rows, d), lambda j: (0, j)),
        compiler_params=pltpu.CompilerParams(vmem_limit_bytes=VMEM_LIMIT),
        name="ada",
    )(c_pad, w_ada, b_ada)


def _mixer_kernel(x_ref, mod_ref, w_in_ref, conv_w_ref, conv_b_ref, wq_ref, wk_ref, wv_ref,
                  wif_ref, bif_ref, mhg_ref, skip_ref, sgg_ref, sgb_ref, wsp_ref, bspt_ref,
                  w_out_ref, ln1g_ref, ln1b_ref, wr_hi_ref, wr_lo_ref,
                  x1_ref, h2_ref, logt_ref,
                  state_ref, m_ref, xm_ref, cat_ref, *, alpha):
    tm = x_ref.shape[1]
    mw = conv_w_ref.shape[1]
    dh = mw // M_HEADS
    gw = sgg_ref.shape[1]
    gd = gw // G_GROUPS
    conv_k = conv_w_ref.shape[0]
    nch = tm // CHUNK
    L = CHUNK

    @pl.when(pl.program_id(1) == 0)
    def _():
        state_ref[...] = jnp.zeros_like(state_ref)
        m_ref[...] = jnp.zeros_like(m_ref)
        xm_ref[0:SUBLANES, :] = jnp.zeros((SUBLANES, mw), F32)

    x = x_ref[0]
    mod = mod_ref[0]
    sh_a, sc_a, gt_a = mod[0:1], mod[1:2], mod[2:3]
    sh_f, sc_f = mod[3:4], mod[4:5]

    h = _ln(x) * (1.0 + sc_a) + sh_a
    proj = _dot(h.astype(BF16), w_in_ref[...])
    xm = proj[:, :mw]
    z = proj[:, mw:2 * mw]
    u = proj[:, 2 * mw:2 * mw + gw]
    v = proj[:, 2 * mw + gw:]

    xm_ref[SUBLANES:SUBLANES + tm, :] = xm
    conv = jnp.broadcast_to(conv_b_ref[...], (tm, mw))
    for j in range(conv_k):
        off = SUBLANES - (conv_k - 1) + j
        conv = conv + conv_w_ref[j:j + 1, :] * xm_ref[off:off + tm, :]
    xm_ref[0:SUBLANES, :] = xm_ref[tm:tm + SUBLANES, :]
    xc = _silu(conv)

    scale = dh ** -0.5
    qs, ks, vs = [], [], []
    for hd in range(M_HEADS):
        sl = slice(hd * dh, (hd + 1) * dh)
        xch = xc[:, sl].astype(BF16)
        qs.append(_dot(xch, wq_ref[hd]))
        ks.append(_dot(xch, wk_ref[hd]) * scale)
        vs.append(_dot(xm[:, sl].astype(BF16), wv_ref[hd]))
    qkv = jnp.concatenate(qs + ks + vs, axis=1).astype(BF16)
    gate = _dot(qkv, wif_ref[...]) + bif_ref[...]
    gi = gate[:, :LANES]
    lf = _log_sigmoid(gate[:, LANES:])

    row = lax.broadcasted_iota(I32, (L, L), 0)
    col = lax.broadcasted_iota(I32, (L, L), 1)
    causal = col <= row
    tri = jnp.where(causal, 1.0, 0.0).astype(F32)
    ones_l = jnp.ones((L, dh), F32)

    for c in range(nch):
        rs = slice(c * L, (c + 1) * L)
        bmat = _dot_exact(tri, lf[rs])
        rmat = gi[rs] - bmat
        rmat_t = rmat.T
        for hd in range(M_HEADS):
            sl = slice(hd * dh, (hd + 1) * dh)
            bcol = bmat[:, hd:hd + 1]
            rcol = rmat[:, hd:hd + 1]
            rrow = rmat_t[hd:hd + 1, :]
            g = bmat[L - 1:L, hd:hd + 1]
            m_prev = m_ref[hd:hd + 1, 0:1]
            q = qs[hd][rs]
            k = ks[hd][rs]
            vv = vs[hd][rs]
            qb = q.astype(BF16)

            log_d = jnp.where(causal, bcol + rrow, NEG_INF)
            log_inter = bcol + m_prev
            m_t = jnp.maximum(log_inter, jnp.max(log_d, axis=1, keepdims=True))
            p = jnp.exp(log_d - m_t) * _dot_nt(qb, k.astype(BF16))
            w_inter = jnp.exp(log_inter - m_t)
            vext = jnp.concatenate([vv, ones_l], axis=1).astype(BF16)
            sx = state_ref[hd]
            out_ext = w_inter * _dot(qb, sx.astype(BF16)) + _dot(p.astype(BF16), vext)
            num = out_ext[:, :dh]
            nq = out_ext[:, dh:]
            hh = num / jnp.maximum(jnp.abs(nq), jnp.exp(-m_t))

            a = g + rcol
            m_new = jnp.maximum(g + m_prev, jnp.max(a, axis=0, keepdims=True))
            decay = jnp.exp(g + m_prev - m_new)
            kw = k * jnp.exp(a - m_new)
            state_ref[hd] = decay * sx + _dot(kw.T.astype(BF16), vext)
            m_ref[hd:hd + 1, :] = jnp.broadcast_to(m_new, (1, LANES))

            hc = _ln(hh) * mhg_ref[:, sl]
            hm = jax.nn.sigmoid(z[rs, sl]) * (hc + skip_ref[:, sl] * xc[rs, sl])
            cat_ref[rs, sl] = hm.astype(BF16)

    ug = _gelu(u)
    vg = _gelu(v)
    for gg in range(G_GROUPS):
        sl = slice(gg * gd, (gg + 1) * gd)
        vn = (_ln(vg[:, sl]) * sgg_ref[:, sl] + sgb_ref[:, sl]).astype(BF16)
        wsg = jnp.where(causal, wsp_ref[gg], 0.0).astype(BF16)
        bcol = bspt_ref[:, gg:gg + 1]
        for c in range(nch):
            rs = slice(c * L, (c + 1) * L)
            s = _dot(wsg, vn[rs]) + bcol
            cat_ref[rs, mw + gg * gd:mw + (gg + 1) * gd] = (ug[rs, sl] * s).astype(BF16)

    mix = _dot(cat_ref[...], w_out_ref[...])
    x1 = _ln(alpha * x + (1.0 + gt_a) * mix) * ln1g_ref[...] + ln1b_ref[...]
    x1_ref[0] = x1
    h2 = _ln(x1) * (1.0 + sc_f) + sh_f
    _store_slabs(h2_ref, h2)
    h_hi = h2.astype(BF16)
    h_lo = (h2 - h_hi.astype(F32)).astype(BF16)
    wr_hi = wr_hi_ref[...]
    logt_ref[...] = _dot_nt(wr_hi, h_hi) + _dot_nt(wr_hi, h_lo) + _dot_nt(wr_lo_ref[...], h_hi)


def _mixer(x, mod, p, alpha):
    bsz, seq, d = x.shape
    tm = min(MIXER_TOKENS, seq)
    nt = seq // tm
    mw = p["conv_w"].shape[1]
    gw = p["sg_g"].shape[1]
    dh = mw // M_HEADS
    n_exp = p["wr_hi"].shape[0]
    const2 = lambda b, i: (0, 0)
    const3 = lambda b, i: (0, 0, 0)
    full = lambda a: pl.BlockSpec(a.shape, const2 if a.ndim == 2 else const3)
    names = ["w_in", "conv_w", "conv_b", "w_q", "w_k", "w_v", "w_if", "b_if", "mh_g", "skip", "sg_g",
             "sg_b", "w_sp", "b_sp_t", "w_out", "ln1_g", "ln1_b", "wr_hi", "wr_lo"]
    weights = [p[n] for n in names]
    return pl.pallas_call(
        functools.partial(_mixer_kernel, alpha=alpha),
        out_shape=(
            jax.ShapeDtypeStruct((bsz, seq, d), F32),
            jax.ShapeDtypeStruct((bsz * seq * SUBLANES, LANES), F32),
            jax.ShapeDtypeStruct((n_exp, bsz * seq), F32),
        ),
        grid=(bsz, nt),
        in_specs=[
            pl.BlockSpec((1, tm, d), lambda b, i: (b, i, 0)),
            pl.BlockSpec((1, SUBLANES, d), lambda b, i: (b, 0, 0)),
        ] + [full(w) for w in weights],
        out_specs=(
            pl.BlockSpec((1, tm, d), lambda b, i: (b, i, 0)),
            pl.BlockSpec((tm * SUBLANES, LANES), lambda b, i: (b * nt + i, 0)),
            pl.BlockSpec((n_exp, tm), lambda b, i: (0, b * nt + i)),
        ),
        scratch_shapes=[
            pltpu.VMEM((M_HEADS, dh, 2 * dh), F32),
            pltpu.VMEM((SUBLANES, LANES), F32),
            pltpu.VMEM((tm + SUBLANES, mw), F32),
            pltpu.VMEM((tm, mw + gw), BF16),
        ],
        compiler_params=pltpu.CompilerParams(
            dimension_semantics=("arbitrary", "arbitrary"), vmem_limit_bytes=VMEM_LIMIT),
        name="mixer",
    )(x, mod, *weights)


def _route_kernel(logt_ref, bias_ref, upper_ref, idx_ref, w_ref, rank_ref, wt_ref, cnt_ref, carry_ref):
    n_exp, tr = logt_ref.shape
    epg = n_exp // N_GROUPS

    @pl.when(pl.program_id(0) == 0)
    def _():
        carry_ref[...] = jnp.zeros_like(carry_ref)

    scores = jax.nn.sigmoid(logt_ref[...])
    sel = scores + bias_ref[...]

    sel3 = sel.reshape(N_GROUPS, epg, tr)
    io3 = lax.broadcasted_iota(I32, (N_GROUPS, epg, tr), 1)
    m1 = jnp.max(sel3, axis=1, keepdims=True)
    first = jnp.min(jnp.where(sel3 == m1, io3, epg), axis=1, keepdims=True)
    m2 = jnp.max(jnp.where(io3 == first, NEG_INF, sel3), axis=1, keepdims=True)
    gs = (m1 + m2).reshape(N_GROUPS, tr)

    gio = lax.broadcasted_iota(I32, (N_GROUPS, tr), 0)
    gmask = jnp.zeros((N_GROUPS, tr), F32)
    for _ in range(TOPK_GROUPS):
        m = jnp.max(gs, axis=0, keepdims=True)
        gi = jnp.min(jnp.where(gs == m, gio, N_GROUPS), axis=0, keepdims=True)
        hit = gio == gi
        gmask = jnp.where(hit, 1.0, gmask)
        gs = jnp.where(hit, NEG_INF, gs)
    emask = jnp.broadcast_to(gmask.reshape(N_GROUPS, 1, tr), (N_GROUPS, epg, tr)).reshape(n_exp, tr)
    selm = jnp.where(emask > 0.0, sel, NEG_INF)

    eio = lax.broadcasted_iota(I32, (n_exp, tr), 0)
    chosen = jnp.zeros((n_exp, tr), F32)
    idx_rows, w_rows = [], []
    for _ in range(TOP_K):
        m = jnp.max(selm, axis=0, keepdims=True)
        ei = jnp.min(jnp.where(selm == m, eio, n_exp), axis=0, keepdims=True)
        hit = eio == ei
        w_rows.append(jnp.sum(jnp.where(hit, scores, 0.0), axis=0, keepdims=True))
        idx_rows.append(ei)
        selm = jnp.where(hit, NEG_INF, selm)
        chosen = jnp.where(hit, 1.0, chosen)

    chosen_b = chosen.astype(BF16)
    carry = carry_ref[...]
    ranks = carry[:, 0:1] + _dot(chosen_b, upper_ref[...])
    carry_new = carry + _dot(chosen_b, jnp.ones((tr, LANES), BF16))
    carry_ref[...] = carry_new
    cnt_ref[...] = carry_new
    rank_rows = [jnp.sum(jnp.where(eio == ei, ranks, 0.0), axis=0, keepdims=True) for ei in idx_rows]

    wsum = w_rows[0]
    for wk in w_rows[1:]:
        wsum = wsum + wk
    w8 = jnp.concatenate([wk / wsum * ROUTE_SCALE for wk in w_rows], axis=0)
    idx_ref[...] = jnp.concatenate(idx_rows, axis=0)
    rank_ref[...] = jnp.concatenate(rank_rows, axis=0).astype(I32)
    w_ref[...] = w8
    wpad = jnp.concatenate([w8, jnp.zeros((LANES - TOP_K, tr), F32)], axis=0)
    wt_ref[...] = wpad.T


def _route(logt, e_bias_col, upper):
    n_exp, t = logt.shape
    tr = upper.shape[0]
    return pl.pallas_call(
        _route_kernel,
        out_shape=(
            jax.ShapeDtypeStruct((TOP_K, t), I32),
            jax.ShapeDtypeStruct((TOP_K, t), F32),
            jax.ShapeDtypeStruct((TOP_K, t), I32),
            jax.ShapeDtypeStruct((t, LANES), F32),
            jax.ShapeDtypeStruct((n_exp, LANES), F32),
        ),
        grid=(t // tr,),
        in_specs=[
            pl.BlockSpec((n_exp, tr), lambda i: (0, i)),
            pl.BlockSpec((n_exp, 1), lambda i: (0, 0)),
            pl.BlockSpec((tr, tr), lambda i: (0, 0)),
        ],
        out_specs=(
            pl.BlockSpec((TOP_K, tr), lambda i: (0, i)),
            pl.BlockSpec((TOP_K, tr), lambda i: (0, i)),
            pl.BlockSpec((TOP_K, tr), lambda i: (0, i)),
            pl.BlockSpec((tr, LANES), lambda i: (i, 0)),
            pl.BlockSpec((n_exp, LANES), lambda i: (0, 0)),
        ),
        scratch_shapes=[pltpu.VMEM((n_exp, LANES), F32)],
        compiler_params=pltpu.CompilerParams(
            dimension_semantics=("arbitrary",), vmem_limit_bytes=VMEM_LIMIT),
        name="route",
    )(logt, e_bias_col, upper)


def _dest_kernel(idx_ref, rank_ref, start_ref, dest_ref):
    n_exp = start_ref.shape[0]
    tt = idx_ref.shape[1]
    eio = lax.broadcasted_iota(I32, (n_exp, tt), 0)
    start = start_ref[...]
    rows = []
    for k in range(TOP_K):
        hit = eio == idx_ref[k:k + 1, :]
        rows.append(jnp.sum(jnp.where(hit, start, 0), axis=0, keepdims=True) + rank_ref[k:k + 1, :])
    dest_ref[...] = jnp.concatenate(rows, axis=0) * SUBLANES


def _dest(idx, rank, start_col):
    t = idx.shape[1]
    tt = min(DEST_TOKENS, t)
    n_exp = start_col.shape[0]
    return pl.pallas_call(
        _dest_kernel,
        out_shape=jax.ShapeDtypeStruct((TOP_K, t), I32),
        grid=(t // tt,),
        in_specs=[
            pl.BlockSpec((TOP_K, tt), lambda i: (0, i)),
            pl.BlockSpec((TOP_K, tt), lambda i: (0, i)),
            pl.BlockSpec((n_exp, 1), lambda i: (0, 0)),
        ],
        out_specs=pl.BlockSpec((TOP_K, tt), lambda i: (0, i)),
        compiler_params=pltpu.CompilerParams(vmem_limit_bytes=VMEM_LIMIT),
        name="dest",
    )(idx, rank, start_col)


def _slab(ref, first_row):
    return ref.at[pl.ds(pl.multiple_of(first_row, SUBLANES), SUBLANES), :]


def _slabs_copy(hbm, sem, n_slabs):
    n = n_slabs * SUBLANES
    return pltpu.make_async_copy(hbm.at[pl.ds(0, n), :], hbm.at[pl.ds(0, n), :], sem)


def _dispatch_kernel(dest_ref, h_ref, xs_hbm, sem):
    td = dest_ref.shape[1]

    def body(t, carry):
        src = _slab(h_ref, t * SUBLANES)
        for k in range(TOP_K):
            pltpu.make_async_copy(src, _slab(xs_hbm, dest_ref[k, t]), sem.at[0]).start()
        return carry

    lax.fori_loop(0, td, body, 0)
    _slabs_copy(xs_hbm, sem.at[0], td * TOP_K).wait()


def _dispatch(dest, h_slabs):
    t = h_slabs.shape[0] // SUBLANES
    td = min(DISPATCH_TOKENS, t)
    return pl.pallas_call(
        _dispatch_kernel,
        out_shape=jax.ShapeDtypeStruct((t * TOP_K * SUBLANES, LANES), h_slabs.dtype),
        grid=(t // td,),
        in_specs=[
            pl.BlockSpec((TOP_K, td), lambda i: (0, i), memory_space=pltpu.SMEM),
            pl.BlockSpec((td * SUBLANES, LANES), lambda i: (i, 0)),
        ],
        out_specs=pl.BlockSpec(memory_space=pl.ANY),
        scratch_shapes=[pltpu.SemaphoreType.DMA((1,))],
        compiler_params=pltpu.CompilerParams(
            dimension_semantics=("arbitrary",), vmem_limit_bytes=VMEM_LIMIT),
        name="dispatch",
    )(dest, h_slabs)


def _experts_kernel(item_block, item_expert, item_first, item_new_expert, n_items, starts,
                    xs_ref, wg_ref, wu_ref, wd_ref, ys_ref, wg_b, wu_b, wd_b):
    i = pl.program_id(0)
    rows = EXPERT_ROWS

    @pl.when(i < n_items[0])
    def _():
        @pl.when(item_new_expert[i] == 1)
        def _():
            wg_b[...] = wg_ref[0].astype(BF16)
            wu_b[...] = wu_ref[0].astype(BF16)
            wd_b[...] = wd_ref[0].astype(BF16)

        e = item_expert[i]
        xb = jnp.concatenate(_load_slabs(xs_ref, rows), axis=1).astype(BF16)
        g = _dot(xb, wg_b[...])
        u = _dot(xb, wu_b[...])
        y = _dot((_silu(g) * u).astype(BF16), wd_b[...])
        r = item_block[i] * rows + lax.broadcasted_iota(I32, (rows, 1), 0)
        mine = (r >= starts[e]) & (r < starts[e + 1])

        @pl.when(item_first[i] == 1)
        def _():
            _store_slabs(ys_ref, jnp.where(mine, y, 0.0))

        @pl.when(item_first[i] == 0)
        def _():
            old = jnp.concatenate(_load_slabs(ys_ref, rows), axis=1)
            _store_slabs(ys_ref, jnp.where(mine, y, old))


def _experts(meta, xs, w_gate, w_up, w_down):
    item_block, item_expert, item_first, item_new_expert, n_items, starts = meta
    _, d, de = w_gate.shape
    n_max = item_block.shape[0]
    slab_rows = EXPERT_ROWS * SUBLANES
    grid_spec = pltpu.PrefetchScalarGridSpec(
        num_scalar_prefetch=6,
        grid=(n_max,),
        in_specs=[
            pl.BlockSpec((slab_rows, LANES), lambda i, ib, ie, fi, ne, ni, st: (ib[i], 0)),
            pl.BlockSpec((1, d, de), lambda i, ib, ie, fi, ne, ni, st: (ie[i], 0, 0)),
            pl.BlockSpec((1, d, de), lambda i, ib, ie, fi, ne, ni, st: (ie[i], 0, 0)),
            pl.BlockSpec((1, de, d), lambda i, ib, ie, fi, ne, ni, st: (ie[i], 0, 0)),
        ],
        out_specs=pl.BlockSpec((slab_rows, LANES), lambda i, ib, ie, fi, ne, ni, st: (ib[i], 0)),
        scratch_shapes=[
            pltpu.VMEM((d, de), BF16),
            pltpu.VMEM((d, de), BF16),
            pltpu.VMEM((de, d), BF16),
        ],
    )
    return pl.pallas_call(
        _experts_kernel,
        out_shape=jax.ShapeDtypeStruct(xs.shape, F32),
        grid_spec=grid_spec,
        compiler_params=pltpu.CompilerParams(
            dimension_semantics=("arbitrary",), vmem_limit_bytes=VMEM_LIMIT),
        name="experts",
    )(item_block, item_expert, item_first, item_new_expert, n_items, starts, xs, w_gate, w_up, w_down)


def _expert_items(counts, n_rows):
    n_exp = counts.shape[0]
    n_blocks = n_rows // EXPERT_ROWS
    n_max = n_blocks + n_exp
    ends = jnp.cumsum(counts)
    starts = ends - counts
    first_b = starts // EXPERT_ROWS
    last_b = (ends - 1) // EXPERT_ROWS
    n_it = jnp.where(counts > 0, last_b - first_b + 1, 0)
    it_end = jnp.cumsum(n_it)
    it_start = it_end - n_it
    n_items = it_end[-1]
    ids = jnp.arange(n_max, dtype=I32)
    ids_c = jnp.minimum(ids, n_items - 1)
    owner = (ids_c[:, None] >= it_start[None, :]) & (ids_c[:, None] < it_end[None, :])
    e_of = jnp.sum(jnp.where(owner, jnp.arange(n_exp, dtype=I32)[None, :], 0), axis=1).astype(I32)
    b_of = (jnp.sum(jnp.where(owner, (first_b - it_start)[None, :], 0), axis=1) + ids_c).astype(I32)
    prev_b = jnp.concatenate([jnp.full((1,), -1, I32), b_of[:-1]])
    first = (b_of != prev_b).astype(I32)
    prev_e = jnp.concatenate([jnp.full((1,), -1, I32), e_of[:-1]])
    new_expert = (e_of != prev_e).astype(I32)
    starts_ext = jnp.concatenate([starts, ends[-1:]]).astype(I32)
    return b_of, e_of, first, new_expert, n_items.reshape(1).astype(I32), starts_ext


def _combine_kernel(dest_cur, dest_nxt, x1_ref, mod_ref, wt_ref, wsg_ref, wsu_ref, wsd_ref,
                    ln2g_ref, ln2b_ref, ys_hbm, out_ref, buf, sem, *, alpha):
    tc = x1_ref.shape[0]
    i = pl.program_id(0)
    n = pl.num_programs(0)
    slot = i % 2

    def issue(dest_ref, s):
        def body(t, carry):
            for k in range(TOP_K):
                pltpu.make_async_copy(_slab(ys_hbm, dest_ref[k, t]), _slab(buf.at[s, k], t * SUBLANES),
                                      sem.at[s]).start()
            return carry
        lax.fori_loop(0, tc, body, 0)

    @pl.when(i == 0)
    def _():
        issue(dest_cur, 0)

    @pl.when(i + 1 < n)
    def _():
        issue(dest_nxt, 1 - slot)

    mod = mod_ref[0]
    sh_f, sc_f, gt_f = mod[3:4], mod[4:5], mod[5:6]
    x1 = x1_ref[...]
    hb = (_ln(x1) * (1.0 + sc_f) + sh_f).astype(BF16)
    mid = _silu(_dot(hb, wsg_ref[...])) * _dot(hb, wsu_ref[...])
    ffn = _dot(mid.astype(BF16), wsd_ref[...])

    for k in range(TOP_K):
        pltpu.make_async_copy(ys_hbm.at[pl.ds(0, tc * SUBLANES), :], buf.at[slot, k], sem.at[slot]).wait()
    wt = wt_ref[...]
    routed = None
    for k in range(TOP_K):
        yk = wt[:, k:k + 1] * jnp.concatenate(_load_slabs(buf.at[slot, k], tc), axis=1)
        routed = yk if routed is None else routed + yk
    ffn = routed + ffn
    out_ref[...] = _ln(alpha * x1 + (1.0 + gt_f) * ffn) * ln2g_ref[...] + ln2b_ref[...]


def _combine(dest, x1_flat, mod, wt, ws_gate, ws_up, ws_down, ln2_g, ln2_b, ys, seq, alpha):
    t, d = x1_flat.shape
    tc = min(COMBINE_TOKENS, seq)
    n = t // tc
    per_seq = seq // tc
    const = lambda i: (0, 0)
    return pl.pallas_call(
        functools.partial(_combine_kernel, alpha=alpha),
        out_shape=jax.ShapeDtypeStruct((t, d), F32),
        grid=(n,),
        in_specs=[
            pl.BlockSpec((TOP_K, tc), lambda i: (0, i), memory_space=pltpu.SMEM),
            pl.BlockSpec((TOP_K, tc), lambda i: (0, jnp.minimum(i + 1, n - 1)), memory_space=pltpu.SMEM),
            pl.BlockSpec((tc, d), lambda i: (i, 0)),
            pl.BlockSpec((1, SUBLANES, d), lambda i: (i // per_seq, 0, 0)),
            pl.BlockSpec((tc, LANES), lambda i: (i, 0)),
            pl.BlockSpec(ws_gate.shape, const),
            pl.BlockSpec(ws_up.shape, const),
            pl.BlockSpec(ws_down.shape, const),
            pl.BlockSpec(ln2_g.shape, const),
            pl.BlockSpec(ln2_b.shape, const),
            pl.BlockSpec(memory_space=pl.ANY),
        ],
        out_specs=pl.BlockSpec((tc, d), lambda i: (i, 0)),
        scratch_shapes=[
            pltpu.VMEM((2, TOP_K, tc * SUBLANES, LANES), F32),
            pltpu.SemaphoreType.DMA((2,)),
        ],
        compiler_params=pltpu.CompilerParams(
            dimension_semantics=("arbitrary",), vmem_limit_bytes=VMEM_LIMIT),
        name="combine",
    )(dest, dest, x1_flat, mod, wt, ws_gate, ws_up, ws_down, ln2_g, ln2_b, ys)


def _layer(x, c_pad, lw, alpha):
    bsz, seq, d = x.shape
    t = bsz * seq
    n_exp = lw["w_router"].shape[1]
    h_count = lw["w_if"].shape[1] // 2

    mod = _ada(c_pad, lw["w_ada"], lw["b_ada"].reshape(1, -1))[:bsz].reshape(bsz, 6, d)
    mod = jnp.pad(mod, ((0, 0), (0, SUBLANES - 6), (0, 0)))

    row2 = lambda a: a.reshape(1, -1)
    w_if = lw["w_if"]
    w_if_pad = jnp.zeros((w_if.shape[0], 2 * LANES), F32)
    w_if_pad = w_if_pad.at[:, :h_count].set(w_if[:, :h_count]).at[:, LANES:LANES + h_count].set(w_if[:, h_count:])
    b_if_pad = jnp.zeros((1, 2 * LANES), F32)
    b_if_pad = b_if_pad.at[0, :h_count].set(lw["b_if"][:h_count]).at[0, LANES:LANES + h_count].set(lw["b_if"][h_count:])
    w_rt = lw["w_router"].astype(F32).T
    wr_hi = w_rt.astype(BF16)
    wr_lo = (w_rt - wr_hi.astype(F32)).astype(BF16)
    p = {
        "w_in": lw["w_in"].astype(BF16), "conv_w": lw["conv_w"], "conv_b": row2(lw["conv_b"]),
        "w_q": lw["w_q"].astype(BF16), "w_k": lw["w_k"].astype(BF16), "w_v": lw["w_v"].astype(BF16),
        "w_if": w_if_pad.astype(BF16), "b_if": b_if_pad,
        "mh_g": row2(lw["mh_g"]), "skip": row2(lw["skip"]), "sg_g": row2(lw["sg_g"]), "sg_b": row2(lw["sg_b"]),
        "w_sp": lw["w_sp"], "b_sp_t": lw["b_sp"].T, "w_out": lw["w_out"].astype(BF16),
        "ln1_g": row2(lw["ln1_g"]), "ln1_b": row2(lw["ln1_b"]), "wr_hi": wr_hi, "wr_lo": wr_lo,
    }
    x1, h2, logt = _mixer(x, mod, p, alpha)

    tr = min(ROUTE_TOKENS, t)
    upper = (jnp.arange(tr)[:, None] < jnp.arange(tr)[None, :]).astype(BF16)
    idx, _, rank, wt, cnt = _route(logt, lw["e_bias"].astype(F32).reshape(n_exp, 1), upper)
    counts = cnt[:, 0].astype(I32)
    meta = _expert_items(counts, t * TOP_K)
    starts = meta[-1]
    dest = _dest(idx, rank, starts[:n_exp].reshape(n_exp, 1))

    xs = _dispatch(dest, h2)
    ys = _experts(meta, xs, lw["w_gate"], lw["w_up"], lw["w_down"])
    out = _combine(dest, x1.reshape(t, d), mod, wt, lw["ws_gate"].astype(BF16), lw["ws_up"].astype(BF16),
                   lw["ws_down"].astype(BF16), row2(lw["ln2_g"]), row2(lw["ln2_b"]), ys, seq, alpha)
    return out.reshape(bsz, seq, d)


def kernel(x, c, w_ada, b_ada, w_in, conv_w, conv_b, w_q, w_k, w_v, w_if, b_if, mh_g, skip, sg_g, sg_b, w_sp, b_sp, w_out, ln1_g, ln1_b, w_router, e_bias, w_gate, w_up, w_down, ws_gate, ws_up, ws_down, ln2_g, ln2_b):
    stacked = dict(w_ada=w_ada, b_ada=b_ada, w_in=w_in, conv_w=conv_w, conv_b=conv_b, w_q=w_q, w_k=w_k,
                   w_v=w_v, w_if=w_if, b_if=b_if, mh_g=mh_g, skip=skip, sg_g=sg_g, sg_b=sg_b, w_sp=w_sp,
                   b_sp=b_sp, w_out=w_out, ln1_g=ln1_g, ln1_b=ln1_b, w_router=w_router, e_bias=e_bias,
                   w_gate=w_gate, w_up=w_up, w_down=w_down, ws_gate=ws_gate, ws_up=ws_up, ws_down=ws_down,
                   ln2_g=ln2_g, ln2_b=ln2_b)
    depth = w_ada.shape[0]
    alpha = float((2 * depth) ** 0.25)
    bsz = x.shape[0]
    c_pad = jnp.pad(c, ((0, -bsz % SUBLANES), (0, 0)))
    for l in range(depth):
        x = _layer(x, c_pad, {k: v[l] for k, v in stacked.items()}, alpha)
    return x
```

```python
import functools
import math

import jax
import jax.numpy as jnp
from jax import lax
from jax.experimental import pallas as pl
from jax.experimental.pallas import tpu as pltpu

F32 = jnp.float32
BF16 = jnp.bfloat16
I32 = jnp.int32

LN_EPS = 1e-5
M_HEADS = 4
G_GROUPS = 4
CHUNK = 128
N_GROUPS = 8
TOPK_GROUPS = 4
TOP_K = 8
ROUTE_SCALE = 2.5
LANES = 128
SUBLANES = 8

MIXER_TOKENS = 512
ROUTE_TOKENS = 1024
DEST_TOKENS = 2048
DISPATCH_TOKENS = 512
EXPERT_ROWS = 512
COMBINE_TOKENS = 256
VMEM_LIMIT = 56 * 1024 * 1024

NEG_INF = float("-inf")


def _ln(x):
    mu = jnp.mean(x, axis=-1, keepdims=True)
    xc = x - mu
    var = jnp.mean(xc * xc, axis=-1, keepdims=True)
    return xc * lax.rsqrt(var + LN_EPS)


def _dot(a, b):
    return jnp.dot(a, b, preferred_element_type=F32)


def _dot_nt(a, b):
    return lax.dot_general(a, b, (((1,), (1,)), ((), ())), preferred_element_type=F32)


def _dot_exact(a, b):
    return jnp.dot(a, b, preferred_element_type=F32, precision=lax.Precision.HIGHEST)


def _silu(x):
    return x * jax.nn.sigmoid(x)


def _gelu(x):
    return 0.5 * x * (1.0 + lax.erf(x * math.sqrt(0.5)))


def _log_sigmoid(x):
    return jnp.minimum(x, 0.0) - jnp.log1p(jnp.exp(-jnp.abs(x)))


SLAB = SUBLANES


def _split_slabs(x):
    rows, d = x.shape
    assert d == SLAB * LANES
    return [x[:, s * LANES:(s + 1) * LANES] for s in range(SLAB)]


def _store_slabs(ref, parts):
    rows = parts[0].shape[0]
    for s in range(SLAB):
        ref[pl.ds(s, rows, stride=SLAB), :] = parts[s]


def _load_slabs(ref, rows):
    return [ref[pl.ds(s, rows, stride=SLAB), :] for s in range(SLAB)]


def _join_slabs(parts):
    return jnp.concatenate(parts, axis=1)


def _ada_kernel(c_ref, w_ref, b_ref, o_ref):
    o_ref[...] = _dot_exact(_silu(c_ref[...]), w_ref[...]) + b_ref[...]


def _ada(c_pad, w_ada, b_ada):
    rows, d = c_pad.shape
    n = w_ada.shape[1]
    return pl.pallas_call(
        _ada_kernel,
        out_shape=jax.ShapeDtypeStruct((rows, n), F32),
        grid=(n // d,),
        in_specs=[
            pl.BlockSpec((rows, d), lambda j: (0, 0)),
            pl.BlockSpec((d, d), lambda j: (0, j)),
            pl.BlockSpec((1, d), lambda j: (0, j)),
        ],
        out_specs=pl.BlockSpec((rows, d), lambda j: (0, j)),
        compiler_params=pltpu.CompilerParams(vmem_limit_bytes=VMEM_LIMIT),
        name="ada",
    )(c_pad, w_ada, b_ada)


def _mixer_kernel(x_ref, mod_ref, w_in_ref, conv_w_ref, conv_b_ref, wq_ref, wk_ref, wv_ref,
                  wif_ref, bif_ref, mhg_ref, skip_ref, sgg_ref, sgb_ref, wsp_ref, bspt_ref,
                  w_out_ref, ln1g_ref, ln1b_ref, wr_hi_ref, wr_lo_ref,
                  x1_ref, h2_ref, logt_ref,
                  state_ref, m_ref, xm_ref, cat_ref, *, alpha):
    tm = x_ref.shape[1]
    mw = conv_w_ref.shape[1]
    dh = mw // M_HEADS
    gw = sgg_ref.shape[1]
    gd = gw // G_GROUPS
    conv_k = conv_w_ref.shape[0]
    nch = tm // CHUNK
    L = CHUNK

    @pl.when(pl.program_id(1) == 0)
    def _():
        state_ref[...] = jnp.zeros_like(state_ref)
        m_ref[...] = jnp.zeros_like(m_ref)
        xm_ref[0:SUBLANES, :] = jnp.zeros((SUBLANES, mw), F32)

    x = x_ref[0]
    mod = mod_ref[0]
    sh_a, sc_a, gt_a = mod[0:1], mod[1:2], mod[2:3]
    sh_f, sc_f = mod[3:4], mod[4:5]

    h = _ln(x) * (1.0 + sc_a) + sh_a
    proj = _dot(h.astype(BF16), w_in_ref[...])
    xm = proj[:, :mw]
    z = proj[:, mw:2 * mw]
    u = proj[:, 2 * mw:2 * mw + gw]
    v = proj[:, 2 * mw + gw:]

    xm_ref[SUBLANES:SUBLANES + tm, :] = xm
    conv = jnp.broadcast_to(conv_b_ref[...], (tm, mw))
    for j in range(conv_k):
        off = SUBLANES - (conv_k - 1) + j
        conv = conv + conv_w_ref[j:j + 1, :] * xm_ref[off:off + tm, :]
    xm_ref[0:SUBLANES, :] = xm_ref[tm:tm + SUBLANES, :]
    xc = _silu(conv)

    scale = dh ** -0.5
    qs, ks, vs = [], [], []
    for hd in range(M_HEADS):
        sl = slice(hd * dh, (hd + 1) * dh)
        xch = xc[:, sl].astype(BF16)
        qs.append(_dot(xch, wq_ref[hd]))
        ks.append(_dot(xch, wk_ref[hd]) * scale)
        vs.append(_dot(xm[:, sl].astype(BF16), wv_ref[hd]))
    qkv = jnp.concatenate(qs + ks + vs, axis=1).astype(BF16)
    gate = _dot(qkv, wif_ref[...]) + bif_ref[...]
    gi = gate[:, :LANES]
    lf = _log_sigmoid(gate[:, LANES:])

    row = lax.broadcasted_iota(I32, (L, L), 0)
    col = lax.broadcasted_iota(I32, (L, L), 1)
    causal = col <= row
    tri = jnp.where(causal, 1.0, 0.0).astype(F32)
    ones_l = jnp.ones((L, dh), F32)

    for c in range(nch):
        rs = slice(c * L, (c + 1) * L)
        bmat = _dot_exact(tri, lf[rs])
        rmat = gi[rs] - bmat
        rmat_t = rmat.T
        for hd in range(M_HEADS):
            sl = slice(hd * dh, (hd + 1) * dh)
            bcol = bmat[:, hd:hd + 1]
            rcol = rmat[:, hd:hd + 1]
            rrow = rmat_t[hd:hd + 1, :]
            g = bmat[L - 1:L, hd:hd + 1]
            m_prev = m_ref[hd:hd + 1, 0:1]
            q = qs[hd][rs]
            k = ks[hd][rs]
            vv = vs[hd][rs]
            qb = q.astype(BF16)

            log_d = jnp.where(causal, bcol + rrow, NEG_INF)
            log_inter = bcol + m_prev
            m_t = jnp.maximum(log_inter, jnp.max(log_d, axis=1, keepdims=True))
            p = jnp.exp(log_d - m_t) * _dot_nt(qb, k.astype(BF16))
            w_inter = jnp.exp(log_inter - m_t)
            vext = jnp.concatenate([vv, ones_l], axis=1).astype(BF16)
            sx = state_ref[hd]
            out_ext = w_inter * _dot(qb, sx.astype(BF16)) + _dot(p.astype(BF16), vext)
            num = out_ext[:, :dh]
            nq = out_ext[:, dh:]
            hh = num / jnp.maximum(jnp.abs(nq), jnp.exp(-m_t))

            a = g + rcol
            m_new = jnp.maximum(g + m_prev, jnp.max(a, axis=0, keepdims=True))
            decay = jnp.exp(g + m_prev - m_new)
            kw = k * jnp.exp(a - m_new)
            state_ref[hd] = decay * sx + _dot(kw.T.astype(BF16), vext)
            m_ref[hd:hd + 1, :] = jnp.broadcast_to(m_new, (1, LANES))

            hc = _ln(hh) * mhg_ref[:, sl]
            hm = jax.nn.sigmoid(z[rs, sl]) * (hc + skip_ref[:, sl] * xc[rs, sl])
            cat_ref[rs, sl] = hm.astype(BF16)

    ug = _gelu(u)
    vg = _gelu(v)
    for gg in range(G_GROUPS):
        sl = slice(gg * gd, (gg + 1) * gd)
        vn = (_ln(vg[:, sl]) * sgg_ref[:, sl] + sgb_ref[:, sl]).astype(BF16)
        wsg = jnp.where(causal, wsp_ref[gg], 0.0).astype(BF16)
        bcol = bspt_ref[:, gg:gg + 1]
        for c in range(nch):
            rs = slice(c * L, (c + 1) * L)
            s = _dot(wsg, vn[rs]) + bcol
            cat_ref[rs, mw + gg * gd:mw + (gg + 1) * gd] = (ug[rs, sl] * s).astype(BF16)

    mix = _dot(cat_ref[...], w_out_ref[...])
    x1 = _ln(alpha * x + (1.0 + gt_a) * mix) * ln1g_ref[...] + ln1b_ref[...]
    x1_ref[0] = x1
    h2 = _ln(x1) * (1.0 + sc_f) + sh_f
    _store_slabs(h2_ref, _split_slabs(h2))
    h_hi = h2.astype(BF16)
    h_lo = (h2 - h_hi.astype(F32)).astype(BF16)
    wr_hi = wr_hi_ref[...]
    logt_ref[...] = _dot_nt(wr_hi, h_hi) + _dot_nt(wr_hi, h_lo) + _dot_nt(wr_lo_ref[...], h_hi)


def _mixer(x, mod, p, alpha):
    bsz, seq, d = x.shape
    tm = min(MIXER_TOKENS, seq)
    nt = seq // tm
    mw = p["conv_w"].shape[1]
    gw = p["sg_g"].shape[1]
    dh = mw // M_HEADS
    n_exp = p["wr_hi"].shape[0]
    const2 = lambda b, i: (0, 0)
    const3 = lambda b, i: (0, 0, 0)
    full = lambda a: pl.BlockSpec(a.shape, const2 if a.ndim == 2 else const3)
    names = ["w_in", "conv_w", "conv_b", "w_q", "w_k", "w_v", "w_if", "b_if", "mh_g", "skip", "sg_g",
             "sg_b", "w_sp", "b_sp_t", "w_out", "ln1_g", "ln1_b", "wr_hi", "wr_lo"]
    weights = [p[n] for n in names]
    return pl.pallas_call(
        functools.partial(_mixer_kernel, alpha=alpha),
        out_shape=(
            jax.ShapeDtypeStruct((bsz, seq, d), F32),
            jax.ShapeDtypeStruct((bsz * seq * SLAB, LANES), F32),
            jax.ShapeDtypeStruct((n_exp, bsz * seq), F32),
        ),
        grid=(bsz, nt),
        in_specs=[
            pl.BlockSpec((1, tm, d), lambda b, i: (b, i, 0)),
            pl.BlockSpec((1, SUBLANES, d), lambda b, i: (b, 0, 0)),
        ] + [full(w) for w in weights],
        out_specs=(
            pl.BlockSpec((1, tm, d), lambda b, i: (b, i, 0)),
            pl.BlockSpec((tm * SLAB, LANES), lambda b, i: (b * nt + i, 0)),
            pl.BlockSpec((n_exp, tm), lambda b, i: (0, b * nt + i)),
        ),
        scratch_shapes=[
            pltpu.VMEM((M_HEADS, dh, 2 * dh), F32),
            pltpu.VMEM((SUBLANES, LANES), F32),
            pltpu.VMEM((tm + SUBLANES, mw), F32),
            pltpu.VMEM((tm, mw + gw), BF16),
        ],
        compiler_params=pltpu.CompilerParams(
            dimension_semantics=("arbitrary", "arbitrary"), vmem_limit_bytes=VMEM_LIMIT),
        name="mixer",
    )(x, mod, *weights)


def _route_kernel(logt_ref, bias_ref, upper_ref, idx_ref, w_ref, rank_ref, wt_ref, cnt_ref, carry_ref):
    n_exp, tr = logt_ref.shape
    epg = n_exp // N_GROUPS

    @pl.when(pl.program_id(0) == 0)
    def _():
        carry_ref[...] = jnp.zeros_like(carry_ref)

    scores = jax.nn.sigmoid(logt_ref[...])
    sel = scores + bias_ref[...]

    sel3 = sel.reshape(N_GROUPS, epg, tr)
    io3 = lax.broadcasted_iota(I32, (N_GROUPS, epg, tr), 1)
    m1 = jnp.max(sel3, axis=1, keepdims=True)
    first = jnp.min(jnp.where(sel3 == m1, io3, epg), axis=1, keepdims=True)
    m2 = jnp.max(jnp.where(io3 == first, NEG_INF, sel3), axis=1, keepdims=True)
    gs = (m1 + m2).reshape(N_GROUPS, tr)

    gio = lax.broadcasted_iota(I32, (N_GROUPS, tr), 0)
    gmask = jnp.zeros((N_GROUPS, tr), F32)
    for _ in range(TOPK_GROUPS):
        m = jnp.max(gs, axis=0, keepdims=True)
        gi = jnp.min(jnp.where(gs == m, gio, N_GROUPS), axis=0, keepdims=True)
        hit = gio == gi
        gmask = jnp.where(hit, 1.0, gmask)
        gs = jnp.where(hit, NEG_INF, gs)
    emask = jnp.broadcast_to(gmask.reshape(N_GROUPS, 1, tr), (N_GROUPS, epg, tr)).reshape(n_exp, tr)
    selm = jnp.where(emask > 0.0, sel, NEG_INF)

    eio = lax.broadcasted_iota(I32, (n_exp, tr), 0)
    chosen = jnp.zeros((n_exp, tr), F32)
    idx_rows, w_rows = [], []
    for _ in range(TOP_K):
        m = jnp.max(selm, axis=0, keepdims=True)
        ei = jnp.min(jnp.where(selm == m, eio, n_exp), axis=0, keepdims=True)
        hit = eio == ei
        w_rows.append(jnp.sum(jnp.where(hit, scores, 0.0), axis=0, keepdims=True))
        idx_rows.append(ei)
        selm = jnp.where(hit, NEG_INF, selm)
        chosen = jnp.where(hit, 1.0, chosen)

    chosen_b = chosen.astype(BF16)
    carry = carry_ref[...]
    ranks = carry[:, 0:1] + _dot(chosen_b, upper_ref[...])
    carry_new = carry + _dot(chosen_b, jnp.ones((tr, LANES), BF16))
    carry_ref[...] = carry_new
    cnt_ref[...] = carry_new
    rank_rows = [jnp.sum(jnp.where(eio == ei, ranks, 0.0), axis=0, keepdims=True) for ei in idx_rows]

    wsum = w_rows[0]
    for wk in w_rows[1:]:
        wsum = wsum + wk
    w8 = jnp.concatenate([wk / wsum * ROUTE_SCALE for wk in w_rows], axis=0)
    idx_ref[...] = jnp.concatenate(idx_rows, axis=0)
    rank_ref[...] = jnp.concatenate(rank_rows, axis=0).astype(I32)
    w_ref[...] = w8
    wpad = jnp.concatenate([w8, jnp.zeros((LANES - TOP_K, tr), F32)], axis=0)
    wt_ref[...] = wpad.T


def _route(logt, e_bias_col, upper):
    n_exp, t = logt.shape
    tr = upper.shape[0]
    return pl.pallas_call(
        _route_kernel,
        out_shape=(
            jax.ShapeDtypeStruct((TOP_K, t), I32),
            jax.ShapeDtypeStruct((TOP_K, t), F32),
            jax.ShapeDtypeStruct((TOP_K, t), I32),
            jax.ShapeDtypeStruct((t, LANES), F32),
            jax.ShapeDtypeStruct((n_exp, LANES), F32),
        ),
        grid=(t // tr,),
        in_specs=[
            pl.BlockSpec((n_exp, tr), lambda i: (0, i)),
            pl.BlockSpec((n_exp, 1), lambda i: (0, 0)),
            pl.BlockSpec((tr, tr), lambda i: (0, 0)),
        ],
        out_specs=(
            pl.BlockSpec((TOP_K, tr), lambda i: (0, i)),
            pl.BlockSpec((TOP_K, tr), lambda i: (0, i)),
            pl.BlockSpec((TOP_K, tr), lambda i: (0, i)),
            pl.BlockSpec((tr, LANES), lambda i: (i, 0)),
            pl.BlockSpec((n_exp, LANES), lambda i: (0, 0)),
        ),
        scratch_shapes=[pltpu.VMEM((n_exp, LANES), F32)],
        compiler_params=pltpu.CompilerParams(
            dimension_semantics=("arbitrary",), vmem_limit_bytes=VMEM_LIMIT),
        name="route",
    )(logt, e_bias_col, upper)


def _dest_kernel(idx_ref, rank_ref, start_ref, dest_ref):
    n_exp = start_ref.shape[0]
    tt = idx_ref.shape[1]
    eio = lax.broadcasted_iota(I32, (n_exp, tt), 0)
    start = start_ref[...]
    rows = []
    for k in range(TOP_K):
        hit = eio == idx_ref[k:k + 1, :]
        rows.append(jnp.sum(jnp.where(hit, start, 0), axis=0, keepdims=True) + rank_ref[k:k + 1, :])
    dest_ref[...] = jnp.concatenate(rows, axis=0) * SLAB


def _dest(idx, rank, start_col):
    t = idx.shape[1]
    tt = min(DEST_TOKENS, t)
    n_exp = start_col.shape[0]
    return pl.pallas_call(
        _dest_kernel,
        out_shape=jax.ShapeDtypeStruct((TOP_K, t), I32),
        grid=(t // tt,),
        in_specs=[
            pl.BlockSpec((TOP_K, tt), lambda i: (0, i)),
            pl.BlockSpec((TOP_K, tt), lambda i: (0, i)),
            pl.BlockSpec((n_exp, 1), lambda i: (0, 0)),
        ],
        out_specs=pl.BlockSpec((TOP_K, tt), lambda i: (0, i)),
        compiler_params=pltpu.CompilerParams(vmem_limit_bytes=VMEM_LIMIT),
        name="dest",
    )(idx, rank, start_col)


def _slab(ref, first_row):
    return ref.at[pl.ds(pl.multiple_of(first_row, SLAB), SLAB), :]


def _slabs_copy(hbm, sem, n_slabs):
    n = n_slabs * SLAB
    return pltpu.make_async_copy(hbm.at[pl.ds(0, n), :], hbm.at[pl.ds(0, n), :], sem)


def _dispatch_kernel(dest_ref, h_ref, xs_hbm, sem):
    td = dest_ref.shape[1]

    def body(t, carry):
        src = _slab(h_ref, t * SLAB)
        for k in range(TOP_K):
            pltpu.make_async_copy(src, _slab(xs_hbm, dest_ref[k, t]), sem.at[0]).start(priority=k % 2)
        return carry

    lax.fori_loop(0, td, body, 0)
    _slabs_copy(xs_hbm, sem.at[0], td * TOP_K).wait()


def _dispatch(dest, h_slabs):
    t = h_slabs.shape[0] // SLAB
    td = min(DISPATCH_TOKENS, t)
    return pl.pallas_call(
        _dispatch_kernel,
        out_shape=jax.ShapeDtypeStruct((t * TOP_K * SLAB, LANES), h_slabs.dtype),
        grid=(t // td,),
        in_specs=[
            pl.BlockSpec((TOP_K, td), lambda i: (0, i), memory_space=pltpu.SMEM),
            pl.BlockSpec((td * SLAB, LANES), lambda i: (i, 0)),
        ],
        out_specs=pl.BlockSpec(memory_space=pl.ANY),
        scratch_shapes=[pltpu.SemaphoreType.DMA((1,))],
        compiler_params=pltpu.CompilerParams(
            dimension_semantics=("arbitrary",), vmem_limit_bytes=VMEM_LIMIT),
        name="dispatch",
    )(dest, h_slabs)


def _experts_kernel(item_block, item_expert, item_first, item_new_expert, n_items, starts,
                    xs_ref, wg_ref, wu_ref, wd_ref, ys_ref, wg_b, wu_b, wd_b):
    i = pl.program_id(0)
    rows = EXPERT_ROWS

    @pl.when(i < n_items[0])
    def _():
        @pl.when(item_new_expert[i] == 1)
        def _():
            wg_b[...] = wg_ref[0].astype(BF16)
            wu_b[...] = wu_ref[0].astype(BF16)
            wd_b[...] = wd_ref[0].astype(BF16)

        e = item_expert[i]
        xb = _join_slabs(_load_slabs(xs_ref, rows)).astype(BF16)
        g = _dot(xb, wg_b[...])
        u = _dot(xb, wu_b[...])
        y = _split_slabs(_dot((_silu(g) * u).astype(BF16), wd_b[...]))
        r = item_block[i] * rows + lax.broadcasted_iota(I32, (rows, 1), 0)
        mine = (r >= starts[e]) & (r < starts[e + 1])

        @pl.when(item_first[i] == 1)
        def _():
            _store_slabs(ys_ref, [jnp.where(mine, w, 0.0) for w in y])

        @pl.when(item_first[i] == 0)
        def _():
            old = _load_slabs(ys_ref, rows)
            _store_slabs(ys_ref, [jnp.where(mine, w, o) for w, o in zip(y, old)])


def _experts(meta, xs, w_gate, w_up, w_down):
    item_block, item_expert, item_first, item_new_expert, n_items, starts = meta
    _, d, de = w_gate.shape
    n_max = item_block.shape[0]
    slab_rows = EXPERT_ROWS * SLAB
    grid_spec = pltpu.PrefetchScalarGridSpec(
        num_scalar_prefetch=6,
        grid=(n_max,),
        in_specs=[
            pl.BlockSpec((slab_rows, LANES), lambda i, ib, ie, fi, ne, ni, st: (ib[i], 0)),
            pl.BlockSpec((1, d, de), lambda i, ib, ie, fi, ne, ni, st: (ie[i], 0, 0)),
            pl.BlockSpec((1, d, de), lambda i, ib, ie, fi, ne, ni, st: (ie[i], 0, 0)),
            pl.BlockSpec((1, de, d), lambda i, ib, ie, fi, ne, ni, st: (ie[i], 0, 0)),
        ],
        out_specs=pl.BlockSpec((slab_rows, LANES), lambda i, ib, ie, fi, ne, ni, st: (ib[i], 0)),
        scratch_shapes=[
            pltpu.VMEM((d, de), BF16),
            pltpu.VMEM((d, de), BF16),
            pltpu.VMEM((de, d), BF16),
        ],
    )
    return pl.pallas_call(
        _experts_kernel,
        out_shape=jax.ShapeDtypeStruct(xs.shape, xs.dtype),
        grid_spec=grid_spec,
        compiler_params=pltpu.CompilerParams(
            dimension_semantics=("arbitrary",), vmem_limit_bytes=VMEM_LIMIT),
        name="experts",
    )(item_block, item_expert, item_first, item_new_expert, n_items, starts, xs, w_gate, w_up, w_down)


def _expert_items(counts, n_rows):
    n_exp = counts.shape[0]
    n_blocks = n_rows // EXPERT_ROWS
    n_max = n_blocks + n_exp
    ends = jnp.cumsum(counts)
    starts = ends - counts
    first_b = starts // EXPERT_ROWS
    last_b = (ends - 1) // EXPERT_ROWS
    n_it = jnp.where(counts > 0, last_b - first_b + 1, 0)
    it_end = jnp.cumsum(n_it)
    it_start = it_end - n_it
    n_items = it_end[-1]
    ids = jnp.arange(n_max, dtype=I32)
    ids_c = jnp.minimum(ids, n_items - 1)
    owner = (ids_c[:, None] >= it_start[None, :]) & (ids_c[:, None] < it_end[None, :])
    e_of = jnp.sum(jnp.where(owner, jnp.arange(n_exp, dtype=I32)[None, :], 0), axis=1).astype(I32)
    b_of = (jnp.sum(jnp.where(owner, (first_b - it_start)[None, :], 0), axis=1) + ids_c).astype(I32)
    prev_b = jnp.concatenate([jnp.full((1,), -1, I32), b_of[:-1]])
    first = (b_of != prev_b).astype(I32)
    prev_e = jnp.concatenate([jnp.full((1,), -1, I32), e_of[:-1]])
    new_expert = (e_of != prev_e).astype(I32)
    starts_ext = jnp.concatenate([starts, ends[-1:]]).astype(I32)
    return b_of, e_of, first, new_expert, n_items.reshape(1).astype(I32), starts_ext


def _combine_kernel(dest_cur, dest_nxt, x1_ref, mod_ref, wt_ref, wsg_ref, wsu_ref, wsd_ref,
                    ln2g_ref, ln2b_ref, ys_hbm, out_ref, buf, sem, *, alpha):
    tc = x1_ref.shape[0]
    i = pl.program_id(0)
    n = pl.num_programs(0)
    slot = i % 2

    def issue(dest_ref, s):
        def body(t, carry):
            for k in range(TOP_K):
                pltpu.make_async_copy(_slab(ys_hbm, dest_ref[k, t]), _slab(buf.at[s, k], t * SLAB),
                                      sem.at[s]).start(priority=k % 2)
            return carry
        lax.fori_loop(0, tc, body, 0)

    @pl.when(i == 0)
    def _():
        issue(dest_cur, 0)

    @pl.when(i + 1 < n)
    def _():
        issue(dest_nxt, 1 - slot)

    mod = mod_ref[0]
    sh_f, sc_f, gt_f = mod[3:4], mod[4:5], mod[5:6]
    x1 = x1_ref[...]
    hb = (_ln(x1) * (1.0 + sc_f) + sh_f).astype(BF16)
    mid = _silu(_dot(hb, wsg_ref[...])) * _dot(hb, wsu_ref[...])
    ffn = _dot(mid.astype(BF16), wsd_ref[...])

    for k in range(TOP_K):
        pltpu.make_async_copy(ys_hbm.at[pl.ds(0, tc * SLAB), :], buf.at[slot, k], sem.at[slot]).wait()
    wt = wt_ref[...]
    routed = None
    for k in range(TOP_K):
        yk = wt[:, k:k + 1] * _join_slabs(_load_slabs(buf.at[slot, k], tc))
        routed = yk if routed is None else routed + yk
    ffn = routed + ffn
    out_ref[...] = _ln(alpha * x1 + (1.0 + gt_f) * ffn) * ln2g_ref[...] + ln2b_ref[...]


def _combine(dest, x1_flat, mod, wt, ws_gate, ws_up, ws_down, ln2_g, ln2_b, ys, seq, alpha):
    t, d = x1_flat.shape
    tc = min(COMBINE_TOKENS, seq)
    n = t // tc
    per_seq = seq // tc
    const = lambda i: (0, 0)
    return pl.pallas_call(
        functools.partial(_combine_kernel, alpha=alpha),
        out_shape=jax.ShapeDtypeStruct((t, d), F32),
        grid=(n,),
        in_specs=[
            pl.BlockSpec((TOP_K, tc), lambda i: (0, i), memory_space=pltpu.SMEM),
            pl.BlockSpec((TOP_K, tc), lambda i: (0, jnp.minimum(i + 1, n - 1)), memory_space=pltpu.SMEM),
            pl.BlockSpec((tc, d), lambda i: (i, 0)),
            pl.BlockSpec((1, SUBLANES, d), lambda i: (i // per_seq, 0, 0)),
            pl.BlockSpec((tc, LANES), lambda i: (i, 0)),
            pl.BlockSpec(ws_gate.shape, const),
            pl.BlockSpec(ws_up.shape, const),
            pl.BlockSpec(ws_down.shape, const),
            pl.BlockSpec(ln2_g.shape, const),
            pl.BlockSpec(ln2_b.shape, const),
            pl.BlockSpec(memory_space=pl.ANY),
        ],
        out_specs=pl.BlockSpec((tc, d), lambda i: (i, 0)),
        scratch_shapes=[
            pltpu.VMEM((2, TOP_K, tc * SLAB, LANES), F32),
            pltpu.SemaphoreType.DMA((2,)),
        ],
        compiler_params=pltpu.CompilerParams(
            dimension_semantics=("arbitrary",), vmem_limit_bytes=VMEM_LIMIT),
        name="combine",
    )(dest, dest, x1_flat, mod, wt, ws_gate, ws_up, ws_down, ln2_g, ln2_b, ys)


def _layer(x, c_pad, lw, alpha):
    bsz, seq, d = x.shape
    t = bsz * seq
    n_exp = lw["w_router"].shape[1]
    h_count = lw["w_if"].shape[1] // 2

    mod = _ada(c_pad, lw["w_ada"], lw["b_ada"].reshape(1, -1))[:bsz].reshape(bsz, 6, d)
    mod = jnp.pad(mod, ((0, 0), (0, SUBLANES - 6), (0, 0)))

    row2 = lambda a: a.reshape(1, -1)
    w_if = lw["w_if"]
    w_if_pad = jnp.zeros((w_if.shape[0], 2 * LANES), F32)
    w_if_pad = w_if_pad.at[:, :h_count].set(w_if[:, :h_count]).at[:, LANES:LANES + h_count].set(w_if[:, h_count:])
    b_if_pad = jnp.zeros((1, 2 * LANES), F32)
    b_if_pad = b_if_pad.at[0, :h_count].set(lw["b_if"][:h_count]).at[0, LANES:LANES + h_count].set(lw["b_if"][h_count:])
    w_rt = lw["w_router"].astype(F32).T
    wr_hi = w_rt.astype(BF16)
    wr_lo = (w_rt - wr_hi.astype(F32)).astype(BF16)
    p = {
        "w_in": lw["w_in"].astype(BF16), "conv_w": lw["conv_w"], "conv_b": row2(lw["conv_b"]),
        "w_q": lw["w_q"].astype(BF16), "w_k": lw["w_k"].astype(BF16), "w_v": lw["w_v"].astype(BF16),
        "w_if": w_if_pad.astype(BF16), "b_if": b_if_pad,
        "mh_g": row2(lw["mh_g"]), "skip": row2(lw["skip"]), "sg_g": row2(lw["sg_g"]), "sg_b": row2(lw["sg_b"]),
        "w_sp": lw["w_sp"], "b_sp_t": lw["b_sp"].T, "w_out": lw["w_out"].astype(BF16),
        "ln1_g": row2(lw["ln1_g"]), "ln1_b": row2(lw["ln1_b"]), "wr_hi": wr_hi, "wr_lo": wr_lo,
    }
    x1, h2, logt = _mixer(x, mod, p, alpha)

    tr = min(ROUTE_TOKENS, t)
    upper = (jnp.arange(tr)[:, None] < jnp.arange(tr)[None, :]).astype(BF16)
    idx, _, rank, wt, cnt = _route(logt, lw["e_bias"].astype(F32).reshape(n_exp, 1), upper)
    counts = cnt[:, 0].astype(I32)
    meta = _expert_items(counts, t * TOP_K)
    starts = meta[-1]
    dest = _dest(idx, rank, starts[:n_exp].reshape(n_exp, 1))

    xs = _dispatch(dest, h2)
    ys = _experts(meta, xs, lw["w_gate"], lw["w_up"], lw["w_down"])
    out = _combine(dest, x1.reshape(t, d), mod, wt, lw["ws_gate"].astype(BF16), lw["ws_up"].astype(BF16),
                   lw["ws_down"].astype(BF16), row2(lw["ln2_g"]), row2(lw["ln2_b"]), ys, seq, alpha)
    return out.reshape(bsz, seq, d)


def kernel(x, c, w_ada, b_ada, w_in, conv_w, conv_b, w_q, w_k, w_v, w_if, b_if, mh_g, skip, sg_g, sg_b, w_sp, b_sp, w_out, ln1_g, ln1_b, w_router, e_bias, w_gate, w_up, w_down, ws_gate, ws_up, ws_down, ln2_g, ln2_b):
    stacked = dict(w_ada=w_ada, b_ada=b_ada, w_in=w_in, conv_w=conv_w, conv_b=conv_b, w_q=w_q, w_k=w_k,
                   w_v=w_v, w_if=w_if, b_if=b_if, mh_g=mh_g, skip=skip, sg_g=sg_g, sg_b=sg_b, w_sp=w_sp,
                   b_sp=b_sp, w_out=w_out, ln1_g=ln1_g, ln1_b=ln1_b, w_router=w_router, e_bias=e_bias,
                   w_gate=w_gate, w_up=w_up, w_down=w_down, ws_gate=ws_gate, ws_up=ws_up, ws_down=ws_down,
                   ln2_g=ln2_g, ln2_b=ln2_b)
    depth = w_ada.shape[0]
    alpha = float((2 * depth) ** 0.25)
    bsz = x.shape[0]
    c_pad = jnp.pad(c, ((0, -bsz % SUBLANES), (0, 0)))
    for l in range(depth):
        x = _layer(x, c_pad, {k: v[l] for k, v in stacked.items()}, alpha)
    return x
```

```python
import functools
import math

import jax
import jax.numpy as jnp
from jax import lax
from jax.experimental import pallas as pl
from jax.experimental.pallas import tpu as pltpu

F32 = jnp.float32
BF16 = jnp.bfloat16
I32 = jnp.int32

LN_EPS = 1e-5
M_HEADS = 4
G_GROUPS = 4
CHUNK = 128
N_GROUPS = 8
TOPK_GROUPS = 4
TOP_K = 8
ROUTE_SCALE = 2.5
LANES = 128
SUBLANES = 8

MIXER_TOKENS = 512
ROUTE_TOKENS = 1024
DEST_TOKENS = 2048
DISPATCH_TOKENS = 512
EXPERT_ROWS = 512
COMBINE_TOKENS = 256
VMEM_LIMIT = 56 * 1024 * 1024

NEG_INF = float("-inf")


def _ln(x):
    mu = jnp.mean(x, axis=-1, keepdims=True)
    xc = x - mu
    var = jnp.mean(xc * xc, axis=-1, keepdims=True)
    return xc * lax.rsqrt(var + LN_EPS)


def _dot(a, b):
    return jnp.dot(a, b, preferred_element_type=F32)


def _dot_nt(a, b):
    return lax.dot_general(a, b, (((1,), (1,)), ((), ())), preferred_element_type=F32)


def _dot_exact(a, b):
    return jnp.dot(a, b, preferred_element_type=F32, precision=lax.Precision.HIGHEST)


def _silu(x):
    return x * jax.nn.sigmoid(x)


def _gelu(x):
    return 0.5 * x * (1.0 + lax.erf(x * math.sqrt(0.5)))


def _log_sigmoid(x):
    return jnp.minimum(x, 0.0) - jnp.log1p(jnp.exp(-jnp.abs(x)))


SLAB = SUBLANES


def _split_slabs(x):
    rows, d = x.shape
    assert d == SLAB * LANES
    return [x[:, s * LANES:(s + 1) * LANES] for s in range(SLAB)]


def _store_slabs(ref, parts):
    rows = parts[0].shape[0]
    for s in range(SLAB):
        ref[pl.ds(s, rows, stride=SLAB), :] = parts[s]


def _load_slabs(ref, rows):
    return [ref[pl.ds(s, rows, stride=SLAB), :] for s in range(SLAB)]


def _join_slabs(parts):
    return jnp.concatenate(parts, axis=1)


def _ada_kernel(c_ref, w_ref, b_ref, o_ref):
    o_ref[...] = _dot_exact(_silu(c_ref[...]), w_ref[...]) + b_ref[...]


def _ada(c_pad, w_ada, b_ada):
    rows, d = c_pad.shape
    n = w_ada.shape[1]
    return pl.pallas_call(
        _ada_kernel,
        out_shape=jax.ShapeDtypeStruct((rows, n), F32),
        grid=(n // d,),
        in_specs=[
            pl.BlockSpec((rows, d), lambda j: (0, 0)),
            pl.BlockSpec((d, d), lambda j: (0, j)),
            pl.BlockSpec((1, d), lambda j: (0, j)),
        ],
        out_specs=pl.BlockSpec((rows, d), lambda j: (0, j)),
        compiler_params=pltpu.CompilerParams(vmem_limit_bytes=VMEM_LIMIT),
        name="ada",
    )(c_pad, w_ada, b_ada)


def _mixer_kernel(x_ref, mod_ref, w_in_ref, conv_w_ref, conv_b_ref, wq_ref, wk_ref, wv_ref,
                  wif_ref, bif_ref, mhg_ref, skip_ref, sgg_ref, sgb_ref, wsp_ref, bspt_ref,
                  w_out_ref, ln1g_ref, ln1b_ref, wr_hi_ref, wr_lo_ref,
                  x1_ref, h2_ref, logt_ref,
                  state_ref, m_ref, xm_ref, cat_ref, *, alpha):
    tm = x_ref.shape[1]
    mw = conv_w_ref.shape[1]
    dh = mw // M_HEADS
    gw = sgg_ref.shape[1]
    gd = gw // G_GROUPS
    conv_k = conv_w_ref.shape[0]
    nch = tm // CHUNK
    L = CHUNK

    @pl.when(pl.program_id(1) == 0)
    def _():
        state_ref[...] = jnp.zeros_like(state_ref)
        m_ref[...] = jnp.zeros_like(m_ref)
        xm_ref[0:SUBLANES, :] = jnp.zeros((SUBLANES, mw), F32)

    x = x_ref[0]
    mod = mod_ref[0]
    sh_a, sc_a, gt_a = mod[0:1], mod[1:2], mod[2:3]
    sh_f, sc_f = mod[3:4], mod[4:5]

    h = _ln(x) * (1.0 + sc_a) + sh_a
    proj = _dot(h.astype(BF16), w_in_ref[...])
    xm = proj[:, :mw]
    z = proj[:, mw:2 * mw]
    u = proj[:, 2 * mw:2 * mw + gw]
    v = proj[:, 2 * mw + gw:]

    xm_ref[SUBLANES:SUBLANES + tm, :] = xm
    conv = jnp.broadcast_to(conv_b_ref[...], (tm, mw))
    for j in range(conv_k):
        off = SUBLANES - (conv_k - 1) + j
        conv = conv + conv_w_ref[j:j + 1, :] * xm_ref[off:off + tm, :]
    xm_ref[0:SUBLANES, :] = xm_ref[tm:tm + SUBLANES, :]
    xc = _silu(conv)

    scale = dh ** -0.5
    qs, ks, vs = [], [], []
    for hd in range(M_HEADS):
        sl = slice(hd * dh, (hd + 1) * dh)
        xch = xc[:, sl].astype(BF16)
        qs.append(_dot(xch, wq_ref[hd]))
        ks.append(_dot(xch, wk_ref[hd]) * scale)
        vs.append(_dot(xm[:, sl].astype(BF16), wv_ref[hd]))
    qkv = jnp.concatenate(qs + ks + vs, axis=1).astype(BF16)
    gate = _dot(qkv, wif_ref[...]) + bif_ref[...]
    gi = gate[:, :LANES]
    lf = _log_sigmoid(gate[:, LANES:])

    row = lax.broadcasted_iota(I32, (L, L), 0)
    col = lax.broadcasted_iota(I32, (L, L), 1)
    causal = col <= row
    tri = jnp.where(causal, 1.0, 0.0).astype(F32)
    ones_l = jnp.ones((L, dh), F32)

    for c in range(nch):
        rs = slice(c * L, (c + 1) * L)
        bmat = _dot_exact(tri, lf[rs])
        rmat = gi[rs] - bmat
        rmat_t = rmat.T
        for hd in range(M_HEADS):
            sl = slice(hd * dh, (hd + 1) * dh)
            bcol = bmat[:, hd:hd + 1]
            rcol = rmat[:, hd:hd + 1]
            rrow = rmat_t[hd:hd + 1, :]
            g = bmat[L - 1:L, hd:hd + 1]
            m_prev = m_ref[hd:hd + 1, 0:1]
            q = qs[hd][rs]
            k = ks[hd][rs]
            vv = vs[hd][rs]
            qb = q.astype(BF16)

            log_d = jnp.where(causal, bcol + rrow, NEG_INF)
            log_inter = bcol + m_prev
            m_t = jnp.maximum(log_inter, jnp.max(log_d, axis=1, keepdims=True))
            p = jnp.exp(log_d - m_t) * _dot_nt(qb, k.astype(BF16))
            w_inter = jnp.exp(log_inter - m_t)
            vext = jnp.concatenate([vv, ones_l], axis=1).astype(BF16)
            sx = state_ref[hd]
            out_ext = w_inter * _dot(qb, sx.astype(BF16)) + _dot(p.astype(BF16), vext)
            num = out_ext[:, :dh]
            nq = out_ext[:, dh:]
            hh = num / jnp.maximum(jnp.abs(nq), jnp.exp(-m_t))

            a = g + rcol
            m_new = jnp.maximum(g + m_prev, jnp.max(a, axis=0, keepdims=True))
            decay = jnp.exp(g + m_prev - m_new)
            kw = k * jnp.exp(a - m_new)
            state_ref[hd] = decay * sx + _dot(kw.T.astype(BF16), vext)
            m_ref[hd:hd + 1, :] = jnp.broadcast_to(m_new, (1, LANES))

            hc = _ln(hh) * mhg_ref[:, sl]
            hm = jax.nn.sigmoid(z[rs, sl]) * (hc + skip_ref[:, sl] * xc[rs, sl])
            cat_ref[rs, sl] = hm.astype(BF16)

    ug = _gelu(u)
    vg = _gelu(v)
    for gg in range(G_GROUPS):
        sl = slice(gg * gd, (gg + 1) * gd)
        vn = (_ln(vg[:, sl]) * sgg_ref[:, sl] + sgb_ref[:, sl]).astype(BF16)
        wsg = jnp.where(causal, wsp_ref[gg], 0.0).astype(BF16)
        bcol = bspt_ref[:, gg:gg + 1]
        for c in range(nch):
            rs = slice(c * L, (c + 1) * L)
            s = _dot(wsg, vn[rs]) + bcol
            cat_ref[rs, mw + gg * gd:mw + (gg + 1) * gd] = (ug[rs, sl] * s).astype(BF16)

    mix = _dot(cat_ref[...], w_out_ref[...])
    x1 = _ln(alpha * x + (1.0 + gt_a) * mix) * ln1g_ref[...] + ln1b_ref[...]
    x1_ref[0] = x1
    h2 = _ln(x1) * (1.0 + sc_f) + sh_f
    _store_slabs(h2_ref, _split_slabs(h2))
    h_hi = h2.astype(BF16)
    h_lo = (h2 - h_hi.astype(F32)).astype(BF16)
    wr_hi = wr_hi_ref[...]
    logt_ref[...] = _dot_nt(wr_hi, h_hi) + _dot_nt(wr_hi, h_lo) + _dot_nt(wr_lo_ref[...], h_hi)


def _mixer(x, mod, p, alpha):
    bsz, seq, d = x.shape
    tm = min(MIXER_TOKENS, seq)
    nt = seq // tm
    mw = p["conv_w"].shape[1]
    gw = p["sg_g"].shape[1]
    dh = mw // M_HEADS
    n_exp = p["wr_hi"].shape[0]
    const2 = lambda b, i: (0, 0)
    const3 = lambda b, i: (0, 0, 0)
    full = lambda a: pl.BlockSpec(a.shape, const2 if a.ndim == 2 else const3)
    names = ["w_in", "conv_w", "conv_b", "w_q", "w_k", "w_v", "w_if", "b_if", "mh_g", "skip", "sg_g",
             "sg_b", "w_sp", "b_sp_t", "w_out", "ln1_g", "ln1_b", "wr_hi", "wr_lo"]
    weights = [p[n] for n in names]
    return pl.pallas_call(
        functools.partial(_mixer_kernel, alpha=alpha),
        out_shape=(
            jax.ShapeDtypeStruct((bsz, seq, d), F32),
            jax.ShapeDtypeStruct((bsz * seq * SLAB, LANES), F32),
            jax.ShapeDtypeStruct((n_exp, bsz * seq), F32),
        ),
        grid=(bsz, nt),
        in_specs=[
            pl.BlockSpec((1, tm, d), lambda b, i: (b, i, 0)),
            pl.BlockSpec((1, SUBLANES, d), lambda b, i: (b, 0, 0)),
        ] + [full(w) for w in weights],
        out_specs=(
            pl.BlockSpec((1, tm, d), lambda b, i: (b, i, 0)),
            pl.BlockSpec((tm * SLAB, LANES), lambda b, i: (b * nt + i, 0)),
            pl.BlockSpec((n_exp, tm), lambda b, i: (0, b * nt + i)),
        ),
        scratch_shapes=[
            pltpu.VMEM((M_HEADS, dh, 2 * dh), F32),
            pltpu.VMEM((SUBLANES, LANES), F32),
            pltpu.VMEM((tm + SUBLANES, mw), F32),
            pltpu.VMEM((tm, mw + gw), BF16),
        ],
        compiler_params=pltpu.CompilerParams(
            dimension_semantics=("arbitrary", "arbitrary"), vmem_limit_bytes=VMEM_LIMIT),
        name="mixer",
    )(x, mod, *weights)


def _route_kernel(logt_ref, bias_ref, upper_ref, idx_ref, w_ref, rank_ref, wt_ref, cnt_ref, carry_ref):
    n_exp, tr = logt_ref.shape
    epg = n_exp // N_GROUPS

    @pl.when(pl.program_id(0) == 0)
    def _():
        carry_ref[...] = jnp.zeros_like(carry_ref)

    scores = jax.nn.sigmoid(logt_ref[...])
    sel = scores + bias_ref[...]

    sel3 = sel.reshape(N_GROUPS, epg, tr)
    io3 = lax.broadcasted_iota(I32, (N_GROUPS, epg, tr), 1)
    m1 = jnp.max(sel3, axis=1, keepdims=True)
    first = jnp.min(jnp.where(sel3 == m1, io3, epg), axis=1, keepdims=True)
    m2 = jnp.max(jnp.where(io3 == first, NEG_INF, sel3), axis=1, keepdims=True)
    gs = (m1 + m2).reshape(N_GROUPS, tr)

    gio = lax.broadcasted_iota(I32, (N_GROUPS, tr), 0)
    gmask = jnp.zeros((N_GROUPS, tr), F32)
    for _ in range(TOPK_GROUPS):
        m = jnp.max(gs, axis=0, keepdims=True)
        gi = jnp.min(jnp.where(gs == m, gio, N_GROUPS), axis=0, keepdims=True)
        hit = gio == gi
        gmask = jnp.where(hit, 1.0, gmask)
        gs = jnp.where(hit, NEG_INF, gs)
    emask = jnp.broadcast_to(gmask.reshape(N_GROUPS, 1, tr), (N_GROUPS, epg, tr)).reshape(n_exp, tr)
    selm = jnp.where(emask > 0.0, sel, NEG_INF)

    eio = lax.broadcasted_iota(I32, (n_exp, tr), 0)
    chosen = jnp.zeros((n_exp, tr), F32)
    idx_rows, w_rows = [], []
    for _ in range(TOP_K):
        m = jnp.max(selm, axis=0, keepdims=True)
        ei = jnp.min(jnp.where(selm == m, eio, n_exp), axis=0, keepdims=True)
        hit = eio == ei
        w_rows.append(jnp.sum(jnp.where(hit, scores, 0.0), axis=0, keepdims=True))
        idx_rows.append(ei)
        selm = jnp.where(hit, NEG_INF, selm)
        chosen = jnp.where(hit, 1.0, chosen)

    chosen_b = chosen.astype(BF16)
    carry = carry_ref[...]
    ranks = carry[:, 0:1] + _dot(chosen_b, upper_ref[...])
    carry_new = carry + _dot(chosen_b, jnp.ones((tr, LANES), BF16))
    carry_ref[...] = carry_new
    cnt_ref[...] = carry_new
    rank_rows = [jnp.sum(jnp.where(eio == ei, ranks, 0.0), axis=0, keepdims=True) for ei in idx_rows]

    wsum = w_rows[0]
    for wk in w_rows[1:]:
        wsum = wsum + wk
    w8 = jnp.concatenate([wk / wsum * ROUTE_SCALE for wk in w_rows], axis=0)
    idx_ref[...] = jnp.concatenate(idx_rows, axis=0)
    rank_ref[...] = jnp.concatenate(rank_rows, axis=0).astype(I32)
    w_ref[...] = w8
    wpad = jnp.concatenate([w8, jnp.zeros((LANES - TOP_K, tr), F32)], axis=0)
    wt_ref[...] = wpad.T


def _route(logt, e_bias_col, upper):
    n_exp, t = logt.shape
    tr = upper.shape[0]
    return pl.pallas_call(
        _route_kernel,
        out_shape=(
            jax.ShapeDtypeStruct((TOP_K, t), I32),
            jax.ShapeDtypeStruct((TOP_K, t), F32),
            jax.ShapeDtypeStruct((TOP_K, t), I32),
            jax.ShapeDtypeStruct((t, LANES), F32),
            jax.ShapeDtypeStruct((n_exp, LANES), F32),
        ),
        grid=(t // tr,),
        in_specs=[
            pl.BlockSpec((n_exp, tr), lambda i: (0, i)),
            pl.BlockSpec((n_exp, 1), lambda i: (0, 0)),
            pl.BlockSpec((tr, tr), lambda i: (0, 0)),
        ],
        out_specs=(
            pl.BlockSpec((TOP_K, tr), lambda i: (0, i)),
            pl.BlockSpec((TOP_K, tr), lambda i: (0, i)),
            pl.BlockSpec((TOP_K, tr), lambda i: (0, i)),
            pl.BlockSpec((tr, LANES), lambda i: (i, 0)),
            pl.BlockSpec((n_exp, LANES), lambda i: (0, 0)),
        ),
        scratch_shapes=[pltpu.VMEM((n_exp, LANES), F32)],
        compiler_params=pltpu.CompilerParams(
            dimension_semantics=("arbitrary",), vmem_limit_bytes=VMEM_LIMIT),
        name="route",
    )(logt, e_bias_col, upper)


def _dest_kernel(idx_ref, rank_ref, start_ref, dest_ref):
    n_exp = start_ref.shape[0]
    tt = idx_ref.shape[1]
    eio = lax.broadcasted_iota(I32, (n_exp, tt), 0)
    start = start_ref[...]
    rows = []
    for k in range(TOP_K):
        hit = eio == idx_ref[k:k + 1, :]
        rows.append(jnp.sum(jnp.where(hit, start, 0), axis=0, keepdims=True) + rank_ref[k:k + 1, :])
    dest_ref[...] = jnp.concatenate(rows, axis=0) * SLAB


def _dest(idx, rank, start_col):
    t = idx.shape[1]
    tt = min(DEST_TOKENS, t)
    n_exp = start_col.shape[0]
    return pl.pallas_call(
        _dest_kernel,
        out_shape=jax.ShapeDtypeStruct((TOP_K, t), I32),
        grid=(t // tt,),
        in_specs=[
            pl.BlockSpec((TOP_K, tt), lambda i: (0, i)),
            pl.BlockSpec((TOP_K, tt), lambda i: (0, i)),
            pl.BlockSpec((n_exp, 1), lambda i: (0, 0)),
        ],
        out_specs=pl.BlockSpec((TOP_K, tt), lambda i: (0, i)),
        compiler_params=pltpu.CompilerParams(vmem_limit_bytes=VMEM_LIMIT),
        name="dest",
    )(idx, rank, start_col)


def _slab(ref, first_row):
    return ref.at[pl.ds(pl.multiple_of(first_row, SLAB), SLAB), :]


def _slabs_copy(hbm, sem, n_slabs):
    n = n_slabs * SLAB
    return pltpu.make_async_copy(hbm.at[pl.ds(0, n), :], hbm.at[pl.ds(0, n), :], sem)


def _dispatch_kernel(pad_first, pad_len, n_used, dest_ref, h_ref, xs_hbm, zeros_ref, sem):
    td = dest_ref.shape[1]
    n_exp = pad_first.shape[0]
    block_slab_rows = EXPERT_ROWS * SLAB
    n_blocks_max = xs_hbm.shape[0] // block_slab_rows
    step = pl.program_id(0)

    def zero_fill(act):
        def zeros_to(first_slab_row, n_rows):
            act(pltpu.make_async_copy(
                zeros_ref.at[pl.ds(0, n_rows * SLAB), :],
                xs_hbm.at[pl.ds(pl.multiple_of(first_slab_row, SLAB), n_rows * SLAB), :], sem.at[1]))

        def expert_padding(e, carry):
            row = pad_first[e]
            for bit in reversed(range(EXPERT_ROWS.bit_length() - 1)):
                take = (pad_len[e] >> bit) & 1
                pl.when(take == 1)(functools.partial(zeros_to, row * SLAB, 1 << bit))
                row = row + (take << bit)
            return carry

        def unused_block(b, carry):
            zeros_to(b * block_slab_rows, EXPERT_ROWS)
            return carry

        lax.fori_loop(0, n_exp, expert_padding, 0)
        lax.fori_loop(n_used[0], n_blocks_max, unused_block, 0)

    @pl.when(step == 0)
    def _():
        zeros_ref[...] = jnp.zeros_like(zeros_ref)
        zero_fill(lambda cp: cp.start())

    def body(t, carry):
        src = _slab(h_ref, t * SLAB)
        for k in range(TOP_K):
            pltpu.make_async_copy(src, _slab(xs_hbm, dest_ref[k, t]), sem.at[0]).start(priority=k % 2)
        return carry

    lax.fori_loop(0, td, body, 0)
    _slabs_copy(xs_hbm, sem.at[0], td * TOP_K).wait()

    @pl.when(step == pl.num_programs(0) - 1)
    def _():
        zero_fill(lambda cp: cp.wait())


def _dispatch(pad_first, pad_len, n_used, dest, h_slabs, n_rows):
    t = h_slabs.shape[0] // SLAB
    td = min(DISPATCH_TOKENS, t)
    grid_spec = pltpu.PrefetchScalarGridSpec(
        num_scalar_prefetch=3,
        grid=(t // td,),
        in_specs=[
            pl.BlockSpec((TOP_K, td), lambda i, pf, pn, nu: (0, i), memory_space=pltpu.SMEM),
            pl.BlockSpec((td * SLAB, LANES), lambda i, pf, pn, nu: (i, 0)),
        ],
        out_specs=pl.BlockSpec(memory_space=pl.ANY),
        scratch_shapes=[
            pltpu.VMEM((EXPERT_ROWS * SLAB, LANES), h_slabs.dtype),
            pltpu.SemaphoreType.DMA((2,)),
        ],
    )
    return pl.pallas_call(
        _dispatch_kernel,
        out_shape=jax.ShapeDtypeStruct((n_rows * SLAB, LANES), h_slabs.dtype),
        grid_spec=grid_spec,
        compiler_params=pltpu.CompilerParams(
            dimension_semantics=("arbitrary",), vmem_limit_bytes=VMEM_LIMIT),
        name="dispatch",
    )(pad_first, pad_len, n_used, dest, h_slabs)


def _experts_kernel(block_expert, block_new_expert, n_blocks,
                    xs_ref, wg_ref, wu_ref, wd_ref, ys_ref, wg_b, wu_b, wd_b):
    i = pl.program_id(0)

    @pl.when(i < n_blocks[0])
    def _():
        @pl.when(block_new_expert[i] == 1)
        def _():
            wg_b[...] = wg_ref[0].astype(BF16)
            wu_b[...] = wu_ref[0].astype(BF16)
            wd_b[...] = wd_ref[0].astype(BF16)

        xb = _join_slabs(_load_slabs(xs_ref, EXPERT_ROWS)).astype(BF16)
        g = _dot(xb, wg_b[...])
        u = _dot(xb, wu_b[...])
        _store_slabs(ys_ref, _split_slabs(_dot((_silu(g) * u).astype(BF16), wd_b[...])))


def _experts(meta, xs, w_gate, w_up, w_down):
    block_expert, block_new_expert, n_blocks = meta
    _, d, de = w_gate.shape
    n_max = block_expert.shape[0]
    slab_rows = EXPERT_ROWS * SLAB
    blk = lambda i, be, ne, nb: (jnp.minimum(i, nb[0] - 1), 0)
    wsel = lambda i, be, ne, nb: (be[i], 0, 0)
    grid_spec = pltpu.PrefetchScalarGridSpec(
        num_scalar_prefetch=3,
        grid=(n_max,),
        in_specs=[
            pl.BlockSpec((slab_rows, LANES), blk),
            pl.BlockSpec((1, d, de), wsel),
            pl.BlockSpec((1, d, de), wsel),
            pl.BlockSpec((1, de, d), wsel),
        ],
        out_specs=pl.BlockSpec((slab_rows, LANES), blk),
        scratch_shapes=[
            pltpu.VMEM((d, de), BF16),
            pltpu.VMEM((d, de), BF16),
            pltpu.VMEM((de, d), BF16),
        ],
    )
    return pl.pallas_call(
        _experts_kernel,
        out_shape=jax.ShapeDtypeStruct(xs.shape, xs.dtype),
        grid_spec=grid_spec,
        input_output_aliases={3: 0},
        compiler_params=pltpu.CompilerParams(
            dimension_semantics=("arbitrary",), vmem_limit_bytes=VMEM_LIMIT),
        name="experts",
    )(block_expert, block_new_expert, n_blocks, xs, w_gate, w_up, w_down)


def _expert_blocks(counts, n_assign):
    n_exp = counts.shape[0]
    n_max = (n_assign + n_exp * (EXPERT_ROWS - 1)) // EXPERT_ROWS + 1
    padded = (counts + EXPERT_ROWS - 1) // EXPERT_ROWS * EXPERT_ROWS
    pad_end = jnp.cumsum(padded)
    pad_start = pad_end - padded
    n_used = pad_end[-1] // EXPERT_ROWS
    first_row = jnp.minimum(jnp.arange(n_max, dtype=I32), n_used - 1) * EXPERT_ROWS
    e_of = jnp.sum((first_row[:, None] >= pad_end[None, :]).astype(I32), axis=1)
    e_of = jnp.minimum(e_of, n_exp - 1).astype(I32)
    prev_e = jnp.concatenate([jnp.full((1,), -1, I32), e_of[:-1]])
    new_expert = (e_of != prev_e).astype(I32)
    meta = (e_of, new_expert, n_used.reshape(1).astype(I32))
    return meta, pad_start.astype(I32), (pad_start + counts).astype(I32), (padded - counts).astype(I32), n_max


def _combine_kernel(dest_cur, dest_nxt, x1_ref, mod_ref, wt_ref, wsg_ref, wsu_ref, wsd_ref,
                    ln2g_ref, ln2b_ref, ys_hbm, out_ref, buf, sem, *, alpha):
    tc = x1_ref.shape[0]
    i = pl.program_id(0)
    n = pl.num_programs(0)
    slot = i % 2

    def issue(dest_ref, s):
        def body(t, carry):
            for k in range(TOP_K):
                pltpu.make_async_copy(_slab(ys_hbm, dest_ref[k, t]), _slab(buf.at[s, k], t * SLAB),
                                      sem.at[s]).start(priority=k % 2)
            return carry
        lax.fori_loop(0, tc, body, 0)

    @pl.when(i == 0)
    def _():
        issue(dest_cur, 0)

    @pl.when(i + 1 < n)
    def _():
        issue(dest_nxt, 1 - slot)

    mod = mod_ref[0]
    sh_f, sc_f, gt_f = mod[3:4], mod[4:5], mod[5:6]
    x1 = x1_ref[...]
    hb = (_ln(x1) * (1.0 + sc_f) + sh_f).astype(BF16)
    mid = _silu(_dot(hb, wsg_ref[...])) * _dot(hb, wsu_ref[...])
    ffn = _dot(mid.astype(BF16), wsd_ref[...])

    for k in range(TOP_K):
        pltpu.make_async_copy(ys_hbm.at[pl.ds(0, tc * SLAB), :], buf.at[slot, k], sem.at[slot]).wait()
    wt = wt_ref[...]
    routed = None
    for k in range(TOP_K):
        yk = wt[:, k:k + 1] * _join_slabs(_load_slabs(buf.at[slot, k], tc))
        routed = yk if routed is None else routed + yk
    ffn = routed + ffn
    out_ref[...] = _ln(alpha * x1 + (1.0 + gt_f) * ffn) * ln2g_ref[...] + ln2b_ref[...]


def _combine(dest, x1_flat, mod, wt, ws_gate, ws_up, ws_down, ln2_g, ln2_b, ys, seq, alpha):
    t, d = x1_flat.shape
    tc = min(COMBINE_TOKENS, seq)
    n = t // tc
    per_seq = seq // tc
    const = lambda i: (0, 0)
    return pl.pallas_call(
        functools.partial(_combine_kernel, alpha=alpha),
        out_shape=jax.ShapeDtypeStruct((t, d), F32),
        grid=(n,),
        in_specs=[
            pl.BlockSpec((TOP_K, tc), lambda i: (0, i), memory_space=pltpu.SMEM),
            pl.BlockSpec((TOP_K, tc), lambda i: (0, jnp.minimum(i + 1, n - 1)), memory_space=pltpu.SMEM),
            pl.BlockSpec((tc, d), lambda i: (i, 0)),
            pl.BlockSpec((1, SUBLANES, d), lambda i: (i // per_seq, 0, 0)),
            pl.BlockSpec((tc, LANES), lambda i: (i, 0)),
            pl.BlockSpec(ws_gate.shape, const),
            pl.BlockSpec(ws_up.shape, const),
            pl.BlockSpec(ws_down.shape, const),
            pl.BlockSpec(ln2_g.shape, const),
            pl.BlockSpec(ln2_b.shape, const),
            pl.BlockSpec(memory_space=pl.ANY),
        ],
        out_specs=pl.BlockSpec((tc, d), lambda i: (i, 0)),
        scratch_shapes=[
            pltpu.VMEM((2, TOP_K, tc * SLAB, LANES), F32),
            pltpu.SemaphoreType.DMA((2,)),
        ],
        compiler_params=pltpu.CompilerParams(
            dimension_semantics=("arbitrary",), vmem_limit_bytes=VMEM_LIMIT),
        name="combine",
    )(dest, dest, x1_flat, mod, wt, ws_gate, ws_up, ws_down, ln2_g, ln2_b, ys)


def _layer(x, c_pad, lw, alpha):
    bsz, seq, d = x.shape
    t = bsz * seq
    n_exp = lw["w_router"].shape[1]
    h_count = lw["w_if"].shape[1] // 2

    mod = _ada(c_pad, lw["w_ada"], lw["b_ada"].reshape(1, -1))[:bsz].reshape(bsz, 6, d)
    mod = jnp.pad(mod, ((0, 0), (0, SUBLANES - 6), (0, 0)))

    row2 = lambda a: a.reshape(1, -1)
    w_if = lw["w_if"]
    w_if_pad = jnp.zeros((w_if.shape[0], 2 * LANES), F32)
    w_if_pad = w_if_pad.at[:, :h_count].set(w_if[:, :h_count]).at[:, LANES:LANES + h_count].set(w_if[:, h_count:])
    b_if_pad = jnp.zeros((1, 2 * LANES), F32)
    b_if_pad = b_if_pad.at[0, :h_count].set(lw["b_if"][:h_count]).at[0, LANES:LANES + h_count].set(lw["b_if"][h_count:])
    w_rt = lw["w_router"].astype(F32).T
    wr_hi = w_rt.astype(BF16)
    wr_lo = (w_rt - wr_hi.astype(F32)).astype(BF16)
    p = {
        "w_in": lw["w_in"].astype(BF16), "conv_w": lw["conv_w"], "conv_b": row2(lw["conv_b"]),
        "w_q": lw["w_q"].astype(BF16), "w_k": lw["w_k"].astype(BF16), "w_v": lw["w_v"].astype(BF16),
        "w_if": w_if_pad.astype(BF16), "b_if": b_if_pad,
        "mh_g": row2(lw["mh_g"]), "skip": row2(lw["skip"]), "sg_g": row2(lw["sg_g"]), "sg_b": row2(lw["sg_b"]),
        "w_sp": lw["w_sp"], "b_sp_t": lw["b_sp"].T, "w_out": lw["w_out"].astype(BF16),
        "ln1_g": row2(lw["ln1_g"]), "ln1_b": row2(lw["ln1_b"]), "wr_hi": wr_hi, "wr_lo": wr_lo,
    }
    x1, h2, logt = _mixer(x, mod, p, alpha)

    tr = min(ROUTE_TOKENS, t)
    upper = (jnp.arange(tr)[:, None] < jnp.arange(tr)[None, :]).astype(BF16)
    idx, _, rank, wt, cnt = _route(logt, lw["e_bias"].astype(F32).reshape(n_exp, 1), upper)
    counts = cnt[:, 0].astype(I32)
    meta, pad_start, pad_first, pad_len, n_blocks_max = _expert_blocks(counts, t * TOP_K)
    dest = _dest(idx, rank, pad_start.reshape(n_exp, 1))

    xs = _dispatch(pad_first, pad_len, meta[2], dest, h2, n_blocks_max * EXPERT_ROWS)
    ys = _experts(meta, xs, lw["w_gate"], lw["w_up"], lw["w_down"])
    out = _combine(dest, x1.reshape(t, d), mod, wt, lw["ws_gate"].astype(BF16), lw["ws_up"].astype(BF16),
                   lw["ws_down"].astype(BF16), row2(lw["ln2_g"]), row2(lw["ln2_b"]), ys, seq, alpha)
    return out.reshape(bsz, seq, d)


def kernel(x, c, w_ada, b_ada, w_in, conv_w, conv_b, w_q, w_k, w_v, w_if, b_if, mh_g, skip, sg_g, sg_b, w_sp, b_sp, w_out, ln1_g, ln1_b, w_router, e_bias, w_gate, w_up, w_down, ws_gate, ws_up, ws_down, ln2_g, ln2_b):
    stacked = dict(w_ada=w_ada, b_ada=b_ada, w_in=w_in, conv_w=conv_w, conv_b=conv_b, w_q=w_q, w_k=w_k,
                   w_v=w_v, w_if=w_if, b_if=b_if, mh_g=mh_g, skip=skip, sg_g=sg_g, sg_b=sg_b, w_sp=w_sp,
                   b_sp=b_sp, w_out=w_out, ln1_g=ln1_g, ln1_b=ln1_b, w_router=w_router, e_bias=e_bias,
                   w_gate=w_gate, w_up=w_up, w_down=w_down, ws_gate=ws_gate, ws_up=ws_up, ws_down=ws_down,
                   ln2_g=ln2_g, ln2_b=ln2_b)
    depth = w_ada.shape[0]
    alpha = float((2 * depth) ** 0.25)
    bsz = x.shape[0]
    c_pad = jnp.pad(c, ((0, -bsz % SUBLANES), (0, 0)))
    for l in range(depth):
        x = _layer(x, c_pad, {k: v[l] for k, v in stacked.items()}, alpha)
    return x
```

```python
import functools
import math

import jax
import jax.numpy as jnp
from jax import lax
from jax.experimental import pallas as pl
from jax.experimental.pallas import tpu as pltpu

F32 = jnp.float32
BF16 = jnp.bfloat16
I32 = jnp.int32

LN_EPS = 1e-5
M_HEADS = 4
G_GROUPS = 4
CHUNK = 128
N_GROUPS = 8
TOPK_GROUPS = 4
TOP_K = 8
ROUTE_SCALE = 2.5
LANES = 128
SUBLANES = 8

MIXER_TOKENS = 512
ROUTE_TOKENS = 1024
DEST_TOKENS = 2048
DISPATCH_TOKENS = 512
EXPERT_ROWS = 512
COMBINE_TOKENS = 256
VMEM_LIMIT = 56 * 1024 * 1024

NEG_INF = float("-inf")


def _ln(x):
    mu = jnp.mean(x, axis=-1, keepdims=True)
    xc = x - mu
    var = jnp.mean(xc * xc, axis=-1, keepdims=True)
    return xc * lax.rsqrt(var + LN_EPS)


def _dot(a, b):
    return jnp.dot(a, b, preferred_element_type=F32)


def _dot_nt(a, b):
    return lax.dot_general(a, b, (((1,), (1,)), ((), ())), preferred_element_type=F32)


def _dot_exact(a, b):
    return jnp.dot(a, b, preferred_element_type=F32, precision=lax.Precision.HIGHEST)


def _silu(x):
    return x * jax.nn.sigmoid(x)


def _gelu(x):
    return 0.5 * x * (1.0 + lax.erf(x * math.sqrt(0.5)))


def _log_sigmoid(x):
    return jnp.minimum(x, 0.0) - jnp.log1p(jnp.exp(-jnp.abs(x)))


SLAB = SUBLANES


def _to_slabs(x, scratch):
    rows, d = x.shape
    assert d == SLAB * LANES
    for s in range(SLAB):
        scratch[pl.ds(s, rows, stride=SLAB), :] = x[:, s * LANES:(s + 1) * LANES]
    return scratch[0:rows * SLAB, :].reshape(rows, SLAB, LANES).astype(BF16)


def _slabs_to_rows(scratch, rows):
    return jnp.concatenate([scratch[pl.ds(s, rows, stride=SLAB), :] for s in range(SLAB)], axis=1)


def _from_slabs(x3, scratch):
    rows = x3.shape[0]
    scratch[0:rows * SLAB, :] = x3.astype(F32).reshape(rows * SLAB, LANES)
    return _slabs_to_rows(scratch, rows)


def _ada_kernel(c_ref, w_ref, b_ref, o_ref):
    o_ref[...] = _dot_exact(_silu(c_ref[...]), w_ref[...]) + b_ref[...]


def _ada(c_pad, w_ada, b_ada):
    rows, d = c_pad.shape
    n = w_ada.shape[1]
    return pl.pallas_call(
        _ada_kernel,
        out_shape=jax.ShapeDtypeStruct((rows, n), F32),
        grid=(n // d,),
        in_specs=[
            pl.BlockSpec((rows, d), lambda j: (0, 0)),
            pl.BlockSpec((d, d), lambda j: (0, j)),
            pl.BlockSpec((1, d), lambda j: (0, j)),
        ],
        out_specs=pl.BlockSpec((rows, d), lambda j: (0, j)),
        compiler_params=pltpu.CompilerParams(vmem_limit_bytes=VMEM_LIMIT),
        name="ada",
    )(c_pad, w_ada, b_ada)


def _mixer_kernel(x_ref, mod_ref, w_in_ref, conv_w_ref, conv_b_ref, wq_ref, wk_ref, wv_ref,
                  wif_ref, bif_ref, mhg_ref, skip_ref, sgg_ref, sgb_ref, wsp_ref, bspt_ref,
                  w_out_ref, ln1g_ref, ln1b_ref, wr_hi_ref, wr_lo_ref,
                  x1_ref, h2_ref, logt_ref,
                  state_ref, m_ref, xm_ref, cat_ref, slab_ref, *, alpha):
    tm = x_ref.shape[1]
    mw = conv_w_ref.shape[1]
    dh = mw // M_HEADS
    gw = sgg_ref.shape[1]
    gd = gw // G_GROUPS
    conv_k = conv_w_ref.shape[0]
    nch = tm // CHUNK
    L = CHUNK

    @pl.when(pl.program_id(1) == 0)
    def _():
        state_ref[...] = jnp.zeros_like(state_ref)
        m_ref[...] = jnp.zeros_like(m_ref)
        xm_ref[0:SUBLANES, :] = jnp.zeros((SUBLANES, mw), F32)

    x = x_ref[0]
    mod = mod_ref[0]
    sh_a, sc_a, gt_a = mod[0:1], mod[1:2], mod[2:3]
    sh_f, sc_f = mod[3:4], mod[4:5]

    h = _ln(x) * (1.0 + sc_a) + sh_a
    proj = _dot(h.astype(BF16), w_in_ref[...])
    xm = proj[:, :mw]
    z = proj[:, mw:2 * mw]
    u = proj[:, 2 * mw:2 * mw + gw]
    v = proj[:, 2 * mw + gw:]

    xm_ref[SUBLANES:SUBLANES + tm, :] = xm
    conv = jnp.broadcast_to(conv_b_ref[...], (tm, mw))
    for j in range(conv_k):
        off = SUBLANES - (conv_k - 1) + j
        conv = conv + conv_w_ref[j:j + 1, :] * xm_ref[off:off + tm, :]
    xm_ref[0:SUBLANES, :] = xm_ref[tm:tm + SUBLANES, :]
    xc = _silu(conv)

    scale = dh ** -0.5
    qs, ks, vs = [], [], []
    for hd in range(M_HEADS):
        sl = slice(hd * dh, (hd + 1) * dh)
        xch = xc[:, sl].astype(BF16)
        qs.append(_dot(xch, wq_ref[hd]))
        ks.append(_dot(xch, wk_ref[hd]) * scale)
        vs.append(_dot(xm[:, sl].astype(BF16), wv_ref[hd]))
    qkv = jnp.concatenate(qs + ks + vs, axis=1).astype(BF16)
    gate = _dot(qkv, wif_ref[...]) + bif_ref[...]
    gi = gate[:, :LANES]
    lf = _log_sigmoid(gate[:, LANES:])

    row = lax.broadcasted_iota(I32, (L, L), 0)
    col = lax.broadcasted_iota(I32, (L, L), 1)
    causal = col <= row
    tri = jnp.where(causal, 1.0, 0.0).astype(F32)
    ones_l = jnp.ones((L, dh), F32)

    for c in range(nch):
        rs = slice(c * L, (c + 1) * L)
        bmat = _dot_exact(tri, lf[rs])
        rmat = gi[rs] - bmat
        rmat_t = rmat.T
        for hd in range(M_HEADS):
            sl = slice(hd * dh, (hd + 1) * dh)
            bcol = bmat[:, hd:hd + 1]
            rcol = rmat[:, hd:hd + 1]
            rrow = rmat_t[hd:hd + 1, :]
            g = bmat[L - 1:L, hd:hd + 1]
            m_prev = m_ref[hd:hd + 1, 0:1]
            q = qs[hd][rs]
            k = ks[hd][rs]
            vv = vs[hd][rs]
            qb = q.astype(BF16)

            log_d = jnp.where(causal, bcol + rrow, NEG_INF)
            log_inter = bcol + m_prev
            m_t = jnp.maximum(log_inter, jnp.max(log_d, axis=1, keepdims=True))
            p = jnp.exp(log_d - m_t) * _dot_nt(qb, k.astype(BF16))
            w_inter = jnp.exp(log_inter - m_t)
            vext = jnp.concatenate([vv, ones_l], axis=1).astype(BF16)
            sx = state_ref[hd]
            out_ext = w_inter * _dot(qb, sx.astype(BF16)) + _dot(p.astype(BF16), vext)
            num = out_ext[:, :dh]
            nq = out_ext[:, dh:]
            hh = num / jnp.maximum(jnp.abs(nq), jnp.exp(-m_t))

            a = g + rcol
            m_new = jnp.maximum(g + m_prev, jnp.max(a, axis=0, keepdims=True))
            decay = jnp.exp(g + m_prev - m_new)
            kw = k * jnp.exp(a - m_new)
            state_ref[hd] = decay * sx + _dot(kw.T.astype(BF16), vext)
            m_ref[hd:hd + 1, :] = jnp.broadcast_to(m_new, (1, LANES))

            hc = _ln(hh) * mhg_ref[:, sl]
            hm = jax.nn.sigmoid(z[rs, sl]) * (hc + skip_ref[:, sl] * xc[rs, sl])
            cat_ref[rs, sl] = hm.astype(BF16)

    ug = _gelu(u)
    vg = _gelu(v)
    for gg in range(G_GROUPS):
        sl = slice(gg * gd, (gg + 1) * gd)
        vn = (_ln(vg[:, sl]) * sgg_ref[:, sl] + sgb_ref[:, sl]).astype(BF16)
        wsg = jnp.where(causal, wsp_ref[gg], 0.0).astype(BF16)
        bcol = bspt_ref[:, gg:gg + 1]
        for c in range(nch):
            rs = slice(c * L, (c + 1) * L)
            s = _dot(wsg, vn[rs]) + bcol
            cat_ref[rs, mw + gg * gd:mw + (gg + 1) * gd] = (ug[rs, sl] * s).astype(BF16)

    mix = _dot(cat_ref[...], w_out_ref[...])
    x1 = _ln(alpha * x + (1.0 + gt_a) * mix) * ln1g_ref[...] + ln1b_ref[...]
    x1_ref[0] = x1
    h2 = _ln(x1) * (1.0 + sc_f) + sh_f
    h2_ref[...] = _to_slabs(h2, slab_ref)
    h_hi = h2.astype(BF16)
    h_lo = (h2 - h_hi.astype(F32)).astype(BF16)
    wr_hi = wr_hi_ref[...]
    logt_ref[...] = _dot_nt(wr_hi, h_hi) + _dot_nt(wr_hi, h_lo) + _dot_nt(wr_lo_ref[...], h_hi)


def _mixer(x, mod, p, alpha):
    bsz, seq, d = x.shape
    tm = min(MIXER_TOKENS, seq)
    nt = seq // tm
    mw = p["conv_w"].shape[1]
    gw = p["sg_g"].shape[1]
    dh = mw // M_HEADS
    n_exp = p["wr_hi"].shape[0]
    const2 = lambda b, i: (0, 0)
    const3 = lambda b, i: (0, 0, 0)
    full = lambda a: pl.BlockSpec(a.shape, const2 if a.ndim == 2 else const3)
    names = ["w_in", "conv_w", "conv_b", "w_q", "w_k", "w_v", "w_if", "b_if", "mh_g", "skip", "sg_g",
             "sg_b", "w_sp", "b_sp_t", "w_out", "ln1_g", "ln1_b", "wr_hi", "wr_lo"]
    weights = [p[n] for n in names]
    return pl.pallas_call(
        functools.partial(_mixer_kernel, alpha=alpha),
        out_shape=(
            jax.ShapeDtypeStruct((bsz, seq, d), F32),
            jax.ShapeDtypeStruct((bsz * seq, SLAB, LANES), BF16),
            jax.ShapeDtypeStruct((n_exp, bsz * seq), F32),
        ),
        grid=(bsz, nt),
        in_specs=[
            pl.BlockSpec((1, tm, d), lambda b, i: (b, i, 0)),
            pl.BlockSpec((1, SUBLANES, d), lambda b, i: (b, 0, 0)),
        ] + [full(w) for w in weights],
        out_specs=(
            pl.BlockSpec((1, tm, d), lambda b, i: (b, i, 0)),
            pl.BlockSpec((tm, SLAB, LANES), lambda b, i: (b * nt + i, 0, 0)),
            pl.BlockSpec((n_exp, tm), lambda b, i: (0, b * nt + i)),
        ),
        scratch_shapes=[
            pltpu.VMEM((M_HEADS, dh, 2 * dh), F32),
            pltpu.VMEM((SUBLANES, LANES), F32),
            pltpu.VMEM((tm + SUBLANES, mw), F32),
            pltpu.VMEM((tm, mw + gw), BF16),
            pltpu.VMEM((tm * SLAB, LANES), F32),
        ],
        compiler_params=pltpu.CompilerParams(
            dimension_semantics=("arbitrary", "arbitrary"), vmem_limit_bytes=VMEM_LIMIT),
        name="mixer",
    )(x, mod, *weights)


def _route_kernel(logt_ref, bias_ref, upper_ref, idx_ref, w_ref, rank_ref, wt_ref, cnt_ref, carry_ref):
    n_exp, tr = logt_ref.shape
    epg = n_exp // N_GROUPS

    @pl.when(pl.program_id(0) == 0)
    def _():
        carry_ref[...] = jnp.zeros_like(carry_ref)

    scores = jax.nn.sigmoid(logt_ref[...])
    sel = scores + bias_ref[...]

    sel3 = sel.reshape(N_GROUPS, epg, tr)
    io3 = lax.broadcasted_iota(I32, (N_GROUPS, epg, tr), 1)
    m1 = jnp.max(sel3, axis=1, keepdims=True)
    first = jnp.min(jnp.where(sel3 == m1, io3, epg), axis=1, keepdims=True)
    m2 = jnp.max(jnp.where(io3 == first, NEG_INF, sel3), axis=1, keepdims=True)
    gs = (m1 + m2).reshape(N_GROUPS, tr)

    gio = lax.broadcasted_iota(I32, (N_GROUPS, tr), 0)
    gmask = jnp.zeros((N_GROUPS, tr), F32)
    for _ in range(TOPK_GROUPS):
        m = jnp.max(gs, axis=0, keepdims=True)
        gi = jnp.min(jnp.where(gs == m, gio, N_GROUPS), axis=0, keepdims=True)
        hit = gio == gi
        gmask = jnp.where(hit, 1.0, gmask)
        gs = jnp.where(hit, NEG_INF, gs)
    emask = jnp.broadcast_to(gmask.reshape(N_GROUPS, 1, tr), (N_GROUPS, epg, tr)).reshape(n_exp, tr)
    selm = jnp.where(emask > 0.0, sel, NEG_INF)

    eio = lax.broadcasted_iota(I32, (n_exp, tr), 0)
    chosen = jnp.zeros((n_exp, tr), F32)
    idx_rows, w_rows = [], []
    for _ in range(TOP_K):
        m = jnp.max(selm, axis=0, keepdims=True)
        ei = jnp.min(jnp.where(selm == m, eio, n_exp), axis=0, keepdims=True)
        hit = eio == ei
        w_rows.append(jnp.sum(jnp.where(hit, scores, 0.0), axis=0, keepdims=True))
        idx_rows.append(ei)
        selm = jnp.where(hit, NEG_INF, selm)
        chosen = jnp.where(hit, 1.0, chosen)

    chosen_b = chosen.astype(BF16)
    carry = carry_ref[...]
    ranks = carry[:, 0:1] + _dot(chosen_b, upper_ref[...])
    carry_new = carry + _dot(chosen_b, jnp.ones((tr, LANES), BF16))
    carry_ref[...] = carry_new
    cnt_ref[...] = carry_new
    rank_rows = [jnp.sum(jnp.where(eio == ei, ranks, 0.0), axis=0, keepdims=True) for ei in idx_rows]

    wsum = w_rows[0]
    for wk in w_rows[1:]:
        wsum = wsum + wk
    w8 = jnp.concatenate([wk / wsum * ROUTE_SCALE for wk in w_rows], axis=0)
    idx_ref[...] = jnp.concatenate(idx_rows, axis=0)
    rank_ref[...] = jnp.concatenate(rank_rows, axis=0).astype(I32)
    w_ref[...] = w8
    wpad = jnp.concatenate([w8, jnp.zeros((LANES - TOP_K, tr), F32)], axis=0)
    wt_ref[...] = wpad.T


def _route(logt, e_bias_col, upper):
    n_exp, t = logt.shape
    tr = upper.shape[0]
    return pl.pallas_call(
        _route_kernel,
        out_shape=(
            jax.ShapeDtypeStruct((TOP_K, t), I32),
            jax.ShapeDtypeStruct((TOP_K, t), F32),
            jax.ShapeDtypeStruct((TOP_K, t), I32),
            jax.ShapeDtypeStruct((t, LANES), F32),
            jax.ShapeDtypeStruct((n_exp, LANES), F32),
        ),
        grid=(t // tr,),
        in_specs=[
            pl.BlockSpec((n_exp, tr), lambda i: (0, i)),
            pl.BlockSpec((n_exp, 1), lambda i: (0, 0)),
            pl.BlockSpec((tr, tr), lambda i: (0, 0)),
        ],
        out_specs=(
            pl.BlockSpec((TOP_K, tr), lambda i: (0, i)),
            pl.BlockSpec((TOP_K, tr), lambda i: (0, i)),
            pl.BlockSpec((TOP_K, tr), lambda i: (0, i)),
            pl.BlockSpec((tr, LANES), lambda i: (i, 0)),
            pl.BlockSpec((n_exp, LANES), lambda i: (0, 0)),
        ),
        scratch_shapes=[pltpu.VMEM((n_exp, LANES), F32)],
        compiler_params=pltpu.CompilerParams(
            dimension_semantics=("arbitrary",), vmem_limit_bytes=VMEM_LIMIT),
        name="route",
    )(logt, e_bias_col, upper)


def _dest_kernel(idx_ref, rank_ref, start_ref, dest_ref):
    n_exp = start_ref.shape[0]
    tt = idx_ref.shape[1]
    eio = lax.broadcasted_iota(I32, (n_exp, tt), 0)
    start = start_ref[...]
    rows = []
    for k in range(TOP_K):
        hit = eio == idx_ref[k:k + 1, :]
        rows.append(jnp.sum(jnp.where(hit, start, 0), axis=0, keepdims=True) + rank_ref[k:k + 1, :])
    dest_ref[...] = jnp.concatenate(rows, axis=0)


def _dest(idx, rank, start_col):
    t = idx.shape[1]
    tt = min(DEST_TOKENS, t)
    n_exp = start_col.shape[0]
    return pl.pallas_call(
        _dest_kernel,
        out_shape=jax.ShapeDtypeStruct((TOP_K, t), I32),
        grid=(t // tt,),
        in_specs=[
            pl.BlockSpec((TOP_K, tt), lambda i: (0, i)),
            pl.BlockSpec((TOP_K, tt), lambda i: (0, i)),
            pl.BlockSpec((n_exp, 1), lambda i: (0, 0)),
        ],
        out_specs=pl.BlockSpec((TOP_K, tt), lambda i: (0, i)),
        compiler_params=pltpu.CompilerParams(vmem_limit_bytes=VMEM_LIMIT),
        name="dest",
    )(idx, rank, start_col)


def _slabs_copy(hbm, sem, n_slabs):
    return pltpu.make_async_copy(hbm.at[pl.ds(0, n_slabs)], hbm.at[pl.ds(0, n_slabs)], sem)


def _dispatch_kernel(pad_first, pad_len, n_used, dest_ref, h_ref, xs_hbm, zeros_ref, sem):
    td = dest_ref.shape[1]
    n_exp = pad_first.shape[0]
    n_blocks_max = xs_hbm.shape[0] // EXPERT_ROWS
    step = pl.program_id(0)

    def zero_fill(act):
        def zeros_to(first_row, n_rows):
            act(pltpu.make_async_copy(zeros_ref.at[pl.ds(0, n_rows)], xs_hbm.at[pl.ds(first_row, n_rows)],
                                      sem.at[1]))

        def expert_padding(e, carry):
            row = pad_first[e]
            for bit in reversed(range(EXPERT_ROWS.bit_length() - 1)):
                take = (pad_len[e] >> bit) & 1
                pl.when(take == 1)(functools.partial(zeros_to, row, 1 << bit))
                row = row + (take << bit)
            return carry

        def unused_block(b, carry):
            zeros_to(b * EXPERT_ROWS, EXPERT_ROWS)
            return carry

        lax.fori_loop(0, n_exp, expert_padding, 0)
        lax.fori_loop(n_used[0], n_blocks_max, unused_block, 0)

    @pl.when(step == 0)
    def _():
        zeros_ref[...] = jnp.zeros_like(zeros_ref)
        zero_fill(lambda cp: cp.start())

    def body(t, carry):
        for k in range(TOP_K):
            pltpu.make_async_copy(h_ref.at[t], xs_hbm.at[dest_ref[k, t]], sem.at[0]).start(priority=k % 2)
        return carry

    lax.fori_loop(0, td, body, 0)
    _slabs_copy(xs_hbm, sem.at[0], td * TOP_K).wait()

    @pl.when(step == pl.num_programs(0) - 1)
    def _():
        zero_fill(lambda cp: cp.wait())


def _dispatch(pad_first, pad_len, n_used, dest, h_slabs, n_rows):
    t = h_slabs.shape[0]
    td = min(DISPATCH_TOKENS, t)
    grid_spec = pltpu.PrefetchScalarGridSpec(
        num_scalar_prefetch=3,
        grid=(t // td,),
        in_specs=[
            pl.BlockSpec((TOP_K, td), lambda i, pf, pn, nu: (0, i), memory_space=pltpu.SMEM),
            pl.BlockSpec((td, SLAB, LANES), lambda i, pf, pn, nu: (i, 0, 0)),
        ],
        out_specs=pl.BlockSpec(memory_space=pl.ANY),
        scratch_shapes=[
            pltpu.VMEM((EXPERT_ROWS, SLAB, LANES), h_slabs.dtype),
            pltpu.SemaphoreType.DMA((2,)),
        ],
    )
    return pl.pallas_call(
        _dispatch_kernel,
        out_shape=jax.ShapeDtypeStruct((n_rows, SLAB, LANES), h_slabs.dtype),
        grid_spec=grid_spec,
        compiler_params=pltpu.CompilerParams(
            dimension_semantics=("arbitrary",), vmem_limit_bytes=VMEM_LIMIT),
        name="dispatch",
    )(pad_first, pad_len, n_used, dest, h_slabs)


def _experts_kernel(block_expert, block_new_expert, n_blocks,
                    xs_ref, wg_ref, wu_ref, wd_ref, ys_ref, wg_b, wu_b, wd_b, slab_ref):
    i = pl.program_id(0)

    @pl.when(i < n_blocks[0])
    def _():
        @pl.when(block_new_expert[i] == 1)
        def _():
            wg_b[...] = wg_ref[0].astype(BF16)
            wu_b[...] = wu_ref[0].astype(BF16)
            wd_b[...] = wd_ref[0].astype(BF16)

        xb = _from_slabs(xs_ref[...], slab_ref).astype(BF16)
        g = _dot(xb, wg_b[...])
        u = _dot(xb, wu_b[...])
        y = _dot((_silu(g) * u).astype(BF16), wd_b[...])
        ys_ref[...] = _to_slabs(y, slab_ref)


def _experts(meta, xs, w_gate, w_up, w_down):
    block_expert, block_new_expert, n_blocks = meta
    _, d, de = w_gate.shape
    n_max = block_expert.shape[0]
    blk = lambda i, be, ne, nb: (jnp.minimum(i, nb[0] - 1), 0, 0)
    wsel = lambda i, be, ne, nb: (be[i], 0, 0)
    grid_spec = pltpu.PrefetchScalarGridSpec(
        num_scalar_prefetch=3,
        grid=(n_max,),
        in_specs=[
            pl.BlockSpec((EXPERT_ROWS, SLAB, LANES), blk),
            pl.BlockSpec((1, d, de), wsel),
            pl.BlockSpec((1, d, de), wsel),
            pl.BlockSpec((1, de, d), wsel),
        ],
        out_specs=pl.BlockSpec((EXPERT_ROWS, SLAB, LANES), blk),
        scratch_shapes=[
            pltpu.VMEM((d, de), BF16),
            pltpu.VMEM((d, de), BF16),
            pltpu.VMEM((de, d), BF16),
            pltpu.VMEM((EXPERT_ROWS * SLAB, LANES), F32),
        ],
    )
    return pl.pallas_call(
        _experts_kernel,
        out_shape=jax.ShapeDtypeStruct(xs.shape, xs.dtype),
        grid_spec=grid_spec,
        input_output_aliases={3: 0},
        compiler_params=pltpu.CompilerParams(
            dimension_semantics=("arbitrary",), vmem_limit_bytes=VMEM_LIMIT),
        name="experts",
    )(block_expert, block_new_expert, n_blocks, xs, w_gate, w_up, w_down)


def _expert_blocks(counts, n_assign):
    n_exp = counts.shape[0]
    n_max = (n_assign + n_exp * (EXPERT_ROWS - 1)) // EXPERT_ROWS + 1
    padded = (counts + EXPERT_ROWS - 1) // EXPERT_ROWS * EXPERT_ROWS
    pad_end = jnp.cumsum(padded)
    pad_start = pad_end - padded
    n_used = pad_end[-1] // EXPERT_ROWS
    first_row = jnp.minimum(jnp.arange(n_max, dtype=I32), n_used - 1) * EXPERT_ROWS
    e_of = jnp.sum((first_row[:, None] >= pad_end[None, :]).astype(I32), axis=1)
    e_of = jnp.minimum(e_of, n_exp - 1).astype(I32)
    prev_e = jnp.concatenate([jnp.full((1,), -1, I32), e_of[:-1]])
    new_expert = (e_of != prev_e).astype(I32)
    meta = (e_of, new_expert, n_used.reshape(1).astype(I32))
    return meta, pad_start.astype(I32), (pad_start + counts).astype(I32), (padded - counts).astype(I32), n_max


def _combine_kernel(dest_cur, dest_nxt, x1_ref, mod_ref, wt_ref, wsg_ref, wsu_ref, wsd_ref,
                    ln2g_ref, ln2b_ref, ys_hbm, out_ref, buf, wb_ref, acc_ref, sem, *, alpha):
    tc = x1_ref.shape[0]
    i = pl.program_id(0)
    n = pl.num_programs(0)
    slot = i % 2

    def request(dest_ref, s, t):
        for k in range(TOP_K):
            pltpu.make_async_copy(ys_hbm.at[dest_ref[k, t]], buf.at[s, k, t], sem.at[s]).start(priority=k % 2)

    @pl.when(i == 0)
    def _():
        def body(t, carry):
            request(dest_cur, 0, t)
            return carry
        lax.fori_loop(0, tc, body, 0)

    for k in range(TOP_K):
        pltpu.make_async_copy(ys_hbm.at[pl.ds(0, tc)], buf.at[slot, k], sem.at[slot]).wait()

    wt = wt_ref[...]
    for k in range(TOP_K):
        wb_ref[k] = jnp.broadcast_to(wt[:, k:k + 1], (tc, LANES))

    def token(t, carry, prefetch):
        if prefetch:
            request(dest_nxt, 1 - slot, t)
        terms = [jnp.broadcast_to(wb_ref[k, pl.ds(t, 1), :], (SLAB, LANES)) * buf[slot, k, t].astype(F32)
                 for k in range(TOP_K)]
        while len(terms) > 1:
            terms = [a + b for a, b in zip(terms[0::2], terms[1::2])]
        acc_ref[pl.ds(pl.multiple_of(t * SLAB, SLAB), SLAB), :] = terms[0]
        return carry

    @pl.when(i + 1 < n)
    def _():
        lax.fori_loop(0, tc, functools.partial(token, prefetch=True), 0, unroll=2)

    @pl.when(i + 1 == n)
    def _():
        lax.fori_loop(0, tc, functools.partial(token, prefetch=False), 0, unroll=2)

    mod = mod_ref[0]
    sh_f, sc_f, gt_f = mod[3:4], mod[4:5], mod[5:6]
    x1 = x1_ref[...]
    hb = (_ln(x1) * (1.0 + sc_f) + sh_f).astype(BF16)
    mid = _silu(_dot(hb, wsg_ref[...])) * _dot(hb, wsu_ref[...])
    ffn = _dot(mid.astype(BF16), wsd_ref[...])
    ffn = _slabs_to_rows(acc_ref, tc) + ffn
    out_ref[...] = _ln(alpha * x1 + (1.0 + gt_f) * ffn) * ln2g_ref[...] + ln2b_ref[...]


def _combine(dest, x1_flat, mod, wt, ws_gate, ws_up, ws_down, ln2_g, ln2_b, ys, seq, alpha):
    t, d = x1_flat.shape
    tc = min(COMBINE_TOKENS, seq)
    n = t // tc
    per_seq = seq // tc
    const = lambda i: (0, 0)
    return pl.pallas_call(
        functools.partial(_combine_kernel, alpha=alpha),
        out_shape=jax.ShapeDtypeStruct((t, d), F32),
        grid=(n,),
        in_specs=[
            pl.BlockSpec((TOP_K, tc), lambda i: (0, i), memory_space=pltpu.SMEM),
            pl.BlockSpec((TOP_K, tc), lambda i: (0, jnp.minimum(i + 1, n - 1)), memory_space=pltpu.SMEM),
            pl.BlockSpec((tc, d), lambda i: (i, 0)),
            pl.BlockSpec((1, SUBLANES, d), lambda i: (i // per_seq, 0, 0)),
            pl.BlockSpec((tc, LANES), lambda i: (i, 0)),
            pl.BlockSpec(ws_gate.shape, const),
            pl.BlockSpec(ws_up.shape, const),
            pl.BlockSpec(ws_down.shape, const),
            pl.BlockSpec(ln2_g.shape, const),
            pl.BlockSpec(ln2_b.shape, const),
            pl.BlockSpec(memory_space=pl.ANY),
        ],
        out_specs=pl.BlockSpec((tc, d), lambda i: (i, 0)),
        scratch_shapes=[
            pltpu.VMEM((2, TOP_K, tc, SLAB, LANES), BF16),
            pltpu.VMEM((TOP_K, tc, LANES), F32),
            pltpu.VMEM((tc * SLAB, LANES), F32),
            pltpu.SemaphoreType.DMA((2,)),
        ],
        compiler_params=pltpu.CompilerParams(
            dimension_semantics=("arbitrary",), vmem_limit_bytes=VMEM_LIMIT),
        name="combine",
    )(dest, dest, x1_flat, mod, wt, ws_gate, ws_up, ws_down, ln2_g, ln2_b, ys)


def _layer(x, c_pad, lw, alpha):
    bsz, seq, d = x.shape
    t = bsz * seq
    n_exp = lw["w_router"].shape[1]
    h_count = lw["w_if"].shape[1] // 2

    mod = _ada(c_pad, lw["w_ada"], lw["b_ada"].reshape(1, -1))[:bsz].reshape(bsz, 6, d)
    mod = jnp.pad(mod, ((0, 0), (0, SUBLANES - 6), (0, 0)))

    row2 = lambda a: a.reshape(1, -1)
    w_if = lw["w_if"]
    w_if_pad = jnp.zeros((w_if.shape[0], 2 * LANES), F32)
    w_if_pad = w_if_pad.at[:, :h_count].set(w_if[:, :h_count]).at[:, LANES:LANES + h_count].set(w_if[:, h_count:])
    b_if_pad = jnp.zeros((1, 2 * LANES), F32)
    b_if_pad = b_if_pad.at[0, :h_count].set(lw["b_if"][:h_count]).at[0, LANES:LANES + h_count].set(lw["b_if"][h_count:])
    w_rt = lw["w_router"].astype(F32).T
    wr_hi = w_rt.astype(BF16)
    wr_lo = (w_rt - wr_hi.astype(F32)).astype(BF16)
    p = {
        "w_in": lw["w_in"].astype(BF16), "conv_w": lw["conv_w"], "conv_b": row2(lw["conv_b"]),
        "w_q": lw["w_q"].astype(BF16), "w_k": lw["w_k"].astype(BF16), "w_v": lw["w_v"].astype(BF16),
        "w_if": w_if_pad.astype(BF16), "b_if": b_if_pad,
        "mh_g": row2(lw["mh_g"]), "skip": row2(lw["skip"]), "sg_g": row2(lw["sg_g"]), "sg_b": row2(lw["sg_b"]),
        "w_sp": lw["w_sp"], "b_sp_t": lw["b_sp"].T, "w_out": lw["w_out"].astype(BF16),
        "ln1_g": row2(lw["ln1_g"]), "ln1_b": row2(lw["ln1_b"]), "wr_hi": wr_hi, "wr_lo": wr_lo,
    }
    x1, h2, logt = _mixer(x, mod, p, alpha)

    tr = min(ROUTE_TOKENS, t)
    upper = (jnp.arange(tr)[:, None] < jnp.arange(tr)[None, :]).astype(BF16)
    idx, _, rank, wt, cnt = _route(logt, lw["e_bias"].astype(F32).reshape(n_exp, 1), upper)
    counts = cnt[:, 0].astype(I32)
    meta, pad_start, pad_first, pad_len, n_blocks_max = _expert_blocks(counts, t * TOP_K)
    dest = _dest(idx, rank, pad_start.reshape(n_exp, 1))

    xs = _dispatch(pad_first, pad_len, meta[2], dest, h2, n_blocks_max * EXPERT_ROWS)
    ys = _experts(meta, xs, lw["w_gate"], lw["w_up"], lw["w_down"])
    out = _combine(dest, x1.reshape(t, d), mod, wt, lw["ws_gate"].astype(BF16), lw["ws_up"].astype(BF16),
                   lw["ws_down"].astype(BF16), row2(lw["ln2_g"]), row2(lw["ln2_b"]), ys, seq, alpha)
    return out.reshape(bsz, seq, d)


def kernel(x, c, w_ada, b_ada, w_in, conv_w, conv_b, w_q, w_k, w_v, w_if, b_if, mh_g, skip, sg_g, sg_b, w_sp, b_sp, w_out, ln1_g, ln1_b, w_router, e_bias, w_gate, w_up, w_down, ws_gate, ws_up, ws_down, ln2_g, ln2_b):
    stacked = dict(w_ada=w_ada, b_ada=b_ada, w_in=w_in, conv_w=conv_w, conv_b=conv_b, w_q=w_q, w_k=w_k,
                   w_v=w_v, w_if=w_if, b_if=b_if, mh_g=mh_g, skip=skip, sg_g=sg_g, sg_b=sg_b, w_sp=w_sp,
                   b_sp=b_sp, w_out=w_out, ln1_g=ln1_g, ln1_b=ln1_b, w_router=w_router, e_bias=e_bias,
                   w_gate=w_gate, w_up=w_up, w_down=w_down, ws_gate=ws_gate, ws_up=ws_up, ws_down=ws_down,
                   ln2_g=ln2_g, ln2_b=ln2_b)
    depth = w_ada.shape[0]
    alpha = float((2 * depth) ** 0.25)
    bsz = x.shape[0]
    c_pad = jnp.pad(c, ((0, -bsz % SUBLANES), (0, 0)))
    for l in range(depth):
        x = _layer(x, c_pad, {k: v[l] for k, v in stacked.items()}, alpha)
    return x
```

```python
import functools
import math

import jax
import jax.numpy as jnp
from jax import lax
from jax.experimental import pallas as pl
from jax.experimental.pallas import tpu as pltpu

F32 = jnp.float32
BF16 = jnp.bfloat16
I32 = jnp.int32

LN_EPS = 1e-5
M_HEADS = 4
G_GROUPS = 4
CHUNK = 128
N_GROUPS = 8
TOPK_GROUPS = 4
TOP_K = 8
ROUTE_SCALE = 2.5
LANES = 128
SUBLANES = 8

MIXER_TOKENS = 512
ROUTE_TOKENS = 1024
DEST_TOKENS = 2048
DISPATCH_TOKENS = 1024
EXPERT_ROWS = 512
COMBINE_TOKENS = 256
VMEM_LIMIT = 56 * 1024 * 1024

NEG_INF = float("-inf")


def _ln(x):
    mu = jnp.mean(x, axis=-1, keepdims=True)
    xc = x - mu
    var = jnp.mean(xc * xc, axis=-1, keepdims=True)
    return xc * lax.rsqrt(var + LN_EPS)


def _dot(a, b):
    return jnp.dot(a, b, preferred_element_type=F32)


def _dot_nt(a, b):
    return lax.dot_general(a, b, (((1,), (1,)), ((), ())), preferred_element_type=F32)


def _dot_exact(a, b):
    return jnp.dot(a, b, preferred_element_type=F32, precision=lax.Precision.HIGHEST)


def _silu(x):
    return x * jax.nn.sigmoid(x)


def _gelu(x):
    return 0.5 * x * (1.0 + lax.erf(x * math.sqrt(0.5)))


def _log_sigmoid(x):
    return jnp.minimum(x, 0.0) - jnp.log1p(jnp.exp(-jnp.abs(x)))


SLAB = SUBLANES


def _to_slabs(x, scratch):
    rows, d = x.shape
    assert d == SLAB * LANES
    for s in range(SLAB):
        scratch[pl.ds(s, rows, stride=SLAB), :] = x[:, s * LANES:(s + 1) * LANES]
    return scratch[0:rows * SLAB, :].reshape(rows, SLAB, LANES).astype(BF16)


def _slabs_to_rows(scratch, rows):
    return jnp.concatenate([scratch[pl.ds(s, rows, stride=SLAB), :] for s in range(SLAB)], axis=1)


def _from_slabs(x3, scratch):
    rows = x3.shape[0]
    scratch[0:rows * SLAB, :] = x3.astype(F32).reshape(rows * SLAB, LANES)
    return _slabs_to_rows(scratch, rows)


def _ada_kernel(c_ref, w_ref, b_ref, o_ref):
    o_ref[...] = _dot_exact(_silu(c_ref[...]), w_ref[...]) + b_ref[...]


def _ada(c_pad, w_ada, b_ada):
    rows, d = c_pad.shape
    n = w_ada.shape[1]
    return pl.pallas_call(
        _ada_kernel,
        out_shape=jax.ShapeDtypeStruct((rows, n), F32),
        grid=(n // d,),
        in_specs=[
            pl.BlockSpec((rows, d), lambda j: (0, 0)),
            pl.BlockSpec((d, d), lambda j: (0, j)),
            pl.BlockSpec((1, d), lambda j: (0, j)),
        ],
        out_specs=pl.BlockSpec((rows, d), lambda j: (0, j)),
        compiler_params=pltpu.CompilerParams(vmem_limit_bytes=VMEM_LIMIT),
        name="ada",
    )(c_pad, w_ada, b_ada)


def _mixer_kernel(x_ref, mod_ref, w_in_ref, conv_w_ref, conv_b_ref, wq_ref, wk_ref, wv_ref,
                  wif_ref, bif_ref, mhg_ref, skip_ref, sgg_ref, sgb_ref, wsp_ref, bspt_ref,
                  w_out_ref, ln1g_ref, ln1b_ref, wr_hi_ref, wr_lo_ref,
                  x1_ref, h2_ref, logt_ref,
                  state_ref, m_ref, xm_ref, cat_ref, slab_ref, *, alpha):
    tm = x_ref.shape[1]
    mw = conv_w_ref.shape[1]
    dh = mw // M_HEADS
    gw = sgg_ref.shape[1]
    gd = gw // G_GROUPS
    conv_k = conv_w_ref.shape[0]
    nch = tm // CHUNK
    L = CHUNK

    @pl.when(pl.program_id(1) == 0)
    def _():
        state_ref[...] = jnp.zeros_like(state_ref)
        m_ref[...] = jnp.zeros_like(m_ref)
        xm_ref[0:SUBLANES, :] = jnp.zeros((SUBLANES, mw), F32)

    x = x_ref[0]
    mod = mod_ref[0]
    sh_a, sc_a, gt_a = mod[0:1], mod[1:2], mod[2:3]
    sh_f, sc_f = mod[3:4], mod[4:5]

    h = _ln(x) * (1.0 + sc_a) + sh_a
    proj = _dot(h.astype(BF16), w_in_ref[...])
    xm = proj[:, :mw]
    z = proj[:, mw:2 * mw]
    u = proj[:, 2 * mw:2 * mw + gw]
    v = proj[:, 2 * mw + gw:]

    xm_ref[SUBLANES:SUBLANES + tm, :] = xm
    conv = jnp.broadcast_to(conv_b_ref[...], (tm, mw))
    for j in range(conv_k):
        off = SUBLANES - (conv_k - 1) + j
        conv = conv + conv_w_ref[j:j + 1, :] * xm_ref[off:off + tm, :]
    xm_ref[0:SUBLANES, :] = xm_ref[tm:tm + SUBLANES, :]
    xc = _silu(conv)

    scale = dh ** -0.5
    qs, ks, vs = [], [], []
    for hd in range(M_HEADS):
        sl = slice(hd * dh, (hd + 1) * dh)
        xch = xc[:, sl].astype(BF16)
        qs.append(_dot(xch, wq_ref[hd]))
        ks.append(_dot(xch, wk_ref[hd]) * scale)
        vs.append(_dot(xm[:, sl].astype(BF16), wv_ref[hd]))
    qkv = jnp.concatenate(qs + ks + vs, axis=1).astype(BF16)
    gate = _dot(qkv, wif_ref[...]) + bif_ref[...]
    gi = gate[:, :LANES]
    lf = _log_sigmoid(gate[:, LANES:])

    row = lax.broadcasted_iota(I32, (L, L), 0)
    col = lax.broadcasted_iota(I32, (L, L), 1)
    causal = col <= row
    tri = jnp.where(causal, 1.0, 0.0).astype(F32)
    ones_l = jnp.ones((L, dh), F32)
    wsg = [jnp.where(causal, wsp_ref[gg], 0.0).astype(BF16) for gg in range(G_GROUPS)]

    for c in range(nch):
        rs = slice(c * L, (c + 1) * L)
        bmat = _dot_exact(tri, lf[rs])
        rmat = gi[rs] - bmat
        rmat_t = rmat.T
        for hd in range(M_HEADS):
            sl = slice(hd * dh, (hd + 1) * dh)
            bcol = bmat[:, hd:hd + 1]
            rcol = rmat[:, hd:hd + 1]
            rrow = rmat_t[hd:hd + 1, :]
            g = bmat[L - 1:L, hd:hd + 1]
            m_prev = m_ref[hd:hd + 1, 0:1]
            q = qs[hd][rs]
            k = ks[hd][rs]
            vv = vs[hd][rs]
            qb = q.astype(BF16)

            log_d = jnp.where(causal, bcol + rrow, NEG_INF)
            log_inter = bcol + m_prev
            m_t = jnp.maximum(log_inter, jnp.max(log_d, axis=1, keepdims=True))
            p = jnp.exp(log_d - m_t) * _dot_nt(qb, k.astype(BF16))
            w_inter = jnp.exp(log_inter - m_t)
            vext = jnp.concatenate([vv, ones_l], axis=1).astype(BF16)
            sx = state_ref[hd]
            out_ext = w_inter * _dot(qb, sx.astype(BF16)) + _dot(p.astype(BF16), vext)
            num = out_ext[:, :dh]
            nq = out_ext[:, dh:]
            hh = num / jnp.maximum(jnp.abs(nq), jnp.exp(-m_t))

            a = g + rcol
            m_new = jnp.maximum(g + m_prev, jnp.max(a, axis=0, keepdims=True))
            decay = jnp.exp(g + m_prev - m_new)
            kw = k * jnp.exp(a - m_new)
            state_ref[hd] = decay * sx + _dot(kw.T.astype(BF16), vext)
            m_ref[hd:hd + 1, :] = jnp.broadcast_to(m_new, (1, LANES))

            hc = _ln(hh) * mhg_ref[:, sl]
            hm = jax.nn.sigmoid(z[rs, sl]) * (hc + skip_ref[:, sl] * xc[rs, sl])
            cat_ref[rs, sl] = hm.astype(BF16)

        ug = _gelu(u[rs])
        vg = _gelu(v[rs])
        for gg in range(G_GROUPS):
            sl = slice(gg * gd, (gg + 1) * gd)
            vn = (_ln(vg[:, sl]) * sgg_ref[:, sl] + sgb_ref[:, sl]).astype(BF16)
            s = _dot(wsg[gg], vn) + bspt_ref[:, gg:gg + 1]
            cat_ref[rs, mw + gg * gd:mw + (gg + 1) * gd] = (ug[:, sl] * s).astype(BF16)

    mix = _dot(cat_ref[...], w_out_ref[...])
    x1 = _ln(alpha * x + (1.0 + gt_a) * mix) * ln1g_ref[...] + ln1b_ref[...]
    x1_ref[0] = x1
    h2 = _ln(x1) * (1.0 + sc_f) + sh_f
    h2_ref[...] = _to_slabs(h2, slab_ref)
    h_hi = h2.astype(BF16)
    h_lo = (h2 - h_hi.astype(F32)).astype(BF16)
    wr_hi = wr_hi_ref[...]
    logt_ref[...] = _dot_nt(wr_hi, h_hi) + _dot_nt(wr_hi, h_lo) + _dot_nt(wr_lo_ref[...], h_hi)


def _mixer(x, mod, p, alpha):
    bsz, seq, d = x.shape
    tm = min(MIXER_TOKENS, seq)
    nt = seq // tm
    mw = p["conv_w"].shape[1]
    gw = p["sg_g"].shape[1]
    dh = mw // M_HEADS
    n_exp = p["wr_hi"].shape[0]
    const2 = lambda b, i: (0, 0)
    const3 = lambda b, i: (0, 0, 0)
    full = lambda a: pl.BlockSpec(a.shape, const2 if a.ndim == 2 else const3)
    names = ["w_in", "conv_w", "conv_b", "w_q", "w_k", "w_v", "w_if", "b_if", "mh_g", "skip", "sg_g",
             "sg_b", "w_sp", "b_sp_t", "w_out", "ln1_g", "ln1_b", "wr_hi", "wr_lo"]
    weights = [p[n] for n in names]
    return pl.pallas_call(
        functools.partial(_mixer_kernel, alpha=alpha),
        out_shape=(
            jax.ShapeDtypeStruct((bsz, seq, d), F32),
            jax.ShapeDtypeStruct((bsz * seq, SLAB, LANES), BF16),
            jax.ShapeDtypeStruct((n_exp, bsz * seq), F32),
        ),
        grid=(bsz, nt),
        in_specs=[
            pl.BlockSpec((1, tm, d), lambda b, i: (b, i, 0)),
            pl.BlockSpec((1, SUBLANES, d), lambda b, i: (b, 0, 0)),
        ] + [full(w) for w in weights],
        out_specs=(
            pl.BlockSpec((1, tm, d), lambda b, i: (b, i, 0)),
            pl.BlockSpec((tm, SLAB, LANES), lambda b, i: (b * nt + i, 0, 0)),
            pl.BlockSpec((n_exp, tm), lambda b, i: (0, b * nt + i)),
        ),
        scratch_shapes=[
            pltpu.VMEM((M_HEADS, dh, 2 * dh), F32),
            pltpu.VMEM((SUBLANES, LANES), F32),
            pltpu.VMEM((tm + SUBLANES, mw), F32),
            pltpu.VMEM((tm, mw + gw), BF16),
            pltpu.VMEM((tm * SLAB, LANES), F32),
        ],
        compiler_params=pltpu.CompilerParams(
            dimension_semantics=("arbitrary", "arbitrary"), vmem_limit_bytes=VMEM_LIMIT),
        name="mixer",
    )(x, mod, *weights)


def _route_kernel(logt_ref, bias_ref, upper_ref, idx_ref, w_ref, rank_ref, wt_ref, cnt_ref, carry_ref):
    n_exp, tr = logt_ref.shape
    epg = n_exp // N_GROUPS

    @pl.when(pl.program_id(0) == 0)
    def _():
        carry_ref[...] = jnp.zeros_like(carry_ref)

    scores = jax.nn.sigmoid(logt_ref[...])
    sel = scores + bias_ref[...]

    sel3 = sel.reshape(N_GROUPS, epg, tr)
    io3 = lax.broadcasted_iota(I32, (N_GROUPS, epg, tr), 1)
    m1 = jnp.max(sel3, axis=1, keepdims=True)
    first = jnp.min(jnp.where(sel3 == m1, io3, epg), axis=1, keepdims=True)
    m2 = jnp.max(jnp.where(io3 == first, NEG_INF, sel3), axis=1, keepdims=True)
    gs = (m1 + m2).reshape(N_GROUPS, tr)

    gio = lax.broadcasted_iota(I32, (N_GROUPS, tr), 0)
    gmask = jnp.zeros((N_GROUPS, tr), F32)
    for _ in range(TOPK_GROUPS):
        m = jnp.max(gs, axis=0, keepdims=True)
        gi = jnp.min(jnp.where(gs == m, gio, N_GROUPS), axis=0, keepdims=True)
        hit = gio == gi
        gmask = jnp.where(hit, 1.0, gmask)
        gs = jnp.where(hit, NEG_INF, gs)
    emask = jnp.broadcast_to(gmask.reshape(N_GROUPS, 1, tr), (N_GROUPS, epg, tr)).reshape(n_exp, tr)
    selm = jnp.where(emask > 0.0, sel, NEG_INF)

    eio = lax.broadcasted_iota(I32, (n_exp, tr), 0)
    chosen = jnp.zeros((n_exp, tr), F32)
    idx_rows, w_rows = [], []
    for _ in range(TOP_K):
        m = jnp.max(selm, axis=0, keepdims=True)
        ei = jnp.min(jnp.where(selm == m, eio, n_exp), axis=0, keepdims=True)
        hit = eio == ei
        w_rows.append(jnp.sum(jnp.where(hit, scores, 0.0), axis=0, keepdims=True))
        idx_rows.append(ei)
        selm = jnp.where(hit, NEG_INF, selm)
        chosen = jnp.where(hit, 1.0, chosen)

    chosen_b = chosen.astype(BF16)
    carry = carry_ref[...]
    ranks = carry[:, 0:1] + _dot(chosen_b, upper_ref[...])
    carry_new = carry + _dot(chosen_b, jnp.ones((tr, LANES), BF16))
    carry_ref[...] = carry_new
    cnt_ref[...] = carry_new
    rank_rows = [jnp.sum(jnp.where(eio == ei, ranks, 0.0), axis=0, keepdims=True) for ei in idx_rows]

    wsum = w_rows[0]
    for wk in w_rows[1:]:
        wsum = wsum + wk
    w8 = jnp.concatenate([wk / wsum * ROUTE_SCALE for wk in w_rows], axis=0)
    idx_ref[...] = jnp.concatenate(idx_rows, axis=0)
    rank_ref[...] = jnp.concatenate(rank_rows, axis=0).astype(I32)
    w_ref[...] = w8
    wpad = jnp.concatenate([w8, jnp.zeros((LANES - TOP_K, tr), F32)], axis=0)
    wt_ref[...] = wpad.T


def _route(logt, e_bias_col, upper):
    n_exp, t = logt.shape
    tr = upper.shape[0]
    return pl.pallas_call(
        _route_kernel,
        out_shape=(
            jax.ShapeDtypeStruct((TOP_K, t), I32),
            jax.ShapeDtypeStruct((TOP_K, t), F32),
            jax.ShapeDtypeStruct((TOP_K, t), I32),
            jax.ShapeDtypeStruct((t, LANES), F32),
            jax.ShapeDtypeStruct((n_exp, LANES), F32),
        ),
        grid=(t // tr,),
        in_specs=[
            pl.BlockSpec((n_exp, tr), lambda i: (0, i)),
            pl.BlockSpec((n_exp, 1), lambda i: (0, 0)),
            pl.BlockSpec((tr, tr), lambda i: (0, 0)),
        ],
        out_specs=(
            pl.BlockSpec((TOP_K, tr), lambda i: (0, i)),
            pl.BlockSpec((TOP_K, tr), lambda i: (0, i)),
            pl.BlockSpec((TOP_K, tr), lambda i: (0, i)),
            pl.BlockSpec((tr, LANES), lambda i: (i, 0)),
            pl.BlockSpec((n_exp, LANES), lambda i: (0, 0)),
        ),
        scratch_shapes=[pltpu.VMEM((n_exp, LANES), F32)],
        compiler_params=pltpu.CompilerParams(
            dimension_semantics=("arbitrary",), vmem_limit_bytes=VMEM_LIMIT),
        name="route",
    )(logt, e_bias_col, upper)


def _dest_kernel(idx_ref, rank_ref, start_ref, dest_ref):
    n_exp = start_ref.shape[0]
    tt = idx_ref.shape[1]
    eio = lax.broadcasted_iota(I32, (n_exp, tt), 0)
    start = start_ref[...]
    rows = []
    for k in range(TOP_K):
        hit = eio == idx_ref[k:k + 1, :]
        rows.append(jnp.sum(jnp.where(hit, start, 0), axis=0, keepdims=True) + rank_ref[k:k + 1, :])
    dest_ref[...] = jnp.concatenate(rows, axis=0)


def _dest(idx, rank, start_col):
    t = idx.shape[1]
    tt = min(DEST_TOKENS, t)
    n_exp = start_col.shape[0]
    return pl.pallas_call(
        _dest_kernel,
        out_shape=jax.ShapeDtypeStruct((TOP_K, t), I32),
        grid=(t // tt,),
        in_specs=[
            pl.BlockSpec((TOP_K, tt), lambda i: (0, i)),
            pl.BlockSpec((TOP_K, tt), lambda i: (0, i)),
            pl.BlockSpec((n_exp, 1), lambda i: (0, 0)),
        ],
        out_specs=pl.BlockSpec((TOP_K, tt), lambda i: (0, i)),
        compiler_params=pltpu.CompilerParams(vmem_limit_bytes=VMEM_LIMIT),
        name="dest",
    )(idx, rank, start_col)


def _slabs_copy(hbm, sem, n_slabs):
    return pltpu.make_async_copy(hbm.at[pl.ds(0, n_slabs)], hbm.at[pl.ds(0, n_slabs)], sem)


def _dispatch_kernel(pad_first, pad_len, n_used, dest_ref, h_ref, xs_hbm, zeros_ref, sem):
    td = dest_ref.shape[1]
    n_exp = pad_first.shape[0]
    n_blocks_max = xs_hbm.shape[0] // EXPERT_ROWS
    step = pl.program_id(0)

    def zero_fill(act):
        def zeros_to(first_row, n_rows):
            act(pltpu.make_async_copy(zeros_ref.at[pl.ds(0, n_rows)], xs_hbm.at[pl.ds(first_row, n_rows)],
                                      sem.at[1]))

        def expert_padding(e, carry):
            row = pad_first[e]
            for bit in reversed(range(EXPERT_ROWS.bit_length() - 1)):
                take = (pad_len[e] >> bit) & 1
                pl.when(take == 1)(functools.partial(zeros_to, row, 1 << bit))
                row = row + (take << bit)
            return carry

        def unused_block(b, carry):
            zeros_to(b * EXPERT_ROWS, EXPERT_ROWS)
            return carry

        lax.fori_loop(0, n_exp, expert_padding, 0)
        lax.fori_loop(n_used[0], n_blocks_max, unused_block, 0)

    @pl.when(step == 0)
    def _():
        zeros_ref[...] = jnp.zeros_like(zeros_ref)
        zero_fill(lambda cp: cp.start())

    def body(t, carry):
        for k in range(TOP_K):
            pltpu.make_async_copy(h_ref.at[t], xs_hbm.at[dest_ref[k, t]], sem.at[0]).start(priority=k % 2)
        return carry

    lax.fori_loop(0, td, body, 0, unroll=2)
    _slabs_copy(xs_hbm, sem.at[0], td * TOP_K).wait()

    @pl.when(step == pl.num_programs(0) - 1)
    def _():
        zero_fill(lambda cp: cp.wait())


def _dispatch(pad_first, pad_len, n_used, dest, h_slabs, n_rows):
    t = h_slabs.shape[0]
    td = min(DISPATCH_TOKENS, t)
    grid_spec = pltpu.PrefetchScalarGridSpec(
        num_scalar_prefetch=3,
        grid=(t // td,),
        in_specs=[
            pl.BlockSpec((TOP_K, td), lambda i, pf, pn, nu: (0, i), memory_space=pltpu.SMEM),
            pl.BlockSpec((td, SLAB, LANES), lambda i, pf, pn, nu: (i, 0, 0)),
        ],
        out_specs=pl.BlockSpec(memory_space=pl.ANY),
        scratch_shapes=[
            pltpu.VMEM((EXPERT_ROWS, SLAB, LANES), h_slabs.dtype),
            pltpu.SemaphoreType.DMA((2,)),
        ],
    )
    return pl.pallas_call(
        _dispatch_kernel,
        out_shape=jax.ShapeDtypeStruct((n_rows, SLAB, LANES), h_slabs.dtype),
        grid_spec=grid_spec,
        compiler_params=pltpu.CompilerParams(
            dimension_semantics=("arbitrary",), vmem_limit_bytes=VMEM_LIMIT),
        name="dispatch",
    )(pad_first, pad_len, n_used, dest, h_slabs)


def _experts_kernel(block_expert, block_new_expert, block_slot, block_next_expert, n_blocks,
                    xs_ref, wg_hbm, wu_hbm, wd_hbm, ys_ref,
                    wg_f, wu_f, wd_f, wg_b, wu_b, wd_b, slab_ref, sem):
    i = pl.program_id(0)

    def weight_copies(e, slot):
        return [pltpu.make_async_copy(hbm.at[e], buf.at[slot], sem.at[slot])
                for hbm, buf in ((wg_hbm, wg_f), (wu_hbm, wu_f), (wd_hbm, wd_f))]

    @pl.when(i < n_blocks[0])
    def _():
        @pl.when(block_new_expert[i] == 1)
        def _():
            slot = block_slot[i]

            @pl.when(i == 0)
            def _():
                for cp in weight_copies(block_expert[i], slot):
                    cp.start()

            for cp in weight_copies(block_expert[i], slot):
                cp.wait()
            wg_b[...] = wg_f[slot].astype(BF16)
            wu_b[...] = wu_f[slot].astype(BF16)
            wd_b[...] = wd_f[slot].astype(BF16)

            @pl.when(block_next_expert[i] >= 0)
            def _():
                for cp in weight_copies(block_next_expert[i], 1 - slot):
                    cp.start()

        xb = _from_slabs(xs_ref[...], slab_ref).astype(BF16)
        g = _dot(xb, wg_b[...])
        u = _dot(xb, wu_b[...])
        y = _dot((_silu(g) * u).astype(BF16), wd_b[...])
        ys_ref[...] = _to_slabs(y, slab_ref)


def _experts(meta, xs, w_gate, w_up, w_down):
    n_scalar = len(meta)
    _, d, de = w_gate.shape
    n_max = meta[0].shape[0]
    blk = lambda i, be, ne, sl, nx, nb: (jnp.minimum(i, nb[0] - 1), 0, 0)
    grid_spec = pltpu.PrefetchScalarGridSpec(
        num_scalar_prefetch=n_scalar,
        grid=(n_max,),
        in_specs=[
            pl.BlockSpec((EXPERT_ROWS, SLAB, LANES), blk),
            pl.BlockSpec(memory_space=pl.ANY),
            pl.BlockSpec(memory_space=pl.ANY),
            pl.BlockSpec(memory_space=pl.ANY),
        ],
        out_specs=pl.BlockSpec((EXPERT_ROWS, SLAB, LANES), blk),
        scratch_shapes=[
            pltpu.VMEM((2, d, de), F32),
            pltpu.VMEM((2, d, de), F32),
            pltpu.VMEM((2, de, d), F32),
            pltpu.VMEM((d, de), BF16),
            pltpu.VMEM((d, de), BF16),
            pltpu.VMEM((de, d), BF16),
            pltpu.VMEM((EXPERT_ROWS * SLAB, LANES), F32),
            pltpu.SemaphoreType.DMA((2,)),
        ],
    )
    return pl.pallas_call(
        _experts_kernel,
        out_shape=jax.ShapeDtypeStruct(xs.shape, xs.dtype),
        grid_spec=grid_spec,
        input_output_aliases={n_scalar: 0},
        compiler_params=pltpu.CompilerParams(
            dimension_semantics=("arbitrary",), vmem_limit_bytes=VMEM_LIMIT),
        name="experts",
    )(*meta, xs, w_gate, w_up, w_down)


def _expert_blocks(counts, n_assign):
    n_exp = counts.shape[0]
    n_max = (n_assign + n_exp * (EXPERT_ROWS - 1)) // EXPERT_ROWS + 1
    padded = (counts + EXPERT_ROWS - 1) // EXPERT_ROWS * EXPERT_ROWS
    pad_end = jnp.cumsum(padded)
    pad_start = pad_end - padded
    n_used = pad_end[-1] // EXPERT_ROWS
    first_row = jnp.minimum(jnp.arange(n_max, dtype=I32), n_used - 1) * EXPERT_ROWS
    e_of = jnp.sum((first_row[:, None] >= pad_end[None, :]).astype(I32), axis=1)
    e_of = jnp.minimum(e_of, n_exp - 1).astype(I32)
    prev_e = jnp.concatenate([jnp.full((1,), -1, I32), e_of[:-1]])
    new_expert = (e_of != prev_e).astype(I32)
    slot = ((jnp.cumsum(new_expert) - 1) & 1).astype(I32)
    ids = jnp.arange(n_exp, dtype=I32)
    later_used = (ids[None, :] > ids[:, None]) & (padded[None, :] > 0)
    next_used = jnp.min(jnp.where(later_used, ids[None, :], n_exp), axis=1)
    next_used = jnp.where(next_used < n_exp, next_used, -1)
    next_expert = jnp.sum(jnp.where(e_of[:, None] == ids[None, :], next_used[None, :], 0), axis=1).astype(I32)
    n_used = n_used.reshape(1).astype(I32)
    meta = (e_of, new_expert, slot, next_expert, n_used)
    return (meta, n_used, pad_start.astype(I32), (pad_start + counts).astype(I32),
            (padded - counts).astype(I32), n_max)


def _combine_kernel(dest_cur, dest_nxt, x1_ref, mod_ref, wt_ref, wsg_ref, wsu_ref, wsd_ref,
                    ln2g_ref, ln2b_ref, ys_hbm, out_ref, buf, wb_ref, acc_ref, sem, *, alpha):
    tc = x1_ref.shape[0]
    i = pl.program_id(0)
    n = pl.num_programs(0)
    slot = i % 2

    def request(dest_ref, s, t):
        for k in range(TOP_K):
            pltpu.make_async_copy(ys_hbm.at[dest_ref[k, t]], buf.at[s, k, t], sem.at[s]).start(priority=k % 2)

    @pl.when(i == 0)
    def _():
        def body(t, carry):
            request(dest_cur, 0, t)
            return carry
        lax.fori_loop(0, tc, body, 0)

    for k in range(TOP_K):
        pltpu.make_async_copy(ys_hbm.at[pl.ds(0, tc)], buf.at[slot, k], sem.at[slot]).wait()

    wt = wt_ref[...]
    for k in range(TOP_K):
        wb_ref[k] = jnp.broadcast_to(wt[:, k:k + 1], (tc, LANES))

    def token(t, carry, prefetch):
        if prefetch:
            request(dest_nxt, 1 - slot, t)
        terms = [jnp.broadcast_to(wb_ref[k, pl.ds(t, 1), :], (SLAB, LANES)) * buf[slot, k, t].astype(F32)
                 for k in range(TOP_K)]
        while len(terms) > 1:
            terms = [a + b for a, b in zip(terms[0::2], terms[1::2])]
        acc_ref[pl.ds(pl.multiple_of(t * SLAB, SLAB), SLAB), :] = terms[0]
        return carry

    @pl.when(i + 1 < n)
    def _():
        lax.fori_loop(0, tc, functools.partial(token, prefetch=True), 0, unroll=2)

    @pl.when(i + 1 == n)
    def _():
        lax.fori_loop(0, tc, functools.partial(token, prefetch=False), 0, unroll=2)

    mod = mod_ref[0]
    sh_f, sc_f, gt_f = mod[3:4], mod[4:5], mod[5:6]
    x1 = x1_ref[...]
    hb = (_ln(x1) * (1.0 + sc_f) + sh_f).astype(BF16)
    mid = _silu(_dot(hb, wsg_ref[...])) * _dot(hb, wsu_ref[...])
    ffn = _dot(mid.astype(BF16), wsd_ref[...])
    ffn = _slabs_to_rows(acc_ref, tc) + ffn
    out_ref[...] = _ln(alpha * x1 + (1.0 + gt_f) * ffn) * ln2g_ref[...] + ln2b_ref[...]


def _combine(dest, x1_flat, mod, wt, ws_gate, ws_up, ws_down, ln2_g, ln2_b, ys, seq, alpha):
    t, d = x1_flat.shape
    tc = min(COMBINE_TOKENS, seq)
    n = t // tc
    per_seq = seq // tc
    const = lambda i: (0, 0)
    return pl.pallas_call(
        functools.partial(_combine_kernel, alpha=alpha),
        out_shape=jax.ShapeDtypeStruct((t, d), F32),
        grid=(n,),
        in_specs=[
            pl.BlockSpec((TOP_K, tc), lambda i: (0, i), memory_space=pltpu.SMEM),
            pl.BlockSpec((TOP_K, tc), lambda i: (0, jnp.minimum(i + 1, n - 1)), memory_space=pltpu.SMEM),
            pl.BlockSpec((tc, d), lambda i: (i, 0)),
            pl.BlockSpec((1, SUBLANES, d), lambda i: (i // per_seq, 0, 0)),
            pl.BlockSpec((tc, LANES), lambda i: (i, 0)),
            pl.BlockSpec(ws_gate.shape, const),
            pl.BlockSpec(ws_up.shape, const),
            pl.BlockSpec(ws_down.shape, const),
            pl.BlockSpec(ln2_g.shape, const),
            pl.BlockSpec(ln2_b.shape, const),
            pl.BlockSpec(memory_space=pl.ANY),
        ],
        out_specs=pl.BlockSpec((tc, d), lambda i: (i, 0)),
        scratch_shapes=[
            pltpu.VMEM((2, TOP_K, tc, SLAB, LANES), BF16),
            pltpu.VMEM((TOP_K, tc, LANES), F32),
            pltpu.VMEM((tc * SLAB, LANES), F32),
            pltpu.SemaphoreType.DMA((2,)),
        ],
        compiler_params=pltpu.CompilerParams(
            dimension_semantics=("arbitrary",), vmem_limit_bytes=VMEM_LIMIT),
        name="combine",
    )(dest, dest, x1_flat, mod, wt, ws_gate, ws_up, ws_down, ln2_g, ln2_b, ys)


def _layer(x, c_pad, lw, alpha):
    bsz, seq, d = x.shape
    t = bsz * seq
    n_exp = lw["w_router"].shape[1]
    h_count = lw["w_if"].shape[1] // 2

    mod = _ada(c_pad, lw["w_ada"], lw["b_ada"].reshape(1, -1))[:bsz].reshape(bsz, 6, d)
    mod = jnp.pad(mod, ((0, 0), (0, SUBLANES - 6), (0, 0)))

    row2 = lambda a: a.reshape(1, -1)
    w_if = lw["w_if"]
    w_if_pad = jnp.zeros((w_if.shape[0], 2 * LANES), F32)
    w_if_pad = w_if_pad.at[:, :h_count].set(w_if[:, :h_count]).at[:, LANES:LANES + h_count].set(w_if[:, h_count:])
    b_if_pad = jnp.zeros((1, 2 * LANES), F32)
    b_if_pad = b_if_pad.at[0, :h_count].set(lw["b_if"][:h_count]).at[0, LANES:LANES + h_count].set(lw["b_if"][h_count:])
    w_rt = lw["w_router"].astype(F32).T
    wr_hi = w_rt.astype(BF16)
    wr_lo = (w_rt - wr_hi.astype(F32)).astype(BF16)
    p = {
        "w_in": lw["w_in"].astype(BF16), "conv_w": lw["conv_w"], "conv_b": row2(lw["conv_b"]),
        "w_q": lw["w_q"].astype(BF16), "w_k": lw["w_k"].astype(BF16), "w_v": lw["w_v"].astype(BF16),
        "w_if": w_if_pad.astype(BF16), "b_if": b_if_pad,
        "mh_g": row2(lw["mh_g"]), "skip": row2(lw["skip"]), "sg_g": row2(lw["sg_g"]), "sg_b": row2(lw["sg_b"]),
        "w_sp": lw["w_sp"], "b_sp_t": lw["b_sp"].T, "w_out": lw["w_out"].astype(BF16),
        "ln1_g": row2(lw["ln1_g"]), "ln1_b": row2(lw["ln1_b"]), "wr_hi": wr_hi, "wr_lo": wr_lo,
    }
    x1, h2, logt = _mixer(x, mod, p, alpha)

    tr = min(ROUTE_TOKENS, t)
    upper = (jnp.arange(tr)[:, None] < jnp.arange(tr)[None, :]).astype(BF16)
    idx, _, rank, wt, cnt = _route(logt, lw["e_bias"].astype(F32).reshape(n_exp, 1), upper)
    counts = cnt[:, 0].astype(I32)
    meta, n_used, pad_start, pad_first, pad_len, n_blocks_max = _expert_blocks(counts, t * TOP_K)
    dest = _dest(idx, rank, pad_start.reshape(n_exp, 1))

    xs = _dispatch(pad_first, pad_len, n_used, dest, h2, n_blocks_max * EXPERT_ROWS)
    ys = _experts(meta, xs, lw["w_gate"], lw["w_up"], lw["w_down"])
    out = _combine(dest, x1.reshape(t, d), mod, wt, lw["ws_gate"].astype(BF16), lw["ws_up"].astype(BF16),
                   lw["ws_down"].astype(BF16), row2(lw["ln2_g"]), row2(lw["ln2_b"]), ys, seq, alpha)
    return out.reshape(bsz, seq, d)


def kernel(x, c, w_ada, b_ada, w_in, conv_w, conv_b, w_q, w_k, w_v, w_if, b_if, mh_g, skip, sg_g, sg_b, w_sp, b_sp, w_out, ln1_g, ln1_b, w_router, e_bias, w_gate, w_up, w_down, ws_gate, ws_up, ws_down, ln2_g, ln2_b):
    stacked = dict(w_ada=w_ada, b_ada=b_ada, w_in=w_in, conv_w=conv_w, conv_b=conv_b, w_q=w_q, w_k=w_k,
                   w_v=w_v, w_if=w_if, b_if=b_if, mh_g=mh_g, skip=skip, sg_g=sg_g, sg_b=sg_b, w_sp=w_sp,
                   b_sp=b_sp, w_out=w_out, ln1_g=ln1_g, ln1_b=ln1_b, w_router=w_router, e_bias=e_bias,
                   w_gate=w_gate, w_up=w_up, w_down=w_down, ws_gate=ws_gate, ws_up=ws_up, ws_down=ws_down,
                   ln2_g=ln2_g, ln2_b=ln2_b)
    depth = w_ada.shape[0]
    alpha = float((2 * depth) ** 0.25)
    bsz = x.shape[0]
    c_pad = jnp.pad(c, ((0, -bsz % SUBLANES), (0, 0)))
    for l in range(depth):
        x = _layer(x, c_pad, {k: v[l] for k, v in stacked.items()}, alpha)
    return x
```

```python
import functools
import math

import jax
import jax.numpy as jnp
from jax import lax
from jax.experimental import pallas as pl
from jax.experimental.pallas import tpu as pltpu

F32 = jnp.float32
BF16 = jnp.bfloat16
I32 = jnp.int32

LN_EPS = 1e-5
M_HEADS = 4
G_GROUPS = 4
CHUNK = 128
N_GROUPS = 8
TOPK_GROUPS = 4
TOP_K = 8
ROUTE_SCALE = 2.5
LANES = 128
SUBLANES = 8

MIXER_TOKENS = 512
ROUTE_TOKENS = 1024
DEST_TOKENS = 2048
DISPATCH_TOKENS = 1024
EXPERT_ROWS = 512
COMBINE_TOKENS = 256
VMEM_LIMIT = 56 * 1024 * 1024

NEG_INF = float("-inf")


def _ln(x):
    mu = jnp.mean(x, axis=-1, keepdims=True)
    xc = x - mu
    var = jnp.mean(xc * xc, axis=-1, keepdims=True)
    return xc * lax.rsqrt(var + LN_EPS)


def _dot(a, b):
    return jnp.dot(a, b, preferred_element_type=F32)


def _dot_nt(a, b):
    return lax.dot_general(a, b, (((1,), (1,)), ((), ())), preferred_element_type=F32)


def _dot_exact(a, b):
    return jnp.dot(a, b, preferred_element_type=F32, precision=lax.Precision.HIGHEST)


def _silu(x):
    return x * jax.nn.sigmoid(x)


def _gelu(x):
    return 0.5 * x * (1.0 + lax.erf(x * math.sqrt(0.5)))


def _log_sigmoid(x):
    return jnp.minimum(x, 0.0) - jnp.log1p(jnp.exp(-jnp.abs(x)))


SLAB = SUBLANES


def _to_slabs(x, scratch):
    rows, d = x.shape
    assert d == SLAB * LANES
    for s in range(SLAB):
        scratch[pl.ds(s, rows, stride=SLAB), :] = x[:, s * LANES:(s + 1) * LANES]
    return scratch[0:rows * SLAB, :].reshape(rows, SLAB, LANES).astype(BF16)


def _slabs_to_rows(scratch, rows):
    return jnp.concatenate([scratch[pl.ds(s, rows, stride=SLAB), :] for s in range(SLAB)], axis=1)


def _from_slabs(x3, scratch):
    rows = x3.shape[0]
    scratch[0:rows * SLAB, :] = x3.astype(F32).reshape(rows * SLAB, LANES)
    return _slabs_to_rows(scratch, rows)


def _ada_kernel(c_ref, w_ref, b_ref, o_ref):
    o_ref[...] = _dot_exact(_silu(c_ref[...]), w_ref[...]) + b_ref[...]


def _ada(c_pad, w_ada, b_ada):
    rows, d = c_pad.shape
    n = w_ada.shape[1]
    return pl.pallas_call(
        _ada_kernel,
        out_shape=jax.ShapeDtypeStruct((rows, n), F32),
        grid=(n // d,),
        in_specs=[
            pl.BlockSpec((rows, d), lambda j: (0, 0)),
            pl.BlockSpec((d, d), lambda j: (0, j)),
            pl.BlockSpec((1, d), lambda j: (0, j)),
        ],
        out_specs=pl.BlockSpec((rows, d), lambda j: (0, j)),
        compiler_params=pltpu.CompilerParams(vmem_limit_bytes=VMEM_LIMIT),
        name="ada",
    )(c_pad, w_ada, b_ada)


def _mixer_kernel(x_ref, mod_ref, w_in_ref, conv_w_ref, conv_b_ref, wq_ref, wk_ref, wv_ref,
                  wif_ref, bif_ref, mhg_ref, skip_ref, sgg_ref, sgb_ref, wsp_ref, bspt_ref,
                  w_out_ref, ln1g_ref, ln1b_ref, wr_hi_ref, wr_lo_ref,
                  x1_ref, h2_ref, logt_ref,
                  state_ref, m_ref, xm_ref, cat_ref, slab_ref, *, alpha):
    tm = x_ref.shape[1]
    mw = conv_w_ref.shape[1]
    dh = mw // M_HEADS
    gw = sgg_ref.shape[1]
    gd = gw // G_GROUPS
    conv_k = conv_w_ref.shape[0]
    nch = tm // CHUNK
    L = CHUNK

    @pl.when(pl.program_id(1) == 0)
    def _():
        state_ref[...] = jnp.zeros_like(state_ref)
        m_ref[...] = jnp.zeros_like(m_ref)
        xm_ref[0:SUBLANES, :] = jnp.zeros((SUBLANES, mw), F32)

    x = x_ref[0]
    mod = mod_ref[0]
    sh_a, sc_a, gt_a = mod[0:1], mod[1:2], mod[2:3]
    sh_f, sc_f = mod[3:4], mod[4:5]

    h = _ln(x) * (1.0 + sc_a) + sh_a
    proj = _dot(h.astype(BF16), w_in_ref[...])
    xm = proj[:, :mw]
    z = proj[:, mw:2 * mw]
    u = proj[:, 2 * mw:2 * mw + gw]
    v = proj[:, 2 * mw + gw:]

    xm_ref[SUBLANES:SUBLANES + tm, :] = xm
    conv = jnp.broadcast_to(conv_b_ref[...], (tm, mw))
    for j in range(conv_k):
        off = SUBLANES - (conv_k - 1) + j
        conv = conv + conv_w_ref[j:j + 1, :] * xm_ref[off:off + tm, :]
    xm_ref[0:SUBLANES, :] = xm_ref[tm:tm + SUBLANES, :]
    xc = _silu(conv)

    scale = dh ** -0.5
    qs, ks, vs = [], [], []
    for hd in range(M_HEADS):
        sl = slice(hd * dh, (hd + 1) * dh)
        xch = xc[:, sl].astype(BF16)
        qs.append(_dot(xch, wq_ref[hd]))
        ks.append(_dot(xch, wk_ref[hd]) * scale)
        vs.append(_dot(xm[:, sl].astype(BF16), wv_ref[hd]))
    qkv = jnp.concatenate(qs + ks + vs, axis=1).astype(BF16)
    gate = _dot(qkv, wif_ref[...]) + bif_ref[...]
    gi = gate[:, :LANES]
    lf = _log_sigmoid(gate[:, LANES:])

    row = lax.broadcasted_iota(I32, (L, L), 0)
    col = lax.broadcasted_iota(I32, (L, L), 1)
    causal = col <= row
    tri = jnp.where(causal, 1.0, 0.0).astype(F32)
    ones_l = jnp.ones((L, dh), F32)
    wsg = [jnp.where(causal, wsp_ref[gg], 0.0).astype(BF16) for gg in range(G_GROUPS)]
    state = [state_ref[hd] for hd in range(M_HEADS)]
    m_run = [m_ref[hd:hd + 1, 0:1] for hd in range(M_HEADS)]

    for c in range(nch):
        rs = slice(c * L, (c + 1) * L)
        bmat = _dot_exact(tri, lf[rs])
        rmat = gi[rs] - bmat
        rmat_t = rmat.T
        for hd in range(M_HEADS):
            sl = slice(hd * dh, (hd + 1) * dh)
            bcol = bmat[:, hd:hd + 1]
            rcol = rmat[:, hd:hd + 1]
            rrow = rmat_t[hd:hd + 1, :]
            g = bmat[L - 1:L, hd:hd + 1]
            m_prev = m_run[hd]
            q = qs[hd][rs]
            k = ks[hd][rs]
            vv = vs[hd][rs]
            qb = q.astype(BF16)

            log_d = jnp.where(causal, bcol + rrow, NEG_INF)
            log_inter = bcol + m_prev
            m_t = jnp.maximum(log_inter, jnp.max(log_d, axis=1, keepdims=True))
            p = jnp.exp(log_d - m_t) * _dot_nt(qb, k.astype(BF16))
            w_inter = jnp.exp(log_inter - m_t)
            vext = jnp.concatenate([vv, ones_l], axis=1).astype(BF16)
            sx = state[hd]
            out_ext = w_inter * _dot(qb, sx.astype(BF16)) + _dot(p.astype(BF16), vext)
            num = out_ext[:, :dh]
            nq = out_ext[:, dh:]
            hh = num / jnp.maximum(jnp.abs(nq), jnp.exp(-m_t))

            a = g + rcol
            m_new = jnp.maximum(g + m_prev, jnp.max(a, axis=0, keepdims=True))
            decay = jnp.exp(g + m_prev - m_new)
            kw = k * jnp.exp(a - m_new)
            state[hd] = decay * sx + _dot(kw.T.astype(BF16), vext)
            m_run[hd] = m_new

            hc = _ln(hh) * mhg_ref[:, sl]
            hm = jax.nn.sigmoid(z[rs, sl]) * (hc + skip_ref[:, sl] * xc[rs, sl])
            cat_ref[rs, sl] = hm.astype(BF16)

        ug = _gelu(u[rs])
        vg = _gelu(v[rs])
        for gg in range(G_GROUPS):
            sl = slice(gg * gd, (gg + 1) * gd)
            vn = (_ln(vg[:, sl]) * sgg_ref[:, sl] + sgb_ref[:, sl]).astype(BF16)
            s = _dot(wsg[gg], vn) + bspt_ref[:, gg:gg + 1]
            cat_ref[rs, mw + gg * gd:mw + (gg + 1) * gd] = (ug[:, sl] * s).astype(BF16)

    for hd in range(M_HEADS):
        state_ref[hd] = state[hd]
        m_ref[hd:hd + 1, :] = jnp.broadcast_to(m_run[hd], (1, LANES))

    mix = _dot(cat_ref[...], w_out_ref[...])
    x1 = _ln(alpha * x + (1.0 + gt_a) * mix) * ln1g_ref[...] + ln1b_ref[...]
    x1_ref[0] = x1
    h2 = _ln(x1) * (1.0 + sc_f) + sh_f
    h2_ref[...] = _to_slabs(h2, slab_ref)
    h_hi = h2.astype(BF16)
    h_lo = (h2 - h_hi.astype(F32)).astype(BF16)
    wr_hi = wr_hi_ref[...]
    logt_ref[...] = _dot_nt(wr_hi, h_hi) + _dot_nt(wr_hi, h_lo) + _dot_nt(wr_lo_ref[...], h_hi)


def _mixer(x, mod, p, alpha):
    bsz, seq, d = x.shape
    tm = min(MIXER_TOKENS, seq)
    nt = seq // tm
    mw = p["conv_w"].shape[1]
    gw = p["sg_g"].shape[1]
    dh = mw // M_HEADS
    n_exp = p["wr_hi"].shape[0]
    const2 = lambda b, i: (0, 0)
    const3 = lambda b, i: (0, 0, 0)
    full = lambda a: pl.BlockSpec(a.shape, const2 if a.ndim == 2 else const3)
    names = ["w_in", "conv_w", "conv_b", "w_q", "w_k", "w_v", "w_if", "b_if", "mh_g", "skip", "sg_g",
             "sg_b", "w_sp", "b_sp_t", "w_out", "ln1_g", "ln1_b", "wr_hi", "wr_lo"]
    weights = [p[n] for n in names]
    return pl.pallas_call(
        functools.partial(_mixer_kernel, alpha=alpha),
        out_shape=(
            jax.ShapeDtypeStruct((bsz, seq, d), F32),
            jax.ShapeDtypeStruct((bsz * seq, SLAB, LANES), BF16),
            jax.ShapeDtypeStruct((n_exp, bsz * seq), F32),
        ),
        grid=(bsz, nt),
        in_specs=[
            pl.BlockSpec((1, tm, d), lambda b, i: (b, i, 0)),
            pl.BlockSpec((1, SUBLANES, d), lambda b, i: (b, 0, 0)),
        ] + [full(w) for w in weights],
        out_specs=(
            pl.BlockSpec((1, tm, d), lambda b, i: (b, i, 0)),
            pl.BlockSpec((tm, SLAB, LANES), lambda b, i: (b * nt + i, 0, 0)),
            pl.BlockSpec((n_exp, tm), lambda b, i: (0, b * nt + i)),
        ),
        scratch_shapes=[
            pltpu.VMEM((M_HEADS, dh, 2 * dh), F32),
            pltpu.VMEM((SUBLANES, LANES), F32),
            pltpu.VMEM((tm + SUBLANES, mw), F32),
            pltpu.VMEM((tm, mw + gw), BF16),
            pltpu.VMEM((tm * SLAB, LANES), F32),
        ],
        compiler_params=pltpu.CompilerParams(
            dimension_semantics=("arbitrary", "arbitrary"), vmem_limit_bytes=VMEM_LIMIT),
        name="mixer",
    )(x, mod, *weights)


def _route_kernel(logt_ref, bias_ref, upper_ref, idx_ref, w_ref, rank_ref, wt_ref, cnt_ref, carry_ref):
    n_exp, tr = logt_ref.shape
    epg = n_exp // N_GROUPS

    @pl.when(pl.program_id(0) == 0)
    def _():
        carry_ref[...] = jnp.zeros_like(carry_ref)

    scores = jax.nn.sigmoid(logt_ref[...])
    sel = scores + bias_ref[...]

    sel3 = sel.reshape(N_GROUPS, epg, tr)
    io3 = lax.broadcasted_iota(I32, (N_GROUPS, epg, tr), 1)
    m1 = jnp.max(sel3, axis=1, keepdims=True)
    first = jnp.min(jnp.where(sel3 == m1, io3, epg), axis=1, keepdims=True)
    m2 = jnp.max(jnp.where(io3 == first, NEG_INF, sel3), axis=1, keepdims=True)
    gs = (m1 + m2).reshape(N_GROUPS, tr)

    gio = lax.broadcasted_iota(I32, (N_GROUPS, tr), 0)
    gmask = jnp.zeros((N_GROUPS, tr), F32)
    for _ in range(TOPK_GROUPS):
        m = jnp.max(gs, axis=0, keepdims=True)
        gi = jnp.min(jnp.where(gs == m, gio, N_GROUPS), axis=0, keepdims=True)
        hit = gio == gi
        gmask = jnp.where(hit, 1.0, gmask)
        gs = jnp.where(hit, NEG_INF, gs)
    emask = jnp.broadcast_to(gmask.reshape(N_GROUPS, 1, tr), (N_GROUPS, epg, tr)).reshape(n_exp, tr)
    selm = jnp.where(emask > 0.0, sel, NEG_INF)

    eio = lax.broadcasted_iota(I32, (n_exp, tr), 0)
    chosen = jnp.zeros((n_exp, tr), F32)
    idx_rows, w_rows = [], []
    for _ in range(TOP_K):
        m = jnp.max(selm, axis=0, keepdims=True)
        ei = jnp.min(jnp.where(selm == m, eio, n_exp), axis=0, keepdims=True)
        hit = eio == ei
        w_rows.append(jnp.sum(jnp.where(hit, scores, 0.0), axis=0, keepdims=True))
        idx_rows.append(ei)
        selm = jnp.where(hit, NEG_INF, selm)
        chosen = jnp.where(hit, 1.0, chosen)

    chosen_b = chosen.astype(BF16)
    carry = carry_ref[...]
    ranks = carry[:, 0:1] + _dot(chosen_b, upper_ref[...])
    carry_new = carry + _dot(chosen_b, jnp.ones((tr, LANES), BF16))
    carry_ref[...] = carry_new
    cnt_ref[...] = carry_new
    rank_rows = [jnp.sum(jnp.where(eio == ei, ranks, 0.0), axis=0, keepdims=True) for ei in idx_rows]

    wsum = w_rows[0]
    for wk in w_rows[1:]:
        wsum = wsum + wk
    w8 = jnp.concatenate([wk / wsum * ROUTE_SCALE for wk in w_rows], axis=0)
    idx_ref[...] = jnp.concatenate(idx_rows, axis=0)
    rank_ref[...] = jnp.concatenate(rank_rows, axis=0).astype(I32)
    w_ref[...] = w8
    wpad = jnp.concatenate([w8, jnp.zeros((LANES - TOP_K, tr), F32)], axis=0)
    wt_ref[...] = wpad.T


def _route(logt, e_bias_col, upper):
    n_exp, t = logt.shape
    tr = upper.shape[0]
    return pl.pallas_call(
        _route_kernel,
        out_shape=(
            jax.ShapeDtypeStruct((TOP_K, t), I32),
            jax.ShapeDtypeStruct((TOP_K, t), F32),
            jax.ShapeDtypeStruct((TOP_K, t), I32),
            jax.ShapeDtypeStruct((t, LANES), F32),
            jax.ShapeDtypeStruct((n_exp, LANES), F32),
        ),
        grid=(t // tr,),
        in_specs=[
            pl.BlockSpec((n_exp, tr), lambda i: (0, i)),
            pl.BlockSpec((n_exp, 1), lambda i: (0, 0)),
            pl.BlockSpec((tr, tr), lambda i: (0, 0)),
        ],
        out_specs=(
            pl.BlockSpec((TOP_K, tr), lambda i: (0, i)),
            pl.BlockSpec((TOP_K, tr), lambda i: (0, i)),
            pl.BlockSpec((TOP_K, tr), lambda i: (0, i)),
            pl.BlockSpec((tr, LANES), lambda i: (i, 0)),
            pl.BlockSpec((n_exp, LANES), lambda i: (0, 0)),
        ),
        scratch_shapes=[pltpu.VMEM((n_exp, LANES), F32)],
        compiler_params=pltpu.CompilerParams(
            dimension_semantics=("arbitrary",), vmem_limit_bytes=VMEM_LIMIT),
        name="route",
    )(logt, e_bias_col, upper)


def _dest_kernel(idx_ref, rank_ref, start_ref, dest_ref):
    n_exp = start_ref.shape[0]
    tt = idx_ref.shape[1]
    eio = lax.broadcasted_iota(I32, (n_exp, tt), 0)
    start = start_ref[...]
    rows = []
    for k in range(TOP_K):
        hit = eio == idx_ref[k:k + 1, :]
        rows.append(jnp.sum(jnp.where(hit, start, 0), axis=0, keepdims=True) + rank_ref[k:k + 1, :])
    dest_ref[...] = jnp.concatenate(rows, axis=0)


def _dest(idx, rank, start_col):
    t = idx.shape[1]
    tt = min(DEST_TOKENS, t)
    n_exp = start_col.shape[0]
    return pl.pallas_call(
        _dest_kernel,
        out_shape=jax.ShapeDtypeStruct((TOP_K, t), I32),
        grid=(t // tt,),
        in_specs=[
            pl.BlockSpec((TOP_K, tt), lambda i: (0, i)),
            pl.BlockSpec((TOP_K, tt), lambda i: (0, i)),
            pl.BlockSpec((n_exp, 1), lambda i: (0, 0)),
        ],
        out_specs=pl.BlockSpec((TOP_K, tt), lambda i: (0, i)),
        compiler_params=pltpu.CompilerParams(vmem_limit_bytes=VMEM_LIMIT),
        name="dest",
    )(idx, rank, start_col)


def _slabs_copy(hbm, sem, n_slabs):
    return pltpu.make_async_copy(hbm.at[pl.ds(0, n_slabs)], hbm.at[pl.ds(0, n_slabs)], sem)


def _dispatch_kernel(pad_first, pad_len, n_used, dest_ref, h_ref, xs_hbm, zeros_ref, sem):
    td = dest_ref.shape[1]
    n_exp = pad_first.shape[0]
    n_blocks_max = xs_hbm.shape[0] // EXPERT_ROWS
    step = pl.program_id(0)

    def zero_fill(act):
        def zeros_to(first_row, n_rows):
            act(pltpu.make_async_copy(zeros_ref.at[pl.ds(0, n_rows)], xs_hbm.at[pl.ds(first_row, n_rows)],
                                      sem.at[1]))

        def expert_padding(e, carry):
            row = pad_first[e]
            for bit in reversed(range(EXPERT_ROWS.bit_length() - 1)):
                take = (pad_len[e] >> bit) & 1
                pl.when(take == 1)(functools.partial(zeros_to, row, 1 << bit))
                row = row + (take << bit)
            return carry

        def unused_block(b, carry):
            zeros_to(b * EXPERT_ROWS, EXPERT_ROWS)
            return carry

        lax.fori_loop(0, n_exp, expert_padding, 0)
        lax.fori_loop(n_used[0], n_blocks_max, unused_block, 0)

    @pl.when(step == 0)
    def _():
        zeros_ref[...] = jnp.zeros_like(zeros_ref)
        zero_fill(lambda cp: cp.start())

    def body(t, carry):
        for k in range(TOP_K):
            pltpu.make_async_copy(h_ref.at[t], xs_hbm.at[dest_ref[k, t]], sem.at[0]).start(priority=k % 2)
        return carry

    lax.fori_loop(0, td, body, 0, unroll=2)
    _slabs_copy(xs_hbm, sem.at[0], td * TOP_K).wait()

    @pl.when(step == pl.num_programs(0) - 1)
    def _():
        zero_fill(lambda cp: cp.wait())


def _dispatch(pad_first, pad_len, n_used, dest, h_slabs, n_rows):
    t = h_slabs.shape[0]
    td = min(DISPATCH_TOKENS, t)
    grid_spec = pltpu.PrefetchScalarGridSpec(
        num_scalar_prefetch=3,
        grid=(t // td,),
        in_specs=[
            pl.BlockSpec((TOP_K, td), lambda i, pf, pn, nu: (0, i), memory_space=pltpu.SMEM),
            pl.BlockSpec((td, SLAB, LANES), lambda i, pf, pn, nu: (i, 0, 0)),
        ],
        out_specs=pl.BlockSpec(memory_space=pl.ANY),
        scratch_shapes=[
            pltpu.VMEM((EXPERT_ROWS, SLAB, LANES), h_slabs.dtype),
            pltpu.SemaphoreType.DMA((2,)),
        ],
    )
    return pl.pallas_call(
        _dispatch_kernel,
        out_shape=jax.ShapeDtypeStruct((n_rows, SLAB, LANES), h_slabs.dtype),
        grid_spec=grid_spec,
        compiler_params=pltpu.CompilerParams(
            dimension_semantics=("arbitrary",), vmem_limit_bytes=VMEM_LIMIT),
        name="dispatch",
    )(pad_first, pad_len, n_used, dest, h_slabs)


def _experts_kernel(block_expert, block_new_expert, block_slot, block_next_expert, n_blocks,
                    xs_ref, wg_hbm, wu_hbm, wd_hbm, ys_ref,
                    wg_f, wu_f, wd_f, wg_b, wu_b, wd_b, slab_ref, sem):
    i = pl.program_id(0)

    def weight_copies(e, slot):
        return [pltpu.make_async_copy(hbm.at[e], buf.at[slot], sem.at[slot])
                for hbm, buf in ((wg_hbm, wg_f), (wu_hbm, wu_f), (wd_hbm, wd_f))]

    @pl.when(i < n_blocks[0])
    def _():
        @pl.when(block_new_expert[i] == 1)
        def _():
            slot = block_slot[i]

            @pl.when(i == 0)
            def _():
                for cp in weight_copies(block_expert[i], slot):
                    cp.start()

            for cp in weight_copies(block_expert[i], slot):
                cp.wait()
            wg_b[...] = wg_f[slot].astype(BF16)
            wu_b[...] = wu_f[slot].astype(BF16)
            wd_b[...] = wd_f[slot].astype(BF16)

            @pl.when(block_next_expert[i] >= 0)
            def _():
                for cp in weight_copies(block_next_expert[i], 1 - slot):
                    cp.start()

        xb = _from_slabs(xs_ref[...], slab_ref).astype(BF16)
        g = _dot(xb, wg_b[...])
        u = _dot(xb, wu_b[...])
        y = _dot((_silu(g) * u).astype(BF16), wd_b[...])
        ys_ref[...] = _to_slabs(y, slab_ref)


def _experts(meta, xs, w_gate, w_up, w_down):
    n_scalar = len(meta)
    _, d, de = w_gate.shape
    n_max = meta[0].shape[0]
    blk = lambda i, be, ne, sl, nx, nb: (jnp.minimum(i, nb[0] - 1), 0, 0)
    grid_spec = pltpu.PrefetchScalarGridSpec(
        num_scalar_prefetch=n_scalar,
        grid=(n_max,),
        in_specs=[
            pl.BlockSpec((EXPERT_ROWS, SLAB, LANES), blk),
            pl.BlockSpec(memory_space=pl.ANY),
            pl.BlockSpec(memory_space=pl.ANY),
            pl.BlockSpec(memory_space=pl.ANY),
        ],
        out_specs=pl.BlockSpec((EXPERT_ROWS, SLAB, LANES), blk),
        scratch_shapes=[
            pltpu.VMEM((2, d, de), F32),
            pltpu.VMEM((2, d, de), F32),
            pltpu.VMEM((2, de, d), F32),
            pltpu.VMEM((d, de), BF16),
            pltpu.VMEM((d, de), BF16),
            pltpu.VMEM((de, d), BF16),
            pltpu.VMEM((EXPERT_ROWS * SLAB, LANES), F32),
            pltpu.SemaphoreType.DMA((2,)),
        ],
    )
    return pl.pallas_call(
        _experts_kernel,
        out_shape=jax.ShapeDtypeStruct(xs.shape, xs.dtype),
        grid_spec=grid_spec,
        input_output_aliases={n_scalar: 0},
        compiler_params=pltpu.CompilerParams(
            dimension_semantics=("arbitrary",), vmem_limit_bytes=VMEM_LIMIT),
        name="experts",
    )(*meta, xs, w_gate, w_up, w_down)


def _expert_blocks(counts, n_assign):
    n_exp = counts.shape[0]
    n_max = (n_assign + n_exp * (EXPERT_ROWS - 1)) // EXPERT_ROWS + 1
    padded = (counts + EXPERT_ROWS - 1) // EXPERT_ROWS * EXPERT_ROWS
    pad_end = jnp.cumsum(padded)
    pad_start = pad_end - padded
    n_used = pad_end[-1] // EXPERT_ROWS
    first_row = jnp.minimum(jnp.arange(n_max, dtype=I32), n_used - 1) * EXPERT_ROWS
    e_of = jnp.sum((first_row[:, None] >= pad_end[None, :]).astype(I32), axis=1)
    e_of = jnp.minimum(e_of, n_exp - 1).astype(I32)
    prev_e = jnp.concatenate([jnp.full((1,), -1, I32), e_of[:-1]])
    new_expert = (e_of != prev_e).astype(I32)
    slot = ((jnp.cumsum(new_expert) - 1) & 1).astype(I32)
    ids = jnp.arange(n_exp, dtype=I32)
    later_used = (ids[None, :] > ids[:, None]) & (padded[None, :] > 0)
    next_used = jnp.min(jnp.where(later_used, ids[None, :], n_exp), axis=1)
    next_used = jnp.where(next_used < n_exp, next_used, -1)
    next_expert = jnp.sum(jnp.where(e_of[:, None] == ids[None, :], next_used[None, :], 0), axis=1).astype(I32)
    n_used = n_used.reshape(1).astype(I32)
    meta = (e_of, new_expert, slot, next_expert, n_used)
    return (meta, n_used, pad_start.astype(I32), (pad_start + counts).astype(I32),
            (padded - counts).astype(I32), n_max)


def _combine_kernel(dest_cur, dest_nxt, x1_ref, mod_ref, wt_ref, wsg_ref, wsu_ref, wsd_ref,
                    ln2g_ref, ln2b_ref, ys_hbm, out_ref, buf, wb_ref, acc_ref, sem, *, alpha):
    tc = x1_ref.shape[0]
    i = pl.program_id(0)
    n = pl.num_programs(0)
    slot = i % 2

    def request(dest_ref, s, t):
        for k in range(TOP_K):
            pltpu.make_async_copy(ys_hbm.at[dest_ref[k, t]], buf.at[s, k, t], sem.at[s]).start(priority=k % 2)

    @pl.when(i == 0)
    def _():
        def body(t, carry):
            request(dest_cur, 0, t)
            return carry
        lax.fori_loop(0, tc, body, 0)

    for k in range(TOP_K):
        pltpu.make_async_copy(ys_hbm.at[pl.ds(0, tc)], buf.at[slot, k], sem.at[slot]).wait()

    wt = wt_ref[...]
    for k in range(TOP_K):
        wb_ref[k] = jnp.broadcast_to(wt[:, k:k + 1], (tc, LANES))

    def token(t, carry, prefetch):
        if prefetch:
            request(dest_nxt, 1 - slot, t)
        terms = [jnp.broadcast_to(wb_ref[k, pl.ds(t, 1), :], (SLAB, LANES)) * buf[slot, k, t].astype(F32)
                 for k in range(TOP_K)]
        while len(terms) > 1:
            terms = [a + b for a, b in zip(terms[0::2], terms[1::2])]
        acc_ref[pl.ds(pl.multiple_of(t * SLAB, SLAB), SLAB), :] = terms[0]
        return carry

    @pl.when(i + 1 < n)
    def _():
        lax.fori_loop(0, tc, functools.partial(token, prefetch=True), 0, unroll=8)

    @pl.when(i + 1 == n)
    def _():
        lax.fori_loop(0, tc, functools.partial(token, prefetch=False), 0, unroll=8)

    mod = mod_ref[0]
    sh_f, sc_f, gt_f = mod[3:4], mod[4:5], mod[5:6]
    x1 = x1_ref[...]
    hb = (_ln(x1) * (1.0 + sc_f) + sh_f).astype(BF16)
    mid = _silu(_dot(hb, wsg_ref[...])) * _dot(hb, wsu_ref[...])
    ffn = _dot(mid.astype(BF16), wsd_ref[...])
    ffn = _slabs_to_rows(acc_ref, tc) + ffn
    out_ref[...] = _ln(alpha * x1 + (1.0 + gt_f) * ffn) * ln2g_ref[...] + ln2b_ref[...]


def _combine(dest, x1_flat, mod, wt, ws_gate, ws_up, ws_down, ln2_g, ln2_b, ys, seq, alpha):
    t, d = x1_flat.shape
    tc = min(COMBINE_TOKENS, seq)
    n = t // tc
    per_seq = seq // tc
    const = lambda i: (0, 0)
    return pl.pallas_call(
        functools.partial(_combine_kernel, alpha=alpha),
        out_shape=jax.ShapeDtypeStruct((t, d), F32),
        grid=(n,),
        in_specs=[
            pl.BlockSpec((TOP_K, tc), lambda i: (0, i), memory_space=pltpu.SMEM),
            pl.BlockSpec((TOP_K, tc), lambda i: (0, jnp.minimum(i + 1, n - 1)), memory_space=pltpu.SMEM),
            pl.BlockSpec((tc, d), lambda i: (i, 0)),
            pl.BlockSpec((1, SUBLANES, d), lambda i: (i // per_seq, 0, 0)),
            pl.BlockSpec((tc, LANES), lambda i: (i, 0)),
            pl.BlockSpec(ws_gate.shape, const),
            pl.BlockSpec(ws_up.shape, const),
            pl.BlockSpec(ws_down.shape, const),
            pl.BlockSpec(ln2_g.shape, const),
            pl.BlockSpec(ln2_b.shape, const),
            pl.BlockSpec(memory_space=pl.ANY),
        ],
        out_specs=pl.BlockSpec((tc, d), lambda i: (i, 0)),
        scratch_shapes=[
            pltpu.VMEM((2, TOP_K, tc, SLAB, LANES), BF16),
            pltpu.VMEM((TOP_K, tc, LANES), F32),
            pltpu.VMEM((tc * SLAB, LANES), F32),
            pltpu.SemaphoreType.DMA((2,)),
        ],
        compiler_params=pltpu.CompilerParams(
            dimension_semantics=("arbitrary",), vmem_limit_bytes=VMEM_LIMIT),
        name="combine",
    )(dest, dest, x1_flat, mod, wt, ws_gate, ws_up, ws_down, ln2_g, ln2_b, ys)


def _layer(x, c_pad, lw, alpha):
    bsz, seq, d = x.shape
    t = bsz * seq
    n_exp = lw["w_router"].shape[1]
    h_count = lw["w_if"].shape[1] // 2

    mod = _ada(c_pad, lw["w_ada"], lw["b_ada"].reshape(1, -1))[:bsz].reshape(bsz, 6, d)
    mod = jnp.pad(mod, ((0, 0), (0, SUBLANES - 6), (0, 0)))

    row2 = lambda a: a.reshape(1, -1)
    w_if = lw["w_if"]
    w_if_pad = jnp.zeros((w_if.shape[0], 2 * LANES), F32)
    w_if_pad = w_if_pad.at[:, :h_count].set(w_if[:, :h_count]).at[:, LANES:LANES + h_count].set(w_if[:, h_count:])
    b_if_pad = jnp.zeros((1, 2 * LANES), F32)
    b_if_pad = b_if_pad.at[0, :h_count].set(lw["b_if"][:h_count]).at[0, LANES:LANES + h_count].set(lw["b_if"][h_count:])
    w_rt = lw["w_router"].astype(F32).T
    wr_hi = w_rt.astype(BF16)
    wr_lo = (w_rt - wr_hi.astype(F32)).astype(BF16)
    p = {
        "w_in": lw["w_in"].astype(BF16), "conv_w": lw["conv_w"], "conv_b": row2(lw["conv_b"]),
        "w_q": lw["w_q"].astype(BF16), "w_k": lw["w_k"].astype(BF16), "w_v": lw["w_v"].astype(BF16),
        "w_if": w_if_pad.astype(BF16), "b_if": b_if_pad,
        "mh_g": row2(lw["mh_g"]), "skip": row2(lw["skip"]), "sg_g": row2(lw["sg_g"]), "sg_b": row2(lw["sg_b"]),
        "w_sp": lw["w_sp"], "b_sp_t": lw["b_sp"].T, "w_out": lw["w_out"].astype(BF16),
        "ln1_g": row2(lw["ln1_g"]), "ln1_b": row2(lw["ln1_b"]), "wr_hi": wr_hi, "wr_lo": wr_lo,
    }
    x1, h2, logt = _mixer(x, mod, p, alpha)

    tr = min(ROUTE_TOKENS, t)
    upper = (jnp.arange(tr)[:, None] < jnp.arange(tr)[None, :]).astype(BF16)
    idx, _, rank, wt, cnt = _route(logt, lw["e_bias"].astype(F32).reshape(n_exp, 1), upper)
    counts = cnt[:, 0].astype(I32)
    meta, n_used, pad_start, pad_first, pad_len, n_blocks_max = _expert_blocks(counts, t * TOP_K)
    dest = _dest(idx, rank, pad_start.reshape(n_exp, 1))

    xs = _dispatch(pad_first, pad_len, n_used, dest, h2, n_blocks_max * EXPERT_ROWS)
    ys = _experts(meta, xs, lw["w_gate"], lw["w_up"], lw["w_down"])
    out = _combine(dest, x1.reshape(t, d), mod, wt, lw["ws_gate"].astype(BF16), lw["ws_up"].astype(BF16),
                   lw["ws_down"].astype(BF16), row2(lw["ln2_g"]), row2(lw["ln2_b"]), ys, seq, alpha)
    return out.reshape(bsz, seq, d)


def kernel(x, c, w_ada, b_ada, w_in, conv_w, conv_b, w_q, w_k, w_v, w_if, b_if, mh_g, skip, sg_g, sg_b, w_sp, b_sp, w_out, ln1_g, ln1_b, w_router, e_bias, w_gate, w_up, w_down, ws_gate, ws_up, ws_down, ln2_g, ln2_b):
    stacked = dict(w_ada=w_ada, b_ada=b_ada, w_in=w_in, conv_w=conv_w, conv_b=conv_b, w_q=w_q, w_k=w_k,
                   w_v=w_v, w_if=w_if, b_if=b_if, mh_g=mh_g, skip=skip, sg_g=sg_g, sg_b=sg_b, w_sp=w_sp,
                   b_sp=b_sp, w_out=w_out, ln1_g=ln1_g, ln1_b=ln1_b, w_router=w_router, e_bias=e_bias,
                   w_gate=w_gate, w_up=w_up, w_down=w_down, ws_gate=ws_gate, ws_up=ws_up, ws_down=ws_down,
                   ln2_g=ln2_g, ln2_b=ln2_b)
    depth = w_ada.shape[0]
    alpha = float((2 * depth) ** 0.25)
    bsz = x.shape[0]
    c_pad = jnp.pad(c, ((0, -bsz % SUBLANES), (0, 0)))
    for l in range(depth):
        x = _layer(x, c_pad, {k: v[l] for k, v in stacked.items()}, alpha)
    return x
```

```python
import functools
import math

import jax
import jax.numpy as jnp
from jax import lax
from jax.experimental import pallas as pl
from jax.experimental.pallas import tpu as pltpu

F32 = jnp.float32
BF16 = jnp.bfloat16
I32 = jnp.int32

LN_EPS = 1e-5
M_HEADS = 4
G_GROUPS = 4
CHUNK = 128
N_GROUPS = 8
TOPK_GROUPS = 4
TOP_K = 8
ROUTE_SCALE = 2.5
LANES = 128
SUBLANES = 8

MIXER_TOKENS = 512
MIXER_CHUNK_GROUP = 4
ROUTE_TOKENS = 1024
DEST_TOKENS = 2048
DISPATCH_TOKENS = 1024
EXPERT_ROWS = 512
COMBINE_TOKENS = 256
VMEM_LIMIT = 56 * 1024 * 1024

NEG_INF = float("-inf")


def _ln(x):
    mu = jnp.mean(x, axis=-1, keepdims=True)
    xc = x - mu
    var = jnp.mean(xc * xc, axis=-1, keepdims=True)
    return xc * lax.rsqrt(var + LN_EPS)


def _dot(a, b):
    return jnp.dot(a, b, preferred_element_type=F32)


def _dot_nt(a, b):
    return lax.dot_general(a, b, (((1,), (1,)), ((), ())), preferred_element_type=F32)


def _dot_exact(a, b):
    return jnp.dot(a, b, preferred_element_type=F32, precision=lax.Precision.HIGHEST)


def _silu(x):
    return x * jax.nn.sigmoid(x)


def _gelu(x):
    return 0.5 * x * (1.0 + lax.erf(x * math.sqrt(0.5)))


def _log_sigmoid(x):
    return jnp.minimum(x, 0.0) - jnp.log1p(jnp.exp(-jnp.abs(x)))


SLAB = SUBLANES


def _to_slabs(x, scratch):
    rows, d = x.shape
    assert d == SLAB * LANES
    for s in range(SLAB):
        scratch[pl.ds(s, rows, stride=SLAB), :] = x[:, s * LANES:(s + 1) * LANES]
    return scratch[0:rows * SLAB, :].reshape(rows, SLAB, LANES).astype(BF16)


def _slabs_to_rows(scratch, rows):
    return jnp.concatenate([scratch[pl.ds(s, rows, stride=SLAB), :] for s in range(SLAB)], axis=1)


def _from_slabs(x3, scratch):
    rows = x3.shape[0]
    scratch[0:rows * SLAB, :] = x3.astype(F32).reshape(rows * SLAB, LANES)
    return _slabs_to_rows(scratch, rows)


def _ada_kernel(c_ref, w_ref, b_ref, o_ref):
    o_ref[...] = _dot_exact(_silu(c_ref[...]), w_ref[...]) + b_ref[...]


def _ada(c_pad, w_ada, b_ada):
    rows, d = c_pad.shape
    n = w_ada.shape[1]
    return pl.pallas_call(
        _ada_kernel,
        out_shape=jax.ShapeDtypeStruct((rows, n), F32),
        grid=(n // d,),
        in_specs=[
            pl.BlockSpec((rows, d), lambda j: (0, 0)),
            pl.BlockSpec((d, d), lambda j: (0, j)),
            pl.BlockSpec((1, d), lambda j: (0, j)),
        ],
        out_specs=pl.BlockSpec((rows, d), lambda j: (0, j)),
        compiler_params=pltpu.CompilerParams(vmem_limit_bytes=VMEM_LIMIT),
        name="ada",
    )(c_pad, w_ada, b_ada)


def _mixer_kernel(x_ref, mod_ref, w_in_ref, conv_w_ref, conv_b_ref, wq_ref, wk_ref, wv_ref,
                  wif_ref, bif_ref, mhg_ref, skip_ref, sgg_ref, sgb_ref, wsp_ref, bspt_ref,
                  w_out_ref, ln1g_ref, ln1b_ref, wr_hi_ref, wr_lo_ref,
                  x1_ref, h2_ref, logt_ref,
                  state_ref, m_ref, xm_ref, cat_ref, slab_ref, *, alpha):
    tm = x_ref.shape[1]
    mw = conv_w_ref.shape[1]
    dh = mw // M_HEADS
    gw = sgg_ref.shape[1]
    gd = gw // G_GROUPS
    conv_k = conv_w_ref.shape[0]
    nch = tm // CHUNK
    L = CHUNK

    @pl.when(pl.program_id(1) == 0)
    def _():
        state_ref[...] = jnp.zeros_like(state_ref)
        m_ref[...] = jnp.zeros_like(m_ref)
        xm_ref[0:SUBLANES, :] = jnp.zeros((SUBLANES, mw), F32)

    x = x_ref[0]
    mod = mod_ref[0]
    sh_a, sc_a, gt_a = mod[0:1], mod[1:2], mod[2:3]
    sh_f, sc_f = mod[3:4], mod[4:5]

    h = _ln(x) * (1.0 + sc_a) + sh_a
    proj = _dot(h.astype(BF16), w_in_ref[...])
    xm = proj[:, :mw]
    z = proj[:, mw:2 * mw]
    u = proj[:, 2 * mw:2 * mw + gw]
    v = proj[:, 2 * mw + gw:]

    xm_ref[SUBLANES:SUBLANES + tm, :] = xm
    conv = jnp.broadcast_to(conv_b_ref[...], (tm, mw))
    for j in range(conv_k):
        off = SUBLANES - (conv_k - 1) + j
        conv = conv + conv_w_ref[j:j + 1, :] * xm_ref[off:off + tm, :]
    xm_ref[0:SUBLANES, :] = xm_ref[tm:tm + SUBLANES, :]
    xc = _silu(conv)

    scale = dh ** -0.5
    qs, ks, vs = [], [], []
    for hd in range(M_HEADS):
        sl = slice(hd * dh, (hd + 1) * dh)
        xch = xc[:, sl].astype(BF16)
        qs.append(_dot(xch, wq_ref[hd]))
        ks.append(_dot(xch, wk_ref[hd]) * scale)
        vs.append(_dot(xm[:, sl].astype(BF16), wv_ref[hd]))
    qkv = jnp.concatenate(qs + ks + vs, axis=1).astype(BF16)
    gate = _dot(qkv, wif_ref[...]) + bif_ref[...]
    gi = gate[:, :LANES]
    lf = _log_sigmoid(gate[:, LANES:])

    row = lax.broadcasted_iota(I32, (L, L), 0)
    col = lax.broadcasted_iota(I32, (L, L), 1)
    causal = col <= row
    tri = jnp.where(causal, 1.0, 0.0).astype(F32)
    ones_l = jnp.ones((L, dh), F32)
    wsg = [jnp.where(causal, wsp_ref[gg], 0.0).astype(BF16) for gg in range(G_GROUPS)]
    state = [state_ref[hd] for hd in range(M_HEADS)]
    m_run = [m_ref[hd:hd + 1, 0:1] for hd in range(M_HEADS)]

    heads = range(M_HEADS)
    hsl = [slice(hd * dh, (hd + 1) * dh) for hd in heads]
    groups = range(G_GROUPS)
    gsl = [slice(gg * gd, (gg + 1) * gd) for gg in groups]

    for c0 in range(0, nch, MIXER_CHUNK_GROUP):
        chunks = range(c0, min(c0 + MIXER_CHUNK_GROUP, nch))
        rsl = {c: slice(c * L, (c + 1) * L) for c in chunks}
        pairs = [(c, hd) for c in chunks for hd in heads]
        bmat = {c: _dot_exact(tri, lf[rsl[c]]) for c in chunks}
        rmat = {c: gi[rsl[c]] - bmat[c] for c in chunks}
        rmat_t = {c: rmat[c].T for c in chunks}
        bcol = {(c, hd): bmat[c][:, hd:hd + 1] for c, hd in pairs}
        rrow = {(c, hd): rmat_t[c][hd:hd + 1, :] for c, hd in pairs}
        g = {(c, hd): bmat[c][L - 1:L, hd:hd + 1] for c, hd in pairs}
        a = {(c, hd): g[c, hd] + rmat[c][:, hd:hd + 1] for c, hd in pairs}
        a_max = {pr: jnp.max(a[pr], axis=0, keepdims=True) for pr in pairs}

        m_in, m_out = {}, {}
        for c in chunks:
            for hd in heads:
                m_in[c, hd] = m_run[hd]
                m_out[c, hd] = jnp.maximum(g[c, hd] + m_run[hd], a_max[c, hd])
            m_run = [m_out[c, hd] for hd in heads]

        k = {(c, hd): ks[hd][rsl[c]] for c, hd in pairs}
        qb = {(c, hd): qs[hd][rsl[c]].astype(BF16) for c, hd in pairs}
        kb = {pr: k[pr].astype(BF16) for pr in pairs}
        vext = {(c, hd): jnp.concatenate([vs[hd][rsl[c]], ones_l], axis=1).astype(BF16)
                for c, hd in pairs}

        log_d = {pr: jnp.where(causal, bcol[pr] + rrow[pr], NEG_INF) for pr in pairs}
        log_inter = {pr: bcol[pr] + m_in[pr] for pr in pairs}
        m_t = {pr: jnp.maximum(log_inter[pr], jnp.max(log_d[pr], axis=1, keepdims=True)) for pr in pairs}
        qk = {pr: _dot_nt(qb[pr], kb[pr]) for pr in pairs}
        p = {pr: (jnp.exp(log_d[pr] - m_t[pr]) * qk[pr]).astype(BF16) for pr in pairs}
        w_inter = {pr: jnp.exp(log_inter[pr] - m_t[pr]) for pr in pairs}
        intra = {pr: _dot(p[pr], vext[pr]) for pr in pairs}
        decay = {pr: jnp.exp(g[pr] + m_in[pr] - m_out[pr]) for pr in pairs}
        kw_t = {pr: (k[pr] * jnp.exp(a[pr] - m_out[pr])).T.astype(BF16) for pr in pairs}
        update = {pr: _dot(kw_t[pr], vext[pr]) for pr in pairs}

        st_in = {}
        for c in chunks:
            for hd in heads:
                st_in[c, hd] = state[hd]
            state = [decay[c, hd] * state[hd] + update[c, hd] for hd in heads]

        inter = {pr: _dot(qb[pr], st_in[pr].astype(BF16)) for pr in pairs}
        out_ext = {pr: w_inter[pr] * inter[pr] + intra[pr] for pr in pairs}
        hh = {pr: out_ext[pr][:, :dh] / jnp.maximum(jnp.abs(out_ext[pr][:, dh:]), jnp.exp(-m_t[pr]))
              for pr in pairs}
        hc = {(c, hd): _ln(hh[c, hd]) * mhg_ref[:, hsl[hd]] for c, hd in pairs}
        for c, hd in pairs:
            hm = jax.nn.sigmoid(z[rsl[c], hsl[hd]]) * (hc[c, hd] + skip_ref[:, hsl[hd]] * xc[rsl[c], hsl[hd]])
            cat_ref[rsl[c], hsl[hd]] = hm.astype(BF16)

        for c in chunks:
            ug = _gelu(u[rsl[c]])
            vg = _gelu(v[rsl[c]])
            vn = [(_ln(vg[:, gsl[gg]]) * sgg_ref[:, gsl[gg]] + sgb_ref[:, gsl[gg]]).astype(BF16) for gg in groups]
            sp = [_dot(wsg[gg], vn[gg]) + bspt_ref[:, gg:gg + 1] for gg in groups]
            for gg in groups:
                cat_ref[rsl[c], mw + gg * gd:mw + (gg + 1) * gd] = (ug[:, gsl[gg]] * sp[gg]).astype(BF16)

    for hd in range(M_HEADS):
        state_ref[hd] = state[hd]
        m_ref[hd:hd + 1, :] = jnp.broadcast_to(m_run[hd], (1, LANES))

    mix = _dot(cat_ref[...], w_out_ref[...])
    x1 = _ln(alpha * x + (1.0 + gt_a) * mix) * ln1g_ref[...] + ln1b_ref[...]
    x1_ref[0] = x1
    h2 = _ln(x1) * (1.0 + sc_f) + sh_f
    h2_ref[...] = _to_slabs(h2, slab_ref)
    h_hi = h2.astype(BF16)
    h_lo = (h2 - h_hi.astype(F32)).astype(BF16)
    wr_hi = wr_hi_ref[...]
    logt_ref[...] = _dot_nt(wr_hi, h_hi) + _dot_nt(wr_hi, h_lo) + _dot_nt(wr_lo_ref[...], h_hi)


def _mixer(x, mod, p, alpha):
    bsz, seq, d = x.shape
    tm = min(MIXER_TOKENS, seq)
    nt = seq // tm
    mw = p["conv_w"].shape[1]
    gw = p["sg_g"].shape[1]
    dh = mw // M_HEADS
    n_exp = p["wr_hi"].shape[0]
    const2 = lambda b, i: (0, 0)
    const3 = lambda b, i: (0, 0, 0)
    full = lambda a: pl.BlockSpec(a.shape, const2 if a.ndim == 2 else const3)
    names = ["w_in", "conv_w", "conv_b", "w_q", "w_k", "w_v", "w_if", "b_if", "mh_g", "skip", "sg_g",
             "sg_b", "w_sp", "b_sp_t", "w_out", "ln1_g", "ln1_b", "wr_hi", "wr_lo"]
    weights = [p[n] for n in names]
    return pl.pallas_call(
        functools.partial(_mixer_kernel, alpha=alpha),
        out_shape=(
            jax.ShapeDtypeStruct((bsz, seq, d), F32),
            jax.ShapeDtypeStruct((bsz * seq, SLAB, LANES), BF16),
            jax.ShapeDtypeStruct((n_exp, bsz * seq), F32),
        ),
        grid=(bsz, nt),
        in_specs=[
            pl.BlockSpec((1, tm, d), lambda b, i: (b, i, 0)),
            pl.BlockSpec((1, SUBLANES, d), lambda b, i: (b, 0, 0)),
        ] + [full(w) for w in weights],
        out_specs=(
            pl.BlockSpec((1, tm, d), lambda b, i: (b, i, 0)),
            pl.BlockSpec((tm, SLAB, LANES), lambda b, i: (b * nt + i, 0, 0)),
            pl.BlockSpec((n_exp, tm), lambda b, i: (0, b * nt + i)),
        ),
        scratch_shapes=[
            pltpu.VMEM((M_HEADS, dh, 2 * dh), F32),
            pltpu.VMEM((SUBLANES, LANES), F32),
            pltpu.VMEM((tm + SUBLANES, mw), F32),
            pltpu.VMEM((tm, mw + gw), BF16),
            pltpu.VMEM((tm * SLAB, LANES), F32),
        ],
        compiler_params=pltpu.CompilerParams(
            dimension_semantics=("arbitrary", "arbitrary"), vmem_limit_bytes=VMEM_LIMIT),
        name="mixer",
    )(x, mod, *weights)


def _route_kernel(logt_ref, bias_ref, upper_ref, idx_ref, w_ref, rank_ref, wt_ref, cnt_ref, carry_ref):
    n_exp, tr = logt_ref.shape
    epg = n_exp // N_GROUPS

    @pl.when(pl.program_id(0) == 0)
    def _():
        carry_ref[...] = jnp.zeros_like(carry_ref)

    scores = jax.nn.sigmoid(logt_ref[...])
    sel = scores + bias_ref[...]

    sel3 = sel.reshape(N_GROUPS, epg, tr)
    io3 = lax.broadcasted_iota(I32, (N_GROUPS, epg, tr), 1)
    m1 = jnp.max(sel3, axis=1, keepdims=True)
    first = jnp.min(jnp.where(sel3 == m1, io3, epg), axis=1, keepdims=True)
    m2 = jnp.max(jnp.where(io3 == first, NEG_INF, sel3), axis=1, keepdims=True)
    gs = (m1 + m2).reshape(N_GROUPS, tr)

    gio = lax.broadcasted_iota(I32, (N_GROUPS, tr), 0)
    gmask = jnp.zeros((N_GROUPS, tr), F32)
    for _ in range(TOPK_GROUPS):
        m = jnp.max(gs, axis=0, keepdims=True)
        gi = jnp.min(jnp.where(gs == m, gio, N_GROUPS), axis=0, keepdims=True)
        hit = gio == gi
        gmask = jnp.where(hit, 1.0, gmask)
        gs = jnp.where(hit, NEG_INF, gs)
    emask = jnp.broadcast_to(gmask.reshape(N_GROUPS, 1, tr), (N_GROUPS, epg, tr)).reshape(n_exp, tr)
    selm = jnp.where(emask > 0.0, sel, NEG_INF)

    eio = lax.broadcasted_iota(I32, (n_exp, tr), 0)
    chosen = jnp.zeros((n_exp, tr), F32)
    idx_rows, w_rows = [], []
    for _ in range(TOP_K):
        m = jnp.max(selm, axis=0, keepdims=True)
        ei = jnp.min(jnp.where(selm == m, eio, n_exp), axis=0, keepdims=True)
        hit = eio == ei
        w_rows.append(jnp.sum(jnp.where(hit, scores, 0.0), axis=0, keepdims=True))
        idx_rows.append(ei)
        selm = jnp.where(hit, NEG_INF, selm)
        chosen = jnp.where(hit, 1.0, chosen)

    chosen_b = chosen.astype(BF16)
    carry = carry_ref[...]
    ranks = carry[:, 0:1] + _dot(chosen_b, upper_ref[...])
    carry_new = carry + _dot(chosen_b, jnp.ones((tr, LANES), BF16))
    carry_ref[...] = carry_new
    cnt_ref[...] = carry_new
    rank_rows = [jnp.sum(jnp.where(eio == ei, ranks, 0.0), axis=0, keepdims=True) for ei in idx_rows]

    wsum = w_rows[0]
    for wk in w_rows[1:]:
        wsum = wsum + wk
    w8 = jnp.concatenate([wk / wsum * ROUTE_SCALE for wk in w_rows], axis=0)
    idx_ref[...] = jnp.concatenate(idx_rows, axis=0)
    rank_ref[...] = jnp.concatenate(rank_rows, axis=0).astype(I32)
    w_ref[...] = w8
    wpad = jnp.concatenate([w8, jnp.zeros((LANES - TOP_K, tr), F32)], axis=0)
    wt_ref[...] = wpad.T


def _route(logt, e_bias_col, upper):
    n_exp, t = logt.shape
    tr = upper.shape[0]
    return pl.pallas_call(
        _route_kernel,
        out_shape=(
            jax.ShapeDtypeStruct((TOP_K, t), I32),
            jax.ShapeDtypeStruct((TOP_K, t), F32),
            jax.ShapeDtypeStruct((TOP_K, t), I32),
            jax.ShapeDtypeStruct((t, LANES), F32),
            jax.ShapeDtypeStruct((n_exp, LANES), F32),
        ),
        grid=(t // tr,),
        in_specs=[
            pl.BlockSpec((n_exp, tr), lambda i: (0, i)),
            pl.BlockSpec((n_exp, 1), lambda i: (0, 0)),
            pl.BlockSpec((tr, tr), lambda i: (0, 0)),
        ],
        out_specs=(
            pl.BlockSpec((TOP_K, tr), lambda i: (0, i)),
            pl.BlockSpec((TOP_K, tr), lambda i: (0, i)),
            pl.BlockSpec((TOP_K, tr), lambda i: (0, i)),
            pl.BlockSpec((tr, LANES), lambda i: (i, 0)),
            pl.BlockSpec((n_exp, LANES), lambda i: (0, 0)),
        ),
        scratch_shapes=[pltpu.VMEM((n_exp, LANES), F32)],
        compiler_params=pltpu.CompilerParams(
            dimension_semantics=("arbitrary",), vmem_limit_bytes=VMEM_LIMIT),
        name="route",
    )(logt, e_bias_col, upper)


def _dest_kernel(idx_ref, rank_ref, start_ref, dest_ref):
    n_exp = start_ref.shape[0]
    tt = idx_ref.shape[1]
    eio = lax.broadcasted_iota(I32, (n_exp, tt), 0)
    start = start_ref[...]
    rows = []
    for k in range(TOP_K):
        hit = eio == idx_ref[k:k + 1, :]
        rows.append(jnp.sum(jnp.where(hit, start, 0), axis=0, keepdims=True) + rank_ref[k:k + 1, :])
    dest_ref[...] = jnp.concatenate(rows, axis=0)


def _dest(idx, rank, start_col):
    t = idx.shape[1]
    tt = min(DEST_TOKENS, t)
    n_exp = start_col.shape[0]
    return pl.pallas_call(
        _dest_kernel,
        out_shape=jax.ShapeDtypeStruct((TOP_K, t), I32),
        grid=(t // tt,),
        in_specs=[
            pl.BlockSpec((TOP_K, tt), lambda i: (0, i)),
            pl.BlockSpec((TOP_K, tt), lambda i: (0, i)),
            pl.BlockSpec((n_exp, 1), lambda i: (0, 0)),
        ],
        out_specs=pl.BlockSpec((TOP_K, tt), lambda i: (0, i)),
        compiler_params=pltpu.CompilerParams(vmem_limit_bytes=VMEM_LIMIT),
        name="dest",
    )(idx, rank, start_col)


def _slabs_copy(hbm, sem, n_slabs):
    return pltpu.make_async_copy(hbm.at[pl.ds(0, n_slabs)], hbm.at[pl.ds(0, n_slabs)], sem)


def _dispatch_kernel(pad_first, pad_len, n_used, dest_ref, h_ref, xs_hbm, zeros_ref, sem):
    td = dest_ref.shape[1]
    n_exp = pad_first.shape[0]
    n_blocks_max = xs_hbm.shape[0] // EXPERT_ROWS
    step = pl.program_id(0)

    def zero_fill(act):
        def zeros_to(first_row, n_rows):
            act(pltpu.make_async_copy(zeros_ref.at[pl.ds(0, n_rows)], xs_hbm.at[pl.ds(first_row, n_rows)],
                                      sem.at[1]))

        def expert_padding(e, carry):
            row = pad_first[e]
            for bit in reversed(range(EXPERT_ROWS.bit_length() - 1)):
                take = (pad_len[e] >> bit) & 1
                pl.when(take == 1)(functools.partial(zeros_to, row, 1 << bit))
                row = row + (take << bit)
            return carry

        def unused_block(b, carry):
            zeros_to(b * EXPERT_ROWS, EXPERT_ROWS)
            return carry

        lax.fori_loop(0, n_exp, expert_padding, 0)
        lax.fori_loop(n_used[0], n_blocks_max, unused_block, 0)

    @pl.when(step == 0)
    def _():
        zeros_ref[...] = jnp.zeros_like(zeros_ref)
        zero_fill(lambda cp: cp.start())

    def body(t, carry):
        for k in range(TOP_K):
            pltpu.make_async_copy(h_ref.at[t], xs_hbm.at[dest_ref[k, t]], sem.at[0]).start(priority=k % 2)
        return carry

    lax.fori_loop(0, td, body, 0, unroll=2)
    _slabs_copy(xs_hbm, sem.at[0], td * TOP_K).wait()

    @pl.when(step == pl.num_programs(0) - 1)
    def _():
        zero_fill(lambda cp: cp.wait())


def _dispatch(pad_first, pad_len, n_used, dest, h_slabs, n_rows):
    t = h_slabs.shape[0]
    td = min(DISPATCH_TOKENS, t)
    grid_spec = pltpu.PrefetchScalarGridSpec(
        num_scalar_prefetch=3,
        grid=(t // td,),
        in_specs=[
            pl.BlockSpec((TOP_K, td), lambda i, pf, pn, nu: (0, i), memory_space=pltpu.SMEM),
            pl.BlockSpec((td, SLAB, LANES), lambda i, pf, pn, nu: (i, 0, 0)),
        ],
        out_specs=pl.BlockSpec(memory_space=pl.ANY),
        scratch_shapes=[
            pltpu.VMEM((EXPERT_ROWS, SLAB, LANES), h_slabs.dtype),
            pltpu.SemaphoreType.DMA((2,)),
        ],
    )
    return pl.pallas_call(
        _dispatch_kernel,
        out_shape=jax.ShapeDtypeStruct((n_rows, SLAB, LANES), h_slabs.dtype),
        grid_spec=grid_spec,
        compiler_params=pltpu.CompilerParams(
            dimension_semantics=("arbitrary",), vmem_limit_bytes=VMEM_LIMIT),
        name="dispatch",
    )(pad_first, pad_len, n_used, dest, h_slabs)


def _experts_kernel(block_expert, block_new_expert, block_slot, block_next_expert, n_blocks,
                    xs_ref, wg_hbm, wu_hbm, wd_hbm, ys_ref,
                    wg_f, wu_f, wd_f, wg_b, wu_b, wd_b, slab_ref, sem):
    i = pl.program_id(0)

    def weight_copies(e, slot):
        return [pltpu.make_async_copy(hbm.at[e], buf.at[slot], sem.at[slot])
                for hbm, buf in ((wg_hbm, wg_f), (wu_hbm, wu_f), (wd_hbm, wd_f))]

    @pl.when(i < n_blocks[0])
    def _():
        @pl.when(block_new_expert[i] == 1)
        def _():
            slot = block_slot[i]

            @pl.when(i == 0)
            def _():
                for cp in weight_copies(block_expert[i], slot):
                    cp.start()

            for cp in weight_copies(block_expert[i], slot):
                cp.wait()
            wg_b[...] = wg_f[slot].astype(BF16)
            wu_b[...] = wu_f[slot].astype(BF16)
            wd_b[...] = wd_f[slot].astype(BF16)

            @pl.when(block_next_expert[i] >= 0)
            def _():
                for cp in weight_copies(block_next_expert[i], 1 - slot):
                    cp.start()

        xb = _from_slabs(xs_ref[...], slab_ref).astype(BF16)
        g = _dot(xb, wg_b[...])
        u = _dot(xb, wu_b[...])
        y = _dot((_silu(g) * u).astype(BF16), wd_b[...])
        ys_ref[...] = _to_slabs(y, slab_ref)


def _experts(meta, xs, w_gate, w_up, w_down):
    n_scalar = len(meta)
    _, d, de = w_gate.shape
    n_max = meta[0].shape[0]
    blk = lambda i, be, ne, sl, nx, nb: (jnp.minimum(i, nb[0] - 1), 0, 0)
    grid_spec = pltpu.PrefetchScalarGridSpec(
        num_scalar_prefetch=n_scalar,
        grid=(n_max,),
        in_specs=[
            pl.BlockSpec((EXPERT_ROWS, SLAB, LANES), blk),
            pl.BlockSpec(memory_space=pl.ANY),
            pl.BlockSpec(memory_space=pl.ANY),
            pl.BlockSpec(memory_space=pl.ANY),
        ],
        out_specs=pl.BlockSpec((EXPERT_ROWS, SLAB, LANES), blk),
        scratch_shapes=[
            pltpu.VMEM((2, d, de), F32),
            pltpu.VMEM((2, d, de), F32),
            pltpu.VMEM((2, de, d), F32),
            pltpu.VMEM((d, de), BF16),
            pltpu.VMEM((d, de), BF16),
            pltpu.VMEM((de, d), BF16),
            pltpu.VMEM((EXPERT_ROWS * SLAB, LANES), F32),
            pltpu.SemaphoreType.DMA((2,)),
        ],
    )
    return pl.pallas_call(
        _experts_kernel,
        out_shape=jax.ShapeDtypeStruct(xs.shape, xs.dtype),
        grid_spec=grid_spec,
        input_output_aliases={n_scalar: 0},
        compiler_params=pltpu.CompilerParams(
            dimension_semantics=("arbitrary",), vmem_limit_bytes=VMEM_LIMIT),
        name="experts",
    )(*meta, xs, w_gate, w_up, w_down)


def _expert_blocks(counts, n_assign):
    n_exp = counts.shape[0]
    n_max = (n_assign + n_exp * (EXPERT_ROWS - 1)) // EXPERT_ROWS + 1
    padded = (counts + EXPERT_ROWS - 1) // EXPERT_ROWS * EXPERT_ROWS
    pad_end = jnp.cumsum(padded)
    pad_start = pad_end - padded
    n_used = pad_end[-1] // EXPERT_ROWS
    first_row = jnp.minimum(jnp.arange(n_max, dtype=I32), n_used - 1) * EXPERT_ROWS
    e_of = jnp.sum((first_row[:, None] >= pad_end[None, :]).astype(I32), axis=1)
    e_of = jnp.minimum(e_of, n_exp - 1).astype(I32)
    prev_e = jnp.concatenate([jnp.full((1,), -1, I32), e_of[:-1]])
    new_expert = (e_of != prev_e).astype(I32)
    slot = ((jnp.cumsum(new_expert) - 1) & 1).astype(I32)
    ids = jnp.arange(n_exp, dtype=I32)
    later_used = (ids[None, :] > ids[:, None]) & (padded[None, :] > 0)
    next_used = jnp.min(jnp.where(later_used, ids[None, :], n_exp), axis=1)
    next_used = jnp.where(next_used < n_exp, next_used, -1)
    next_expert = jnp.sum(jnp.where(e_of[:, None] == ids[None, :], next_used[None, :], 0), axis=1).astype(I32)
    n_used = n_used.reshape(1).astype(I32)
    meta = (e_of, new_expert, slot, next_expert, n_used)
    return (meta, n_used, pad_start.astype(I32), (pad_start + counts).astype(I32),
            (padded - counts).astype(I32), n_max)


def _combine_kernel(dest_cur, dest_nxt, x1_ref, mod_ref, wt_ref, wsg_ref, wsu_ref, wsd_ref,
                    ln2g_ref, ln2b_ref, ys_hbm, out_ref, buf, wb_ref, acc_ref, sem, *, alpha):
    tc = x1_ref.shape[0]
    i = pl.program_id(0)
    n = pl.num_programs(0)
    slot = i % 2

    def request(dest_ref, s, t):
        for k in range(TOP_K):
            pltpu.make_async_copy(ys_hbm.at[dest_ref[k, t]], buf.at[s, t * TOP_K + k],
                                  sem.at[s]).start(priority=k % 2)

    @pl.when(i == 0)
    def _():
        def body(t, carry):
            request(dest_cur, 0, t)
            return carry
        lax.fori_loop(0, tc, body, 0)

    pltpu.make_async_copy(ys_hbm.at[pl.ds(0, tc * TOP_K)], buf.at[slot], sem.at[slot]).wait()

    wt = wt_ref[...]
    for k in range(TOP_K):
        wb_ref[k] = jnp.broadcast_to(wt[:, k:k + 1], (tc, LANES))

    def token(t, carry, prefetch):
        if prefetch:
            request(dest_nxt, 1 - slot, t)
        terms = [jnp.broadcast_to(wb_ref[k, pl.ds(t, 1), :], (SLAB, LANES))
                 * buf[slot, t * TOP_K + k].astype(F32)
                 for k in range(TOP_K)]
        while len(terms) > 1:
            terms = [a + b for a, b in zip(terms[0::2], terms[1::2])]
        acc_ref[pl.ds(pl.multiple_of(t * SLAB, SLAB), SLAB), :] = terms[0]
        return carry

    @pl.when(i + 1 < n)
    def _():
        lax.fori_loop(0, tc, functools.partial(token, prefetch=True), 0, unroll=8)

    @pl.when(i + 1 == n)
    def _():
        lax.fori_loop(0, tc, functools.partial(token, prefetch=False), 0, unroll=8)

    mod = mod_ref[0]
    sh_f, sc_f, gt_f = mod[3:4], mod[4:5], mod[5:6]
    x1 = x1_ref[...]
    hb = (_ln(x1) * (1.0 + sc_f) + sh_f).astype(BF16)
    mid = _silu(_dot(hb, wsg_ref[...])) * _dot(hb, wsu_ref[...])
    ffn = _dot(mid.astype(BF16), wsd_ref[...])
    ffn = _slabs_to_rows(acc_ref, tc) + ffn
    out_ref[...] = _ln(alpha * x1 + (1.0 + gt_f) * ffn) * ln2g_ref[...] + ln2b_ref[...]


def _combine(dest, x1_flat, mod, wt, ws_gate, ws_up, ws_down, ln2_g, ln2_b, ys, seq, alpha):
    t, d = x1_flat.shape
    tc = min(COMBINE_TOKENS, seq)
    n = t // tc
    per_seq = seq // tc
    const = lambda i: (0, 0)
    return pl.pallas_call(
        functools.partial(_combine_kernel, alpha=alpha),
        out_shape=jax.ShapeDtypeStruct((t, d), F32),
        grid=(n,),
        in_specs=[
            pl.BlockSpec((TOP_K, tc), lambda i: (0, i), memory_space=pltpu.SMEM),
            pl.BlockSpec((TOP_K, tc), lambda i: (0, jnp.minimum(i + 1, n - 1)), memory_space=pltpu.SMEM),
            pl.BlockSpec((tc, d), lambda i: (i, 0)),
            pl.BlockSpec((1, SUBLANES, d), lambda i: (i // per_seq, 0, 0)),
            pl.BlockSpec((tc, LANES), lambda i: (i, 0)),
            pl.BlockSpec(ws_gate.shape, const),
            pl.BlockSpec(ws_up.shape, const),
            pl.BlockSpec(ws_down.shape, const),
            pl.BlockSpec(ln2_g.shape, const),
            pl.BlockSpec(ln2_b.shape, const),
            pl.BlockSpec(memory_space=pl.ANY),
        ],
        out_specs=pl.BlockSpec((tc, d), lambda i: (i, 0)),
        scratch_shapes=[
            pltpu.VMEM((2, tc * TOP_K, SLAB, LANES), BF16),
            pltpu.VMEM((TOP_K, tc, LANES), F32),
            pltpu.VMEM((tc * SLAB, LANES), F32),
            pltpu.SemaphoreType.DMA((2,)),
        ],
        compiler_params=pltpu.CompilerParams(
            dimension_semantics=("arbitrary",), vmem_limit_bytes=VMEM_LIMIT),
        name="combine",
    )(dest, dest, x1_flat, mod, wt, ws_gate, ws_up, ws_down, ln2_g, ln2_b, ys)


def _layer(x, c_pad, lw, alpha):
    bsz, seq, d = x.shape
    t = bsz * seq
    n_exp = lw["w_router"].shape[1]
    h_count = lw["w_if"].shape[1] // 2

    mod = _ada(c_pad, lw["w_ada"], lw["b_ada"].reshape(1, -1))[:bsz].reshape(bsz, 6, d)
    mod = jnp.pad(mod, ((0, 0), (0, SUBLANES - 6), (0, 0)))

    row2 = lambda a: a.reshape(1, -1)
    w_if = lw["w_if"]
    w_if_pad = jnp.zeros((w_if.shape[0], 2 * LANES), F32)
    w_if_pad = w_if_pad.at[:, :h_count].set(w_if[:, :h_count]).at[:, LANES:LANES + h_count].set(w_if[:, h_count:])
    b_if_pad = jnp.zeros((1, 2 * LANES), F32)
    b_if_pad = b_if_pad.at[0, :h_count].set(lw["b_if"][:h_count]).at[0, LANES:LANES + h_count].set(lw["b_if"][h_count:])
    w_rt = lw["w_router"].astype(F32).T
    wr_hi = w_rt.astype(BF16)
    wr_lo = (w_rt - wr_hi.astype(F32)).astype(BF16)
    p = {
        "w_in": lw["w_in"].astype(BF16), "conv_w": lw["conv_w"], "conv_b": row2(lw["conv_b"]),
        "w_q": lw["w_q"].astype(BF16), "w_k": lw["w_k"].astype(BF16), "w_v": lw["w_v"].astype(BF16),
        "w_if": w_if_pad.astype(BF16), "b_if": b_if_pad,
        "mh_g": row2(lw["mh_g"]), "skip": row2(lw["skip"]), "sg_g": row2(lw["sg_g"]), "sg_b": row2(lw["sg_b"]),
        "w_sp": lw["w_sp"], "b_sp_t": lw["b_sp"].T, "w_out": lw["w_out"].astype(BF16),
        "ln1_g": row2(lw["ln1_g"]), "ln1_b": row2(lw["ln1_b"]), "wr_hi": wr_hi, "wr_lo": wr_lo,
    }
    x1, h2, logt = _mixer(x, mod, p, alpha)

    tr = min(ROUTE_TOKENS, t)
    upper = (jnp.arange(tr)[:, None] < jnp.arange(tr)[None, :]).astype(BF16)
    idx, _, rank, wt, cnt = _route(logt, lw["e_bias"].astype(F32).reshape(n_exp, 1), upper)
    counts = cnt[:, 0].astype(I32)
    meta, n_used, pad_start, pad_first, pad_len, n_blocks_max = _expert_blocks(counts, t * TOP_K)
    dest = _dest(idx, rank, pad_start.reshape(n_exp, 1))

    xs = _dispatch(pad_first, pad_len, n_used, dest, h2, n_blocks_max * EXPERT_ROWS)
    ys = _experts(meta, xs, lw["w_gate"], lw["w_up"], lw["w_down"])
    out = _combine(dest, x1.reshape(t, d), mod, wt, lw["ws_gate"].astype(BF16), lw["ws_up"].astype(BF16),
                   lw["ws_down"].astype(BF16), row2(lw["ln2_g"]), row2(lw["ln2_b"]), ys, seq, alpha)
    return out.reshape(bsz, seq, d)


def kernel(x, c, w_ada, b_ada, w_in, conv_w, conv_b, w_q, w_k, w_v, w_if, b_if, mh_g, skip, sg_g, sg_b, w_sp, b_sp, w_out, ln1_g, ln1_b, w_router, e_bias, w_gate, w_up, w_down, ws_gate, ws_up, ws_down, ln2_g, ln2_b):
    stacked = dict(w_ada=w_ada, b_ada=b_ada, w_in=w_in, conv_w=conv_w, conv_b=conv_b, w_q=w_q, w_k=w_k,
                   w_v=w_v, w_if=w_if, b_if=b_if, mh_g=mh_g, skip=skip, sg_g=sg_g, sg_b=sg_b, w_sp=w_sp,
                   b_sp=b_sp, w_out=w_out, ln1_g=ln1_g, ln1_b=ln1_b, w_router=w_router, e_bias=e_bias,
                   w_gate=w_gate, w_up=w_up, w_down=w_down, ws_gate=ws_gate, ws_up=ws_up, ws_down=ws_down,
                   ln2_g=ln2_g, ln2_b=ln2_b)
    depth = w_ada.shape[0]
    alpha = float((2 * depth) ** 0.25)
    bsz = x.shape[0]
    c_pad = jnp.pad(c, ((0, -bsz % SUBLANES), (0, 0)))
    for l in range(depth):
        x = _layer(x, c_pad, {k: v[l] for k, v in stacked.items()}, alpha)
    return x
```

```python
import functools
import math

import jax
import jax.numpy as jnp
from jax import lax
from jax.experimental import pallas as pl
from jax.experimental.pallas import tpu as pltpu

F32 = jnp.float32
BF16 = jnp.bfloat16
I32 = jnp.int32

LN_EPS = 1e-5
M_HEADS = 4
G_GROUPS = 4
CHUNK = 128
N_GROUPS = 8
TOPK_GROUPS = 4
TOP_K = 8
ROUTE_SCALE = 2.5
LANES = 128
SUBLANES = 8

MIXER_TOKENS = 512
MIXER_CHUNK_GROUP = 4
ROUTE_TOKENS = 1024
DEST_TOKENS = 2048
DISPATCH_TOKENS = 1024
EXPERT_ROWS = 256
COMBINE_TOKENS = 256
VMEM_LIMIT = 56 * 1024 * 1024

NEG_INF = float("-inf")


def _ln(x):
    mu = jnp.mean(x, axis=-1, keepdims=True)
    xc = x - mu
    var = jnp.mean(xc * xc, axis=-1, keepdims=True)
    return xc * lax.rsqrt(var + LN_EPS)


def _dot(a, b):
    return jnp.dot(a, b, preferred_element_type=F32)


def _dot_nt(a, b):
    return lax.dot_general(a, b, (((1,), (1,)), ((), ())), preferred_element_type=F32)


def _dot_exact(a, b):
    return jnp.dot(a, b, preferred_element_type=F32, precision=lax.Precision.HIGHEST)


def _silu(x):
    return x * jax.nn.sigmoid(x)


def _gelu(x):
    return 0.5 * x * (1.0 + lax.erf(x * math.sqrt(0.5)))


def _log_sigmoid(x):
    return jnp.minimum(x, 0.0) - jnp.log1p(jnp.exp(-jnp.abs(x)))


SLAB = SUBLANES


def _to_slabs(x, scratch):
    rows, d = x.shape
    assert d == SLAB * LANES
    for s in range(SLAB):
        scratch[pl.ds(s, rows, stride=SLAB), :] = x[:, s * LANES:(s + 1) * LANES]
    return scratch[0:rows * SLAB, :].reshape(rows, SLAB, LANES).astype(BF16)


def _slabs_to_rows(scratch, rows):
    return jnp.concatenate([scratch[pl.ds(s, rows, stride=SLAB), :] for s in range(SLAB)], axis=1)


def _from_slabs(x3, scratch):
    rows = x3.shape[0]
    scratch[0:rows * SLAB, :] = x3.astype(F32).reshape(rows * SLAB, LANES)
    return _slabs_to_rows(scratch, rows)


def _ada_kernel(c_ref, w_ref, b_ref, o_ref):
    o_ref[...] = _dot_exact(_silu(c_ref[...]), w_ref[...]) + b_ref[...]


def _ada(c_pad, w_ada, b_ada):
    rows, d = c_pad.shape
    n = w_ada.shape[1]
    return pl.pallas_call(
        _ada_kernel,
        out_shape=jax.ShapeDtypeStruct((rows, n), F32),
        grid=(n // d,),
        in_specs=[
            pl.BlockSpec((rows, d), lambda j: (0, 0)),
            pl.BlockSpec((d, d), lambda j: (0, j)),
            pl.BlockSpec((1, d), lambda j: (0, j)),
        ],
        out_specs=pl.BlockSpec((rows, d), lambda j: (0, j)),
        compiler_params=pltpu.CompilerParams(vmem_limit_bytes=VMEM_LIMIT),
        name="ada",
    )(c_pad, w_ada, b_ada)


def _mixer_kernel(x_ref, mod_ref, w_in_ref, conv_w_ref, conv_b_ref, wq_ref, wk_ref, wv_ref,
                  wif_ref, bif_ref, mhg_ref, skip_ref, sgg_ref, sgb_ref, wsp_ref, bspt_ref,
                  w_out_ref, ln1g_ref, ln1b_ref, wr_hi_ref, wr_lo_ref,
                  x1_ref, h2_ref, logt_ref,
                  state_ref, m_ref, xm_ref, cat_ref, slab_ref, *, alpha):
    tm = x_ref.shape[1]
    mw = conv_w_ref.shape[1]
    dh = mw // M_HEADS
    gw = sgg_ref.shape[1]
    gd = gw // G_GROUPS
    conv_k = conv_w_ref.shape[0]
    nch = tm // CHUNK
    L = CHUNK

    @pl.when(pl.program_id(1) == 0)
    def _():
        state_ref[...] = jnp.zeros_like(state_ref)
        m_ref[...] = jnp.zeros_like(m_ref)
        xm_ref[0:SUBLANES, :] = jnp.zeros((SUBLANES, mw), F32)

    x = x_ref[0]
    mod = mod_ref[0]
    sh_a, sc_a, gt_a = mod[0:1], mod[1:2], mod[2:3]
    sh_f, sc_f = mod[3:4], mod[4:5]

    h = _ln(x) * (1.0 + sc_a) + sh_a
    proj = _dot(h.astype(BF16), w_in_ref[...])
    xm = proj[:, :mw]
    z = proj[:, mw:2 * mw]
    u = proj[:, 2 * mw:2 * mw + gw]
    v = proj[:, 2 * mw + gw:]

    xm_ref[SUBLANES:SUBLANES + tm, :] = xm
    conv = jnp.broadcast_to(conv_b_ref[...], (tm, mw))
    for j in range(conv_k):
        off = SUBLANES - (conv_k - 1) + j
        conv = conv + conv_w_ref[j:j + 1, :] * xm_ref[off:off + tm, :]
    xm_ref[0:SUBLANES, :] = xm_ref[tm:tm + SUBLANES, :]
    xc = _silu(conv)

    scale = dh ** -0.5
    qs, ks, vs = [], [], []
    for hd in range(M_HEADS):
        sl = slice(hd * dh, (hd + 1) * dh)
        xch = xc[:, sl].astype(BF16)
        qs.append(_dot(xch, wq_ref[hd]))
        ks.append(_dot(xch, wk_ref[hd]) * scale)
        vs.append(_dot(xm[:, sl].astype(BF16), wv_ref[hd]))
    qkv = jnp.concatenate(qs + ks + vs, axis=1).astype(BF16)
    gate = _dot(qkv, wif_ref[...]) + bif_ref[...]
    gi = gate[:, :LANES]
    lf = _log_sigmoid(gate[:, LANES:])

    row = lax.broadcasted_iota(I32, (L, L), 0)
    col = lax.broadcasted_iota(I32, (L, L), 1)
    causal = col <= row
    tri = jnp.where(causal, 1.0, 0.0).astype(F32)
    ones_l = jnp.ones((L, dh), F32)
    wsg = [jnp.where(causal, wsp_ref[gg], 0.0).astype(BF16) for gg in range(G_GROUPS)]
    state = [state_ref[hd] for hd in range(M_HEADS)]
    m_run = [m_ref[hd:hd + 1, 0:1] for hd in range(M_HEADS)]

    heads = range(M_HEADS)
    hsl = [slice(hd * dh, (hd + 1) * dh) for hd in heads]
    groups = range(G_GROUPS)
    gsl = [slice(gg * gd, (gg + 1) * gd) for gg in groups]

    for c0 in range(0, nch, MIXER_CHUNK_GROUP):
        chunks = range(c0, min(c0 + MIXER_CHUNK_GROUP, nch))
        rsl = {c: slice(c * L, (c + 1) * L) for c in chunks}
        pairs = [(c, hd) for c in chunks for hd in heads]
        bmat = {c: _dot_exact(tri, lf[rsl[c]]) for c in chunks}
        rmat = {c: gi[rsl[c]] - bmat[c] for c in chunks}
        rmat_t = {c: rmat[c].T for c in chunks}
        bcol = {(c, hd): bmat[c][:, hd:hd + 1] for c, hd in pairs}
        rrow = {(c, hd): rmat_t[c][hd:hd + 1, :] for c, hd in pairs}
        g = {(c, hd): bmat[c][L - 1:L, hd:hd + 1] for c, hd in pairs}
        a = {(c, hd): g[c, hd] + rmat[c][:, hd:hd + 1] for c, hd in pairs}
        a_max = {pr: jnp.max(a[pr], axis=0, keepdims=True) for pr in pairs}

        m_in, m_out = {}, {}
        for c in chunks:
            for hd in heads:
                m_in[c, hd] = m_run[hd]
                m_out[c, hd] = jnp.maximum(g[c, hd] + m_run[hd], a_max[c, hd])
            m_run = [m_out[c, hd] for hd in heads]

        k = {(c, hd): ks[hd][rsl[c]] for c, hd in pairs}
        qb = {(c, hd): qs[hd][rsl[c]].astype(BF16) for c, hd in pairs}
        kb = {pr: k[pr].astype(BF16) for pr in pairs}
        vext = {(c, hd): jnp.concatenate([vs[hd][rsl[c]], ones_l], axis=1).astype(BF16)
                for c, hd in pairs}

        log_d = {pr: jnp.where(causal, bcol[pr] + rrow[pr], NEG_INF) for pr in pairs}
        log_inter = {pr: bcol[pr] + m_in[pr] for pr in pairs}
        m_t = {pr: jnp.maximum(log_inter[pr], jnp.max(log_d[pr], axis=1, keepdims=True)) for pr in pairs}
        qk = {pr: _dot_nt(qb[pr], kb[pr]) for pr in pairs}
        p = {pr: (jnp.exp(log_d[pr] - m_t[pr]) * qk[pr]).astype(BF16) for pr in pairs}
        w_inter = {pr: jnp.exp(log_inter[pr] - m_t[pr]) for pr in pairs}
        intra = {pr: _dot(p[pr], vext[pr]) for pr in pairs}
        decay = {pr: jnp.exp(g[pr] + m_in[pr] - m_out[pr]) for pr in pairs}
        kw_t = {pr: (k[pr] * jnp.exp(a[pr] - m_out[pr])).T.astype(BF16) for pr in pairs}
        update = {pr: _dot(kw_t[pr], vext[pr]) for pr in pairs}

        st_in = {}
        for c in chunks:
            for hd in heads:
                st_in[c, hd] = state[hd]
            state = [decay[c, hd] * state[hd] + update[c, hd] for hd in heads]

        inter = {pr: _dot(qb[pr], st_in[pr].astype(BF16)) for pr in pairs}
        out_ext = {pr: w_inter[pr] * inter[pr] + intra[pr] for pr in pairs}
        hh = {pr: out_ext[pr][:, :dh] / jnp.maximum(jnp.abs(out_ext[pr][:, dh:]), jnp.exp(-m_t[pr]))
              for pr in pairs}
        hc = {(c, hd): _ln(hh[c, hd]) * mhg_ref[:, hsl[hd]] for c, hd in pairs}
        for c, hd in pairs:
            hm = jax.nn.sigmoid(z[rsl[c], hsl[hd]]) * (hc[c, hd] + skip_ref[:, hsl[hd]] * xc[rsl[c], hsl[hd]])
            cat_ref[rsl[c], hsl[hd]] = hm.astype(BF16)

        for c in chunks:
            ug = _gelu(u[rsl[c]])
            vg = _gelu(v[rsl[c]])
            vn = [(_ln(vg[:, gsl[gg]]) * sgg_ref[:, gsl[gg]] + sgb_ref[:, gsl[gg]]).astype(BF16) for gg in groups]
            sp = [_dot(wsg[gg], vn[gg]) + bspt_ref[:, gg:gg + 1] for gg in groups]
            for gg in groups:
                cat_ref[rsl[c], mw + gg * gd:mw + (gg + 1) * gd] = (ug[:, gsl[gg]] * sp[gg]).astype(BF16)

    for hd in range(M_HEADS):
        state_ref[hd] = state[hd]
        m_ref[hd:hd + 1, :] = jnp.broadcast_to(m_run[hd], (1, LANES))

    mix = _dot(cat_ref[...], w_out_ref[...])
    x1 = _ln(alpha * x + (1.0 + gt_a) * mix) * ln1g_ref[...] + ln1b_ref[...]
    x1_ref[0] = x1
    h2 = _ln(x1) * (1.0 + sc_f) + sh_f
    h2_ref[...] = _to_slabs(h2, slab_ref)
    h_hi = h2.astype(BF16)
    h_lo = (h2 - h_hi.astype(F32)).astype(BF16)
    wr_hi = wr_hi_ref[...]
    logt_ref[...] = _dot_nt(wr_hi, h_hi) + _dot_nt(wr_hi, h_lo) + _dot_nt(wr_lo_ref[...], h_hi)


def _mixer(x, mod, p, alpha):
    bsz, seq, d = x.shape
    tm = min(MIXER_TOKENS, seq)
    nt = seq // tm
    mw = p["conv_w"].shape[1]
    gw = p["sg_g"].shape[1]
    dh = mw // M_HEADS
    n_exp = p["wr_hi"].shape[0]
    const2 = lambda b, i: (0, 0)
    const3 = lambda b, i: (0, 0, 0)
    full = lambda a: pl.BlockSpec(a.shape, const2 if a.ndim == 2 else const3)
    names = ["w_in", "conv_w", "conv_b", "w_q", "w_k", "w_v", "w_if", "b_if", "mh_g", "skip", "sg_g",
             "sg_b", "w_sp", "b_sp_t", "w_out", "ln1_g", "ln1_b", "wr_hi", "wr_lo"]
    weights = [p[n] for n in names]
    return pl.pallas_call(
        functools.partial(_mixer_kernel, alpha=alpha),
        out_shape=(
            jax.ShapeDtypeStruct((bsz, seq, d), F32),
            jax.ShapeDtypeStruct((bsz * seq, SLAB, LANES), BF16),
            jax.ShapeDtypeStruct((n_exp, bsz * seq), F32),
        ),
        grid=(bsz, nt),
        in_specs=[
            pl.BlockSpec((1, tm, d), lambda b, i: (b, i, 0)),
            pl.BlockSpec((1, SUBLANES, d), lambda b, i: (b, 0, 0)),
        ] + [full(w) for w in weights],
        out_specs=(
            pl.BlockSpec((1, tm, d), lambda b, i: (b, i, 0)),
            pl.BlockSpec((tm, SLAB, LANES), lambda b, i: (b * nt + i, 0, 0)),
            pl.BlockSpec((n_exp, tm), lambda b, i: (0, b * nt + i)),
        ),
        scratch_shapes=[
            pltpu.VMEM((M_HEADS, dh, 2 * dh), F32),
            pltpu.VMEM((SUBLANES, LANES), F32),
            pltpu.VMEM((tm + SUBLANES, mw), F32),
            pltpu.VMEM((tm, mw + gw), BF16),
            pltpu.VMEM((tm * SLAB, LANES), F32),
        ],
        compiler_params=pltpu.CompilerParams(
            dimension_semantics=("arbitrary", "arbitrary"), vmem_limit_bytes=VMEM_LIMIT),
        name="mixer",
    )(x, mod, *weights)


def _route_kernel(logt_ref, bias_ref, upper_ref, idx_ref, w_ref, rank_ref, wt_ref, cnt_ref, carry_ref):
    n_exp, tr = logt_ref.shape
    epg = n_exp // N_GROUPS

    @pl.when(pl.program_id(0) == 0)
    def _():
        carry_ref[...] = jnp.zeros_like(carry_ref)

    scores = jax.nn.sigmoid(logt_ref[...])
    sel = scores + bias_ref[...]

    sel3 = sel.reshape(N_GROUPS, epg, tr)
    io3 = lax.broadcasted_iota(I32, (N_GROUPS, epg, tr), 1)
    m1 = jnp.max(sel3, axis=1, keepdims=True)
    first = jnp.min(jnp.where(sel3 == m1, io3, epg), axis=1, keepdims=True)
    m2 = jnp.max(jnp.where(io3 == first, NEG_INF, sel3), axis=1, keepdims=True)
    gs = (m1 + m2).reshape(N_GROUPS, tr)

    gio = lax.broadcasted_iota(I32, (N_GROUPS, tr), 0)
    gmask = jnp.zeros((N_GROUPS, tr), F32)
    for _ in range(TOPK_GROUPS):
        m = jnp.max(gs, axis=0, keepdims=True)
        gi = jnp.min(jnp.where(gs == m, gio, N_GROUPS), axis=0, keepdims=True)
        hit = gio == gi
        gmask = jnp.where(hit, 1.0, gmask)
        gs = jnp.where(hit, NEG_INF, gs)
    emask = jnp.broadcast_to(gmask.reshape(N_GROUPS, 1, tr), (N_GROUPS, epg, tr)).reshape(n_exp, tr)
    selm = jnp.where(emask > 0.0, sel, NEG_INF)

    eio = lax.broadcasted_iota(I32, (n_exp, tr), 0)
    chosen = jnp.zeros((n_exp, tr), F32)
    idx_rows, w_rows = [], []
    for _ in range(TOP_K):
        m = jnp.max(selm, axis=0, keepdims=True)
        ei = jnp.min(jnp.where(selm == m, eio, n_exp), axis=0, keepdims=True)
        hit = eio == ei
        w_rows.append(jnp.sum(jnp.where(hit, scores, 0.0), axis=0, keepdims=True))
        idx_rows.append(ei)
        selm = jnp.where(hit, NEG_INF, selm)
        chosen = jnp.where(hit, 1.0, chosen)

    chosen_b = chosen.astype(BF16)
    carry = carry_ref[...]
    ranks = carry[:, 0:1] + _dot(chosen_b, upper_ref[...])
    carry_new = carry + _dot(chosen_b, jnp.ones((tr, LANES), BF16))
    carry_ref[...] = carry_new
    cnt_ref[...] = carry_new
    rank_rows = [jnp.sum(jnp.where(eio == ei, ranks, 0.0), axis=0, keepdims=True) for ei in idx_rows]

    wsum = w_rows[0]
    for wk in w_rows[1:]:
        wsum = wsum + wk
    w8 = jnp.concatenate([wk / wsum * ROUTE_SCALE for wk in w_rows], axis=0)
    idx_ref[...] = jnp.concatenate(idx_rows, axis=0)
    rank_ref[...] = jnp.concatenate(rank_rows, axis=0).astype(I32)
    w_ref[...] = w8
    wpad = jnp.concatenate([w8, jnp.zeros((LANES - TOP_K, tr), F32)], axis=0)
    wt_ref[...] = wpad.T


def _route(logt, e_bias_col, upper):
    n_exp, t = logt.shape
    tr = upper.shape[0]
    return pl.pallas_call(
        _route_kernel,
        out_shape=(
            jax.ShapeDtypeStruct((TOP_K, t), I32),
            jax.ShapeDtypeStruct((TOP_K, t), F32),
            jax.ShapeDtypeStruct((TOP_K, t), I32),
            jax.ShapeDtypeStruct((t, LANES), F32),
            jax.ShapeDtypeStruct((n_exp, LANES), F32),
        ),
        grid=(t // tr,),
        in_specs=[
            pl.BlockSpec((n_exp, tr), lambda i: (0, i)),
            pl.BlockSpec((n_exp, 1), lambda i: (0, 0)),
            pl.BlockSpec((tr, tr), lambda i: (0, 0)),
        ],
        out_specs=(
            pl.BlockSpec((TOP_K, tr), lambda i: (0, i)),
            pl.BlockSpec((TOP_K, tr), lambda i: (0, i)),
            pl.BlockSpec((TOP_K, tr), lambda i: (0, i)),
            pl.BlockSpec((tr, LANES), lambda i: (i, 0)),
            pl.BlockSpec((n_exp, LANES), lambda i: (0, 0)),
        ),
        scratch_shapes=[pltpu.VMEM((n_exp, LANES), F32)],
        compiler_params=pltpu.CompilerParams(
            dimension_semantics=("arbitrary",), vmem_limit_bytes=VMEM_LIMIT),
        name="route",
    )(logt, e_bias_col, upper)


def _dest_kernel(idx_ref, rank_ref, start_ref, dest_ref):
    n_exp = start_ref.shape[0]
    tt = idx_ref.shape[1]
    eio = lax.broadcasted_iota(I32, (n_exp, tt), 0)
    start = start_ref[...]
    rows = []
    for k in range(TOP_K):
        hit = eio == idx_ref[k:k + 1, :]
        rows.append(jnp.sum(jnp.where(hit, start, 0), axis=0, keepdims=True) + rank_ref[k:k + 1, :])
    dest_ref[...] = jnp.concatenate(rows, axis=0)


def _dest(idx, rank, start_col):
    t = idx.shape[1]
    tt = min(DEST_TOKENS, t)
    n_exp = start_col.shape[0]
    return pl.pallas_call(
        _dest_kernel,
        out_shape=jax.ShapeDtypeStruct((TOP_K, t), I32),
        grid=(t // tt,),
        in_specs=[
            pl.BlockSpec((TOP_K, tt), lambda i: (0, i)),
            pl.BlockSpec((TOP_K, tt), lambda i: (0, i)),
            pl.BlockSpec((n_exp, 1), lambda i: (0, 0)),
        ],
        out_specs=pl.BlockSpec((TOP_K, tt), lambda i: (0, i)),
        compiler_params=pltpu.CompilerParams(vmem_limit_bytes=VMEM_LIMIT),
        name="dest",
    )(idx, rank, start_col)


def _slabs_copy(hbm, sem, n_slabs):
    return pltpu.make_async_copy(hbm.at[pl.ds(0, n_slabs)], hbm.at[pl.ds(0, n_slabs)], sem)


def _dispatch_kernel(pad_first, pad_len, n_used, dest_ref, h_ref, xs_hbm, zeros_ref, sem):
    td = dest_ref.shape[1]
    n_exp = pad_first.shape[0]
    n_blocks_max = xs_hbm.shape[0] // EXPERT_ROWS
    step = pl.program_id(0)

    def zero_fill(act):
        def zeros_to(first_row, n_rows):
            act(pltpu.make_async_copy(zeros_ref.at[pl.ds(0, n_rows)], xs_hbm.at[pl.ds(first_row, n_rows)],
                                      sem.at[1]))

        def expert_padding(e, carry):
            row = pad_first[e]
            for bit in reversed(range(EXPERT_ROWS.bit_length() - 1)):
                take = (pad_len[e] >> bit) & 1
                pl.when(take == 1)(functools.partial(zeros_to, row, 1 << bit))
                row = row + (take << bit)
            return carry

        def unused_block(b, carry):
            zeros_to(b * EXPERT_ROWS, EXPERT_ROWS)
            return carry

        lax.fori_loop(0, n_exp, expert_padding, 0)
        lax.fori_loop(n_used[0], n_blocks_max, unused_block, 0)

    @pl.when(step == 0)
    def _():
        zeros_ref[...] = jnp.zeros_like(zeros_ref)
        zero_fill(lambda cp: cp.start())

    def body(t, carry):
        for k in range(TOP_K):
            pltpu.make_async_copy(h_ref.at[t], xs_hbm.at[dest_ref[k, t]], sem.at[0]).start(priority=k % 2)
        return carry

    lax.fori_loop(0, td, body, 0, unroll=2)
    _slabs_copy(xs_hbm, sem.at[0], td * TOP_K).wait()

    @pl.when(step == pl.num_programs(0) - 1)
    def _():
        zero_fill(lambda cp: cp.wait())


def _dispatch(pad_first, pad_len, n_used, dest, h_slabs, n_rows):
    t = h_slabs.shape[0]
    td = min(DISPATCH_TOKENS, t)
    grid_spec = pltpu.PrefetchScalarGridSpec(
        num_scalar_prefetch=3,
        grid=(t // td,),
        in_specs=[
            pl.BlockSpec((TOP_K, td), lambda i, pf, pn, nu: (0, i), memory_space=pltpu.SMEM),
            pl.BlockSpec((td, SLAB, LANES), lambda i, pf, pn, nu: (i, 0, 0)),
        ],
        out_specs=pl.BlockSpec(memory_space=pl.ANY),
        scratch_shapes=[
            pltpu.VMEM((EXPERT_ROWS, SLAB, LANES), h_slabs.dtype),
            pltpu.SemaphoreType.DMA((2,)),
        ],
    )
    return pl.pallas_call(
        _dispatch_kernel,
        out_shape=jax.ShapeDtypeStruct((n_rows, SLAB, LANES), h_slabs.dtype),
        grid_spec=grid_spec,
        compiler_params=pltpu.CompilerParams(
            dimension_semantics=("arbitrary",), vmem_limit_bytes=VMEM_LIMIT),
        name="dispatch",
    )(pad_first, pad_len, n_used, dest, h_slabs)


def _experts_kernel(block_expert, block_new_expert, block_slot, block_next_expert, n_blocks,
                    xs_ref, wg_hbm, wu_hbm, wd_hbm, ys_ref,
                    wg_f, wu_f, wd_f, wg_b, wu_b, wd_b, slab_ref, sem):
    i = pl.program_id(0)

    def weight_copies(e, slot):
        return [pltpu.make_async_copy(hbm.at[e], buf.at[slot], sem.at[slot])
                for hbm, buf in ((wg_hbm, wg_f), (wu_hbm, wu_f), (wd_hbm, wd_f))]

    @pl.when(i < n_blocks[0])
    def _():
        @pl.when(block_new_expert[i] == 1)
        def _():
            slot = block_slot[i]

            @pl.when(i == 0)
            def _():
                for cp in weight_copies(block_expert[i], slot):
                    cp.start()

            for cp in weight_copies(block_expert[i], slot):
                cp.wait()
            wg_b[...] = wg_f[slot].astype(BF16)
            wu_b[...] = wu_f[slot].astype(BF16)
            wd_b[...] = wd_f[slot].astype(BF16)

            @pl.when(block_next_expert[i] >= 0)
            def _():
                for cp in weight_copies(block_next_expert[i], 1 - slot):
                    cp.start()

        xb = _from_slabs(xs_ref[...], slab_ref).astype(BF16)
        g = _dot(xb, wg_b[...])
        u = _dot(xb, wu_b[...])
        y = _dot((_silu(g) * u).astype(BF16), wd_b[...])
        ys_ref[...] = _to_slabs(y, slab_ref)


def _experts(meta, xs, w_gate, w_up, w_down):
    n_scalar = len(meta)
    _, d, de = w_gate.shape
    n_max = meta[0].shape[0]
    blk = lambda i, be, ne, sl, nx, nb: (jnp.minimum(i, nb[0] - 1), 0, 0)
    grid_spec = pltpu.PrefetchScalarGridSpec(
        num_scalar_prefetch=n_scalar,
        grid=(n_max,),
        in_specs=[
            pl.BlockSpec((EXPERT_ROWS, SLAB, LANES), blk),
            pl.BlockSpec(memory_space=pl.ANY),
            pl.BlockSpec(memory_space=pl.ANY),
            pl.BlockSpec(memory_space=pl.ANY),
        ],
        out_specs=pl.BlockSpec((EXPERT_ROWS, SLAB, LANES), blk),
        scratch_shapes=[
            pltpu.VMEM((2, d, de), F32),
            pltpu.VMEM((2, d, de), F32),
            pltpu.VMEM((2, de, d), F32),
            pltpu.VMEM((d, de), BF16),
            pltpu.VMEM((d, de), BF16),
            pltpu.VMEM((de, d), BF16),
            pltpu.VMEM((EXPERT_ROWS * SLAB, LANES), F32),
            pltpu.SemaphoreType.DMA((2,)),
        ],
    )
    return pl.pallas_call(
        _experts_kernel,
        out_shape=jax.ShapeDtypeStruct(xs.shape, xs.dtype),
        grid_spec=grid_spec,
        input_output_aliases={n_scalar: 0},
        compiler_params=pltpu.CompilerParams(
            dimension_semantics=("arbitrary",), vmem_limit_bytes=VMEM_LIMIT),
        name="experts",
    )(*meta, xs, w_gate, w_up, w_down)


def _expert_blocks(counts, n_assign):
    n_exp = counts.shape[0]
    n_max = (n_assign + n_exp * (EXPERT_ROWS - 1)) // EXPERT_ROWS + 1
    padded = (counts + EXPERT_ROWS - 1) // EXPERT_ROWS * EXPERT_ROWS
    pad_end = jnp.cumsum(padded)
    pad_start = pad_end - padded
    n_used = pad_end[-1] // EXPERT_ROWS
    first_row = jnp.minimum(jnp.arange(n_max, dtype=I32), n_used - 1) * EXPERT_ROWS
    e_of = jnp.sum((first_row[:, None] >= pad_end[None, :]).astype(I32), axis=1)
    e_of = jnp.minimum(e_of, n_exp - 1).astype(I32)
    prev_e = jnp.concatenate([jnp.full((1,), -1, I32), e_of[:-1]])
    new_expert = (e_of != prev_e).astype(I32)
    slot = ((jnp.cumsum(new_expert) - 1) & 1).astype(I32)
    ids = jnp.arange(n_exp, dtype=I32)
    later_used = (ids[None, :] > ids[:, None]) & (padded[None, :] > 0)
    next_used = jnp.min(jnp.where(later_used, ids[None, :], n_exp), axis=1)
    next_used = jnp.where(next_used < n_exp, next_used, -1)
    next_expert = jnp.sum(jnp.where(e_of[:, None] == ids[None, :], next_used[None, :], 0), axis=1).astype(I32)
    n_used = n_used.reshape(1).astype(I32)
    meta = (e_of, new_expert, slot, next_expert, n_used)
    return (meta, n_used, pad_start.astype(I32), (pad_start + counts).astype(I32),
            (padded - counts).astype(I32), n_max)


def _combine_kernel(dest_cur, dest_nxt, x1_ref, mod_ref, wt_ref, wsg_ref, wsu_ref, wsd_ref,
                    ln2g_ref, ln2b_ref, ys_hbm, out_ref, buf, wb_ref, acc_ref, sem, *, alpha):
    tc = x1_ref.shape[0]
    i = pl.program_id(0)
    n = pl.num_programs(0)
    slot = i % 2

    def request(dest_ref, s, t):
        for k in range(TOP_K):
            pltpu.make_async_copy(ys_hbm.at[dest_ref[k, t]], buf.at[s, t * TOP_K + k],
                                  sem.at[s]).start(priority=k % 2)

    @pl.when(i == 0)
    def _():
        def body(t, carry):
            request(dest_cur, 0, t)
            return carry
        lax.fori_loop(0, tc, body, 0)

    pltpu.make_async_copy(ys_hbm.at[pl.ds(0, tc * TOP_K)], buf.at[slot], sem.at[slot]).wait()

    wt = wt_ref[...]
    for k in range(TOP_K):
        wb_ref[k] = jnp.broadcast_to(wt[:, k:k + 1], (tc, LANES))

    def token(t, carry, prefetch):
        if prefetch:
            request(dest_nxt, 1 - slot, t)
        terms = [jnp.broadcast_to(wb_ref[k, pl.ds(t, 1), :], (SLAB, LANES))
                 * buf[slot, t * TOP_K + k].astype(F32)
                 for k in range(TOP_K)]
        while len(terms) > 1:
            terms = [a + b for a, b in zip(terms[0::2], terms[1::2])]
        acc_ref[pl.ds(pl.multiple_of(t * SLAB, SLAB), SLAB), :] = terms[0]
        return carry

    @pl.when(i + 1 < n)
    def _():
        lax.fori_loop(0, tc, functools.partial(token, prefetch=True), 0, unroll=8)

    @pl.when(i + 1 == n)
    def _():
        lax.fori_loop(0, tc, functools.partial(token, prefetch=False), 0, unroll=8)

    mod = mod_ref[0]
    sh_f, sc_f, gt_f = mod[3:4], mod[4:5], mod[5:6]
    x1 = x1_ref[...]
    hb = (_ln(x1) * (1.0 + sc_f) + sh_f).astype(BF16)
    mid = _silu(_dot(hb, wsg_ref[...])) * _dot(hb, wsu_ref[...])
    ffn = _dot(mid.astype(BF16), wsd_ref[...])
    ffn = _slabs_to_rows(acc_ref, tc) + ffn
    out_ref[...] = _ln(alpha * x1 + (1.0 + gt_f) * ffn) * ln2g_ref[...] + ln2b_ref[...]


def _combine(dest, x1_flat, mod, wt, ws_gate, ws_up, ws_down, ln2_g, ln2_b, ys, seq, alpha):
    t, d = x1_flat.shape
    tc = min(COMBINE_TOKENS, seq)
    n = t // tc
    per_seq = seq // tc
    const = lambda i: (0, 0)
    return pl.pallas_call(
        functools.partial(_combine_kernel, alpha=alpha),
        out_shape=jax.ShapeDtypeStruct((t, d), F32),
        grid=(n,),
        in_specs=[
            pl.BlockSpec((TOP_K, tc), lambda i: (0, i), memory_space=pltpu.SMEM),
            pl.BlockSpec((TOP_K, tc), lambda i: (0, jnp.minimum(i + 1, n - 1)), memory_space=pltpu.SMEM),
            pl.BlockSpec((tc, d), lambda i: (i, 0)),
            pl.BlockSpec((1, SUBLANES, d), lambda i: (i // per_seq, 0, 0)),
            pl.BlockSpec((tc, LANES), lambda i: (i, 0)),
            pl.BlockSpec(ws_gate.shape, const),
            pl.BlockSpec(ws_up.shape, const),
            pl.BlockSpec(ws_down.shape, const),
            pl.BlockSpec(ln2_g.shape, const),
            pl.BlockSpec(ln2_b.shape, const),
            pl.BlockSpec(memory_space=pl.ANY),
        ],
        out_specs=pl.BlockSpec((tc, d), lambda i: (i, 0)),
        scratch_shapes=[
            pltpu.VMEM((2, tc * TOP_K, SLAB, LANES), BF16),
            pltpu.VMEM((TOP_K, tc, LANES), F32),
            pltpu.VMEM((tc * SLAB, LANES), F32),
            pltpu.SemaphoreType.DMA((2,)),
        ],
        compiler_params=pltpu.CompilerParams(
            dimension_semantics=("arbitrary",), vmem_limit_bytes=VMEM_LIMIT),
        name="combine",
    )(dest, dest, x1_flat, mod, wt, ws_gate, ws_up, ws_down, ln2_g, ln2_b, ys)


def _layer(x, c_pad, lw, alpha):
    bsz, seq, d = x.shape
    t = bsz * seq
    n_exp = lw["w_router"].shape[1]
    h_count = lw["w_if"].shape[1] // 2

    mod = _ada(c_pad, lw["w_ada"], lw["b_ada"].reshape(1, -1))[:bsz].reshape(bsz, 6, d)
    mod = jnp.pad(mod, ((0, 0), (0, SUBLANES - 6), (0, 0)))

    row2 = lambda a: a.reshape(1, -1)
    w_if = lw["w_if"]
    w_if_pad = jnp.zeros((w_if.shape[0], 2 * LANES), F32)
    w_if_pad = w_if_pad.at[:, :h_count].set(w_if[:, :h_count]).at[:, LANES:LANES + h_count].set(w_if[:, h_count:])
    b_if_pad = jnp.zeros((1, 2 * LANES), F32)
    b_if_pad = b_if_pad.at[0, :h_count].set(lw["b_if"][:h_count]).at[0, LANES:LANES + h_count].set(lw["b_if"][h_count:])
    w_rt = lw["w_router"].astype(F32).T
    wr_hi = w_rt.astype(BF16)
    wr_lo = (w_rt - wr_hi.astype(F32)).astype(BF16)
    p = {
        "w_in": lw["w_in"].astype(BF16), "conv_w": lw["conv_w"], "conv_b": row2(lw["conv_b"]),
        "w_q": lw["w_q"].astype(BF16), "w_k": lw["w_k"].astype(BF16), "w_v": lw["w_v"].astype(BF16),
        "w_if": w_if_pad.astype(BF16), "b_if": b_if_pad,
        "mh_g": row2(lw["mh_g"]), "skip": row2(lw["skip"]), "sg_g": row2(lw["sg_g"]), "sg_b": row2(lw["sg_b"]),
        "w_sp": lw["w_sp"], "b_sp_t": lw["b_sp"].T, "w_out": lw["w_out"].astype(BF16),
        "ln1_g": row2(lw["ln1_g"]), "ln1_b": row2(lw["ln1_b"]), "wr_hi": wr_hi, "wr_lo": wr_lo,
    }
    x1, h2, logt = _mixer(x, mod, p, alpha)

    tr = min(ROUTE_TOKENS, t)
    upper = (jnp.arange(tr)[:, None] < jnp.arange(tr)[None, :]).astype(BF16)
    idx, _, rank, wt, cnt = _route(logt, lw["e_bias"].astype(F32).reshape(n_exp, 1), upper)
    counts = cnt[:, 0].astype(I32)
    meta, n_used, pad_start, pad_first, pad_len, n_blocks_max = _expert_blocks(counts, t * TOP_K)
    dest = _dest(idx, rank, pad_start.reshape(n_exp, 1))

    xs = _dispatch(pad_first, pad_len, n_used, dest, h2, n_blocks_max * EXPERT_ROWS)
    ys = _experts(meta, xs, lw["w_gate"], lw["w_up"], lw["w_down"])
    out = _combine(dest, x1.reshape(t, d), mod, wt, lw["ws_gate"].astype(BF16), lw["ws_up"].astype(BF16),
                   lw["ws_down"].astype(BF16), row2(lw["ln2_g"]), row2(lw["ln2_b"]), ys, seq, alpha)
    return out.reshape(bsz, seq, d)


def kernel(x, c, w_ada, b_ada, w_in, conv_w, conv_b, w_q, w_k, w_v, w_if, b_if, mh_g, skip, sg_g, sg_b, w_sp, b_sp, w_out, ln1_g, ln1_b, w_router, e_bias, w_gate, w_up, w_down, ws_gate, ws_up, ws_down, ln2_g, ln2_b):
    stacked = dict(w_ada=w_ada, b_ada=b_ada, w_in=w_in, conv_w=conv_w, conv_b=conv_b, w_q=w_q, w_k=w_k,
                   w_v=w_v, w_if=w_if, b_if=b_if, mh_g=mh_g, skip=skip, sg_g=sg_g, sg_b=sg_b, w_sp=w_sp,
                   b_sp=b_sp, w_out=w_out, ln1_g=ln1_g, ln1_b=ln1_b, w_router=w_router, e_bias=e_bias,
                   w_gate=w_gate, w_up=w_up, w_down=w_down, ws_gate=ws_gate, ws_up=ws_up, ws_down=ws_down,
                   ln2_g=ln2_g, ln2_b=ln2_b)
    depth = w_ada.shape[0]
    alpha = float((2 * depth) ** 0.25)
    bsz = x.shape[0]
    c_pad = jnp.pad(c, ((0, -bsz % SUBLANES), (0, 0)))
    for l in range(depth):
        x = _layer(x, c_pad, {k: v[l] for k, v in stacked.items()}, alpha)
    return x
```

```python
import functools
import math

import jax
import jax.numpy as jnp
from jax import lax
from jax.experimental import pallas as pl
from jax.experimental.pallas import tpu as pltpu

F32 = jnp.float32
BF16 = jnp.bfloat16
I32 = jnp.int32

LN_EPS = 1e-5
M_HEADS = 4
G_GROUPS = 4
CHUNK = 128
N_GROUPS = 8
TOPK_GROUPS = 4
TOP_K = 8
ROUTE_SCALE = 2.5
LANES = 128
SUBLANES = 8

MIXER_TOKENS = 512
MIXER_CHUNK_GROUP = 4
ROUTE_TOKENS = 1024
DEST_TOKENS = 2048
DISPATCH_TOKENS = 2048
EXPERT_ROWS = 512
COMBINE_TOKENS = 512
VMEM_LIMIT = 56 * 1024 * 1024

NEG_INF = float("-inf")


def _ln(x):
    mu = jnp.mean(x, axis=-1, keepdims=True)
    xc = x - mu
    var = jnp.mean(xc * xc, axis=-1, keepdims=True)
    return xc * lax.rsqrt(var + LN_EPS)


def _dot(a, b):
    return jnp.dot(a, b, preferred_element_type=F32)


def _dot_nt(a, b):
    return lax.dot_general(a, b, (((1,), (1,)), ((), ())), preferred_element_type=F32)


def _dot_exact(a, b):
    return jnp.dot(a, b, preferred_element_type=F32, precision=lax.Precision.HIGHEST)


def _silu(x):
    return x * jax.nn.sigmoid(x)


def _gelu(x):
    return 0.5 * x * (1.0 + lax.erf(x * math.sqrt(0.5)))


def _log_sigmoid(x):
    return jnp.minimum(x, 0.0) - jnp.log1p(jnp.exp(-jnp.abs(x)))


SLAB = SUBLANES


def _to_slabs(x, scratch):
    rows, d = x.shape
    assert d == SLAB * LANES
    for s in range(SLAB):
        scratch[pl.ds(s, rows, stride=SLAB), :] = x[:, s * LANES:(s + 1) * LANES]
    return scratch[0:rows * SLAB, :].reshape(rows, SLAB, LANES).astype(BF16)


def _slabs_to_rows(scratch, rows):
    return jnp.concatenate([scratch[pl.ds(s, rows, stride=SLAB), :] for s in range(SLAB)], axis=1)


def _from_slabs(x3, scratch):
    rows = x3.shape[0]
    scratch[0:rows * SLAB, :] = x3.astype(F32).reshape(rows * SLAB, LANES)
    return _slabs_to_rows(scratch, rows)


def _ada_kernel(c_ref, w_ref, b_ref, o_ref):
    o_ref[...] = _dot_exact(_silu(c_ref[...]), w_ref[...]) + b_ref[...]


def _ada(c_pad, w_ada, b_ada):
    rows, d = c_pad.shape
    n = w_ada.shape[1]
    return pl.pallas_call(
        _ada_kernel,
        out_shape=jax.ShapeDtypeStruct((rows, n), F32),
        grid=(n // d,),
        in_specs=[
            pl.BlockSpec((rows, d), lambda j: (0, 0)),
            pl.BlockSpec((d, d), lambda j: (0, j)),
            pl.BlockSpec((1, d), lambda j: (0, j)),
        ],
        out_specs=pl.BlockSpec((rows, d), lambda j: (0, j)),
        compiler_params=pltpu.CompilerParams(vmem_limit_bytes=VMEM_LIMIT),
        name="ada",
    )(c_pad, w_ada, b_ada)


def _mixer_kernel(x_ref, mod_ref, w_in_ref, conv_w_ref, conv_b_ref, wq_ref, wk_ref, wv_ref,
                  wif_ref, bif_ref, mhg_ref, skip_ref, sgg_ref, sgb_ref, wsp_ref, bspt_ref,
                  w_out_ref, ln1g_ref, ln1b_ref, wr_hi_ref, wr_lo_ref,
                  x1_ref, h2_ref, logt_ref,
                  state_ref, m_ref, xm_ref, cat_ref, slab_ref, *, alpha):
    tm = x_ref.shape[1]
    mw = conv_w_ref.shape[1]
    dh = mw // M_HEADS
    gw = sgg_ref.shape[1]
    gd = gw // G_GROUPS
    conv_k = conv_w_ref.shape[0]
    nch = tm // CHUNK
    L = CHUNK

    @pl.when(pl.program_id(1) == 0)
    def _():
        state_ref[...] = jnp.zeros_like(state_ref)
        m_ref[...] = jnp.zeros_like(m_ref)
        xm_ref[0:SUBLANES, :] = jnp.zeros((SUBLANES, mw), F32)

    x = x_ref[0]
    mod = mod_ref[0]
    sh_a, sc_a, gt_a = mod[0:1], mod[1:2], mod[2:3]
    sh_f, sc_f = mod[3:4], mod[4:5]

    h = _ln(x) * (1.0 + sc_a) + sh_a
    proj = _dot(h.astype(BF16), w_in_ref[...])
    xm = proj[:, :mw]
    z = proj[:, mw:2 * mw]
    u = proj[:, 2 * mw:2 * mw + gw]
    v = proj[:, 2 * mw + gw:]

    xm_ref[SUBLANES:SUBLANES + tm, :] = xm
    conv = jnp.broadcast_to(conv_b_ref[...], (tm, mw))
    for j in range(conv_k):
        off = SUBLANES - (conv_k - 1) + j
        conv = conv + conv_w_ref[j:j + 1, :] * xm_ref[off:off + tm, :]
    xm_ref[0:SUBLANES, :] = xm_ref[tm:tm + SUBLANES, :]
    xc = _silu(conv)

    scale = dh ** -0.5
    qs, ks, vs = [], [], []
    for hd in range(M_HEADS):
        sl = slice(hd * dh, (hd + 1) * dh)
        xch = xc[:, sl].astype(BF16)
        qs.append(_dot(xch, wq_ref[hd]))
        ks.append(_dot(xch, wk_ref[hd]) * scale)
        vs.append(_dot(xm[:, sl].astype(BF16), wv_ref[hd]))
    qkv = jnp.concatenate(qs + ks + vs, axis=1).astype(BF16)
    gate = _dot(qkv, wif_ref[...]) + bif_ref[...]
    gi = gate[:, :LANES]
    lf = _log_sigmoid(gate[:, LANES:])

    row = lax.broadcasted_iota(I32, (L, L), 0)
    col = lax.broadcasted_iota(I32, (L, L), 1)
    causal = col <= row
    tri = jnp.where(causal, 1.0, 0.0).astype(F32)
    ones_l = jnp.ones((L, dh), F32)
    wsg = [jnp.where(causal, wsp_ref[gg], 0.0).astype(BF16) for gg in range(G_GROUPS)]
    state = [state_ref[hd] for hd in range(M_HEADS)]
    m_run = [m_ref[hd:hd + 1, 0:1] for hd in range(M_HEADS)]

    heads = range(M_HEADS)
    hsl = [slice(hd * dh, (hd + 1) * dh) for hd in heads]
    groups = range(G_GROUPS)
    gsl = [slice(gg * gd, (gg + 1) * gd) for gg in groups]

    for c0 in range(0, nch, MIXER_CHUNK_GROUP):
        chunks = range(c0, min(c0 + MIXER_CHUNK_GROUP, nch))
        rsl = {c: slice(c * L, (c + 1) * L) for c in chunks}
        pairs = [(c, hd) for c in chunks for hd in heads]
        bmat = {c: _dot_exact(tri, lf[rsl[c]]) for c in chunks}
        rmat = {c: gi[rsl[c]] - bmat[c] for c in chunks}
        rmat_t = {c: rmat[c].T for c in chunks}
        bcol = {(c, hd): bmat[c][:, hd:hd + 1] for c, hd in pairs}
        rrow = {(c, hd): rmat_t[c][hd:hd + 1, :] for c, hd in pairs}
        g = {(c, hd): bmat[c][L - 1:L, hd:hd + 1] for c, hd in pairs}
        a = {(c, hd): g[c, hd] + rmat[c][:, hd:hd + 1] for c, hd in pairs}
        a_max = {pr: jnp.max(a[pr], axis=0, keepdims=True) for pr in pairs}

        m_in, m_out = {}, {}
        for c in chunks:
            for hd in heads:
                m_in[c, hd] = m_run[hd]
                m_out[c, hd] = jnp.maximum(g[c, hd] + m_run[hd], a_max[c, hd])
            m_run = [m_out[c, hd] for hd in heads]

        k = {(c, hd): ks[hd][rsl[c]] for c, hd in pairs}
        qb = {(c, hd): qs[hd][rsl[c]].astype(BF16) for c, hd in pairs}
        kb = {pr: k[pr].astype(BF16) for pr in pairs}
        vext = {(c, hd): jnp.concatenate([vs[hd][rsl[c]], ones_l], axis=1).astype(BF16)
                for c, hd in pairs}

        log_d = {pr: jnp.where(causal, bcol[pr] + rrow[pr], NEG_INF) for pr in pairs}
        log_inter = {pr: bcol[pr] + m_in[pr] for pr in pairs}
        m_t = {pr: jnp.maximum(log_inter[pr], jnp.max(log_d[pr], axis=1, keepdims=True)) for pr in pairs}
        qk = {pr: _dot_nt(qb[pr], kb[pr]) for pr in pairs}
        p = {pr: (jnp.exp(log_d[pr] - m_t[pr]) * qk[pr]).astype(BF16) for pr in pairs}
        w_inter = {pr: jnp.exp(log_inter[pr] - m_t[pr]) for pr in pairs}
        intra = {pr: _dot(p[pr], vext[pr]) for pr in pairs}
        decay = {pr: jnp.exp(g[pr] + m_in[pr] - m_out[pr]) for pr in pairs}
        kw_t = {pr: (k[pr] * jnp.exp(a[pr] - m_out[pr])).T.astype(BF16) for pr in pairs}
        update = {pr: _dot(kw_t[pr], vext[pr]) for pr in pairs}

        st_in = {}
        for c in chunks:
            for hd in heads:
                st_in[c, hd] = state[hd]
            state = [decay[c, hd] * state[hd] + update[c, hd] for hd in heads]

        inter = {pr: _dot(qb[pr], st_in[pr].astype(BF16)) for pr in pairs}
        out_ext = {pr: w_inter[pr] * inter[pr] + intra[pr] for pr in pairs}
        hh = {pr: out_ext[pr][:, :dh] / jnp.maximum(jnp.abs(out_ext[pr][:, dh:]), jnp.exp(-m_t[pr]))
              for pr in pairs}
        hc = {(c, hd): _ln(hh[c, hd]) * mhg_ref[:, hsl[hd]] for c, hd in pairs}
        for c, hd in pairs:
            hm = jax.nn.sigmoid(z[rsl[c], hsl[hd]]) * (hc[c, hd] + skip_ref[:, hsl[hd]] * xc[rsl[c], hsl[hd]])
            cat_ref[rsl[c], hsl[hd]] = hm.astype(BF16)

        for c in chunks:
            ug = _gelu(u[rsl[c]])
            vg = _gelu(v[rsl[c]])
            vn = [(_ln(vg[:, gsl[gg]]) * sgg_ref[:, gsl[gg]] + sgb_ref[:, gsl[gg]]).astype(BF16) for gg in groups]
            sp = [_dot(wsg[gg], vn[gg]) + bspt_ref[:, gg:gg + 1] for gg in groups]
            for gg in groups:
                cat_ref[rsl[c], mw + gg * gd:mw + (gg + 1) * gd] = (ug[:, gsl[gg]] * sp[gg]).astype(BF16)

    for hd in range(M_HEADS):
        state_ref[hd] = state[hd]
        m_ref[hd:hd + 1, :] = jnp.broadcast_to(m_run[hd], (1, LANES))

    mix = _dot(cat_ref[...], w_out_ref[...])
    x1 = _ln(alpha * x + (1.0 + gt_a) * mix) * ln1g_ref[...] + ln1b_ref[...]
    x1_ref[0] = x1
    h2 = _ln(x1) * (1.0 + sc_f) + sh_f
    h2_ref[...] = _to_slabs(h2, slab_ref)
    h_hi = h2.astype(BF16)
    h_lo = (h2 - h_hi.astype(F32)).astype(BF16)
    wr_hi = wr_hi_ref[...]
    logt_ref[...] = _dot_nt(wr_hi, h_hi) + _dot_nt(wr_hi, h_lo) + _dot_nt(wr_lo_ref[...], h_hi)


def _mixer(x, mod, p, alpha):
    bsz, seq, d = x.shape
    tm = min(MIXER_TOKENS, seq)
    nt = seq // tm
    mw = p["conv_w"].shape[1]
    gw = p["sg_g"].shape[1]
    dh = mw // M_HEADS
    n_exp = p["wr_hi"].shape[0]
    const2 = lambda b, i: (0, 0)
    const3 = lambda b, i: (0, 0, 0)
    full = lambda a: pl.BlockSpec(a.shape, const2 if a.ndim == 2 else const3)
    names = ["w_in", "conv_w", "conv_b", "w_q", "w_k", "w_v", "w_if", "b_if", "mh_g", "skip", "sg_g",
             "sg_b", "w_sp", "b_sp_t", "w_out", "ln1_g", "ln1_b", "wr_hi", "wr_lo"]
    weights = [p[n] for n in names]
    return pl.pallas_call(
        functools.partial(_mixer_kernel, alpha=alpha),
        out_shape=(
            jax.ShapeDtypeStruct((bsz, seq, d), F32),
            jax.ShapeDtypeStruct((bsz * seq, SLAB, LANES), BF16),
            jax.ShapeDtypeStruct((n_exp, bsz * seq), F32),
        ),
        grid=(bsz, nt),
        in_specs=[
            pl.BlockSpec((1, tm, d), lambda b, i: (b, i, 0)),
            pl.BlockSpec((1, SUBLANES, d), lambda b, i: (b, 0, 0)),
        ] + [full(w) for w in weights],
        out_specs=(
            pl.BlockSpec((1, tm, d), lambda b, i: (b, i, 0)),
            pl.BlockSpec((tm, SLAB, LANES), lambda b, i: (b * nt + i, 0, 0)),
            pl.BlockSpec((n_exp, tm), lambda b, i: (0, b * nt + i)),
        ),
        scratch_shapes=[
            pltpu.VMEM((M_HEADS, dh, 2 * dh), F32),
            pltpu.VMEM((SUBLANES, LANES), F32),
            pltpu.VMEM((tm + SUBLANES, mw), F32),
            pltpu.VMEM((tm, mw + gw), BF16),
            pltpu.VMEM((tm * SLAB, LANES), F32),
        ],
        compiler_params=pltpu.CompilerParams(
            dimension_semantics=("arbitrary", "arbitrary"), vmem_limit_bytes=VMEM_LIMIT),
        name="mixer",
    )(x, mod, *weights)


def _route_kernel(logt_ref, bias_ref, upper_ref, idx_ref, w_ref, rank_ref, wt_ref, cnt_ref, carry_ref):
    n_exp, tr = logt_ref.shape
    epg = n_exp // N_GROUPS

    @pl.when(pl.program_id(0) == 0)
    def _():
        carry_ref[...] = jnp.zeros_like(carry_ref)

    scores = jax.nn.sigmoid(logt_ref[...])
    sel = scores + bias_ref[...]

    sel3 = sel.reshape(N_GROUPS, epg, tr)
    io3 = lax.broadcasted_iota(I32, (N_GROUPS, epg, tr), 1)
    m1 = jnp.max(sel3, axis=1, keepdims=True)
    first = jnp.min(jnp.where(sel3 == m1, io3, epg), axis=1, keepdims=True)
    m2 = jnp.max(jnp.where(io3 == first, NEG_INF, sel3), axis=1, keepdims=True)
    gs = (m1 + m2).reshape(N_GROUPS, tr)

    gio = lax.broadcasted_iota(I32, (N_GROUPS, tr), 0)
    gmask = jnp.zeros((N_GROUPS, tr), F32)
    for _ in range(TOPK_GROUPS):
        m = jnp.max(gs, axis=0, keepdims=True)
        gi = jnp.min(jnp.where(gs == m, gio, N_GROUPS), axis=0, keepdims=True)
        hit = gio == gi
        gmask = jnp.where(hit, 1.0, gmask)
        gs = jnp.where(hit, NEG_INF, gs)
    emask = jnp.broadcast_to(gmask.reshape(N_GROUPS, 1, tr), (N_GROUPS, epg, tr)).reshape(n_exp, tr)
    selm = jnp.where(emask > 0.0, sel, NEG_INF)

    eio = lax.broadcasted_iota(I32, (n_exp, tr), 0)
    chosen = jnp.zeros((n_exp, tr), F32)
    idx_rows, w_rows = [], []
    for _ in range(TOP_K):
        m = jnp.max(selm, axis=0, keepdims=True)
        ei = jnp.min(jnp.where(selm == m, eio, n_exp), axis=0, keepdims=True)
        hit = eio == ei
        w_rows.append(jnp.sum(jnp.where(hit, scores, 0.0), axis=0, keepdims=True))
        idx_rows.append(ei)
        selm = jnp.where(hit, NEG_INF, selm)
        chosen = jnp.where(hit, 1.0, chosen)

    chosen_b = chosen.astype(BF16)
    carry = carry_ref[...]
    ranks = carry[:, 0:1] + _dot(chosen_b, upper_ref[...])
    carry_new = carry + _dot(chosen_b, jnp.ones((tr, LANES), BF16))
    carry_ref[...] = carry_new
    cnt_ref[...] = carry_new
    rank_rows = [jnp.sum(jnp.where(eio == ei, ranks, 0.0), axis=0, keepdims=True) for ei in idx_rows]

    wsum = w_rows[0]
    for wk in w_rows[1:]:
        wsum = wsum + wk
    w8 = jnp.concatenate([wk / wsum * ROUTE_SCALE for wk in w_rows], axis=0)
    idx_ref[...] = jnp.concatenate(idx_rows, axis=0)
    rank_ref[...] = jnp.concatenate(rank_rows, axis=0).astype(I32)
    w_ref[...] = w8
    wpad = jnp.concatenate([w8, jnp.zeros((LANES - TOP_K, tr), F32)], axis=0)
    wt_ref[...] = wpad.T


def _route(logt, e_bias_col, upper):
    n_exp, t = logt.shape
    tr = upper.shape[0]
    return pl.pallas_call(
        _route_kernel,
        out_shape=(
            jax.ShapeDtypeStruct((TOP_K, t), I32),
            jax.ShapeDtypeStruct((TOP_K, t), F32),
            jax.ShapeDtypeStruct((TOP_K, t), I32),
            jax.ShapeDtypeStruct((t, LANES), F32),
            jax.ShapeDtypeStruct((n_exp, LANES), F32),
        ),
        grid=(t // tr,),
        in_specs=[
            pl.BlockSpec((n_exp, tr), lambda i: (0, i)),
            pl.BlockSpec((n_exp, 1), lambda i: (0, 0)),
            pl.BlockSpec((tr, tr), lambda i: (0, 0)),
        ],
        out_specs=(
            pl.BlockSpec((TOP_K, tr), lambda i: (0, i)),
            pl.BlockSpec((TOP_K, tr), lambda i: (0, i)),
            pl.BlockSpec((TOP_K, tr), lambda i: (0, i)),
            pl.BlockSpec((tr, LANES), lambda i: (i, 0)),
            pl.BlockSpec((n_exp, LANES), lambda i: (0, 0)),
        ),
        scratch_shapes=[pltpu.VMEM((n_exp, LANES), F32)],
        compiler_params=pltpu.CompilerParams(
            dimension_semantics=("arbitrary",), vmem_limit_bytes=VMEM_LIMIT),
        name="route",
    )(logt, e_bias_col, upper)


def _dest_kernel(idx_ref, rank_ref, start_ref, dest_ref):
    n_exp = start_ref.shape[0]
    tt = idx_ref.shape[1]
    eio = lax.broadcasted_iota(I32, (n_exp, tt), 0)
    start = start_ref[...]
    rows = []
    for k in range(TOP_K):
        hit = eio == idx_ref[k:k + 1, :]
        rows.append(jnp.sum(jnp.where(hit, start, 0), axis=0, keepdims=True) + rank_ref[k:k + 1, :])
    dest_ref[...] = jnp.concatenate(rows, axis=0)


def _dest(idx, rank, start_col):
    t = idx.shape[1]
    tt = min(DEST_TOKENS, t)
    n_exp = start_col.shape[0]
    return pl.pallas_call(
        _dest_kernel,
        out_shape=jax.ShapeDtypeStruct((TOP_K, t), I32),
        grid=(t // tt,),
        in_specs=[
            pl.BlockSpec((TOP_K, tt), lambda i: (0, i)),
            pl.BlockSpec((TOP_K, tt), lambda i: (0, i)),
            pl.BlockSpec((n_exp, 1), lambda i: (0, 0)),
        ],
        out_specs=pl.BlockSpec((TOP_K, tt), lambda i: (0, i)),
        compiler_params=pltpu.CompilerParams(vmem_limit_bytes=VMEM_LIMIT),
        name="dest",
    )(idx, rank, start_col)


def _slabs_copy(hbm, sem, n_slabs):
    return pltpu.make_async_copy(hbm.at[pl.ds(0, n_slabs)], hbm.at[pl.ds(0, n_slabs)], sem)


def _dispatch_kernel(pad_first, pad_len, n_used, dest_ref, h_ref, xs_hbm, zeros_ref, sem):
    td = dest_ref.shape[1]
    n_exp = pad_first.shape[0]
    n_blocks_max = xs_hbm.shape[0] // EXPERT_ROWS
    step = pl.program_id(0)

    def zero_fill(act):
        def zeros_to(first_row, n_rows):
            act(pltpu.make_async_copy(zeros_ref.at[pl.ds(0, n_rows)], xs_hbm.at[pl.ds(first_row, n_rows)],
                                      sem.at[1]))

        def expert_padding(e, carry):
            row = pad_first[e]
            for bit in reversed(range(EXPERT_ROWS.bit_length() - 1)):
                take = (pad_len[e] >> bit) & 1
                pl.when(take == 1)(functools.partial(zeros_to, row, 1 << bit))
                row = row + (take << bit)
            return carry

        def unused_block(b, carry):
            zeros_to(b * EXPERT_ROWS, EXPERT_ROWS)
            return carry

        lax.fori_loop(0, n_exp, expert_padding, 0)
        lax.fori_loop(n_used[0], n_blocks_max, unused_block, 0)

    @pl.when(step == 0)
    def _():
        zeros_ref[...] = jnp.zeros_like(zeros_ref)
        zero_fill(lambda cp: cp.start())

    def body(t, carry):
        for k in range(TOP_K):
            pltpu.make_async_copy(h_ref.at[t], xs_hbm.at[dest_ref[k, t]], sem.at[0]).start(priority=k % 2)
        return carry

    lax.fori_loop(0, td, body, 0, unroll=2)
    _slabs_copy(xs_hbm, sem.at[0], td * TOP_K).wait()

    @pl.when(step == pl.num_programs(0) - 1)
    def _():
        zero_fill(lambda cp: cp.wait())


def _dispatch(pad_first, pad_len, n_used, dest, h_slabs, n_rows):
    t = h_slabs.shape[0]
    td = min(DISPATCH_TOKENS, t)
    grid_spec = pltpu.PrefetchScalarGridSpec(
        num_scalar_prefetch=3,
        grid=(t // td,),
        in_specs=[
            pl.BlockSpec((TOP_K, td), lambda i, pf, pn, nu: (0, i), memory_space=pltpu.SMEM),
            pl.BlockSpec((td, SLAB, LANES), lambda i, pf, pn, nu: (i, 0, 0)),
        ],
        out_specs=pl.BlockSpec(memory_space=pl.ANY),
        scratch_shapes=[
            pltpu.VMEM((EXPERT_ROWS, SLAB, LANES), h_slabs.dtype),
            pltpu.SemaphoreType.DMA((2,)),
        ],
    )
    return pl.pallas_call(
        _dispatch_kernel,
        out_shape=jax.ShapeDtypeStruct((n_rows, SLAB, LANES), h_slabs.dtype),
        grid_spec=grid_spec,
        compiler_params=pltpu.CompilerParams(
            dimension_semantics=("arbitrary",), vmem_limit_bytes=VMEM_LIMIT),
        name="dispatch",
    )(pad_first, pad_len, n_used, dest, h_slabs)


def _experts_kernel(block_expert, block_new_expert, block_slot, block_next_expert, n_blocks,
                    xs_ref, wg_hbm, wu_hbm, wd_hbm, ys_ref,
                    wg_f, wu_f, wd_f, wg_b, wu_b, wd_b, slab_ref, sem):
    i = pl.program_id(0)

    def weight_copies(e, slot):
        return [pltpu.make_async_copy(hbm.at[e], buf.at[slot], sem.at[slot])
                for hbm, buf in ((wg_hbm, wg_f), (wu_hbm, wu_f), (wd_hbm, wd_f))]

    @pl.when(i < n_blocks[0])
    def _():
        @pl.when(block_new_expert[i] == 1)
        def _():
            slot = block_slot[i]

            @pl.when(i == 0)
            def _():
                for cp in weight_copies(block_expert[i], slot):
                    cp.start()

            for cp in weight_copies(block_expert[i], slot):
                cp.wait()
            wg_b[...] = wg_f[slot].astype(BF16)
            wu_b[...] = wu_f[slot].astype(BF16)
            wd_b[...] = wd_f[slot].astype(BF16)

            @pl.when(block_next_expert[i] >= 0)
            def _():
                for cp in weight_copies(block_next_expert[i], 1 - slot):
                    cp.start()

        xb = _from_slabs(xs_ref[...], slab_ref).astype(BF16)
        g = _dot(xb, wg_b[...])
        u = _dot(xb, wu_b[...])
        y = _dot((_silu(g) * u).astype(BF16), wd_b[...])
        ys_ref[...] = _to_slabs(y, slab_ref)


def _experts(meta, xs, w_gate, w_up, w_down):
    n_scalar = len(meta)
    _, d, de = w_gate.shape
    n_max = meta[0].shape[0]
    blk = lambda i, be, ne, sl, nx, nb: (jnp.minimum(i, nb[0] - 1), 0, 0)
    grid_spec = pltpu.PrefetchScalarGridSpec(
        num_scalar_prefetch=n_scalar,
        grid=(n_max,),
        in_specs=[
            pl.BlockSpec((EXPERT_ROWS, SLAB, LANES), blk),
            pl.BlockSpec(memory_space=pl.ANY),
            pl.BlockSpec(memory_space=pl.ANY),
            pl.BlockSpec(memory_space=pl.ANY),
        ],
        out_specs=pl.BlockSpec((EXPERT_ROWS, SLAB, LANES), blk),
        scratch_shapes=[
            pltpu.VMEM((2, d, de), F32),
            pltpu.VMEM((2, d, de), F32),
            pltpu.VMEM((2, de, d), F32),
            pltpu.VMEM((d, de), BF16),
            pltpu.VMEM((d, de), BF16),
            pltpu.VMEM((de, d), BF16),
            pltpu.VMEM((EXPERT_ROWS * SLAB, LANES), F32),
            pltpu.SemaphoreType.DMA((2,)),
        ],
    )
    return pl.pallas_call(
        _experts_kernel,
        out_shape=jax.ShapeDtypeStruct(xs.shape, xs.dtype),
        grid_spec=grid_spec,
        input_output_aliases={n_scalar: 0},
        compiler_params=pltpu.CompilerParams(
            dimension_semantics=("arbitrary",), vmem_limit_bytes=VMEM_LIMIT),
        name="experts",
    )(*meta, xs, w_gate, w_up, w_down)


def _expert_blocks(counts, n_assign):
    n_exp = counts.shape[0]
    n_max = (n_assign + n_exp * (EXPERT_ROWS - 1)) // EXPERT_ROWS + 1
    padded = (counts + EXPERT_ROWS - 1) // EXPERT_ROWS * EXPERT_ROWS
    pad_end = jnp.cumsum(padded)
    pad_start = pad_end - padded
    n_used = pad_end[-1] // EXPERT_ROWS
    first_row = jnp.minimum(jnp.arange(n_max, dtype=I32), n_used - 1) * EXPERT_ROWS
    e_of = jnp.sum((first_row[:, None] >= pad_end[None, :]).astype(I32), axis=1)
    e_of = jnp.minimum(e_of, n_exp - 1).astype(I32)
    prev_e = jnp.concatenate([jnp.full((1,), -1, I32), e_of[:-1]])
    new_expert = (e_of != prev_e).astype(I32)
    slot = ((jnp.cumsum(new_expert) - 1) & 1).astype(I32)
    ids = jnp.arange(n_exp, dtype=I32)
    later_used = (ids[None, :] > ids[:, None]) & (padded[None, :] > 0)
    next_used = jnp.min(jnp.where(later_used, ids[None, :], n_exp), axis=1)
    next_used = jnp.where(next_used < n_exp, next_used, -1)
    next_expert = jnp.sum(jnp.where(e_of[:, None] == ids[None, :], next_used[None, :], 0), axis=1).astype(I32)
    n_used = n_used.reshape(1).astype(I32)
    meta = (e_of, new_expert, slot, next_expert, n_used)
    return (meta, n_used, pad_start.astype(I32), (pad_start + counts).astype(I32),
            (padded - counts).astype(I32), n_max)


def _combine_kernel(dest_cur, dest_nxt, x1_ref, mod_ref, wt_ref, wsg_ref, wsu_ref, wsd_ref,
                    ln2g_ref, ln2b_ref, ys_hbm, out_ref, buf, wb_ref, acc_ref, sem, *, alpha):
    tc = x1_ref.shape[0]
    i = pl.program_id(0)
    n = pl.num_programs(0)
    slot = i % 2

    def request(dest_ref, s, t):
        for k in range(TOP_K):
            pltpu.make_async_copy(ys_hbm.at[dest_ref[k, t]], buf.at[s, t * TOP_K + k],
                                  sem.at[s]).start(priority=k % 2)

    @pl.when(i == 0)
    def _():
        def body(t, carry):
            request(dest_cur, 0, t)
            return carry
        lax.fori_loop(0, tc, body, 0)

    pltpu.make_async_copy(ys_hbm.at[pl.ds(0, tc * TOP_K)], buf.at[slot], sem.at[slot]).wait()

    wt = wt_ref[...]
    for k in range(TOP_K):
        wb_ref[k] = jnp.broadcast_to(wt[:, k:k + 1], (tc, LANES))

    def token(t, carry, prefetch):
        if prefetch:
            request(dest_nxt, 1 - slot, t)
        terms = [jnp.broadcast_to(wb_ref[k, pl.ds(t, 1), :], (SLAB, LANES))
                 * buf[slot, t * TOP_K + k].astype(F32)
                 for k in range(TOP_K)]
        while len(terms) > 1:
            terms = [a + b for a, b in zip(terms[0::2], terms[1::2])]
        acc_ref[pl.ds(pl.multiple_of(t * SLAB, SLAB), SLAB), :] = terms[0]
        return carry

    @pl.when(i + 1 < n)
    def _():
        lax.fori_loop(0, tc, functools.partial(token, prefetch=True), 0, unroll=8)

    @pl.when(i + 1 == n)
    def _():
        lax.fori_loop(0, tc, functools.partial(token, prefetch=False), 0, unroll=8)

    mod = mod_ref[0]
    sh_f, sc_f, gt_f = mod[3:4], mod[4:5], mod[5:6]
    x1 = x1_ref[...]
    hb = (_ln(x1) * (1.0 + sc_f) + sh_f).astype(BF16)
    mid = _silu(_dot(hb, wsg_ref[...])) * _dot(hb, wsu_ref[...])
    ffn = _dot(mid.astype(BF16), wsd_ref[...])
    ffn = _slabs_to_rows(acc_ref, tc) + ffn
    out_ref[...] = _ln(alpha * x1 + (1.0 + gt_f) * ffn) * ln2g_ref[...] + ln2b_ref[...]


def _combine(dest, x1_flat, mod, wt, ws_gate, ws_up, ws_down, ln2_g, ln2_b, ys, seq, alpha):
    t, d = x1_flat.shape
    tc = min(COMBINE_TOKENS, seq)
    n = t // tc
    per_seq = seq // tc
    const = lambda i: (0, 0)
    return pl.pallas_call(
        functools.partial(_combine_kernel, alpha=alpha),
        out_shape=jax.ShapeDtypeStruct((t, d), F32),
        grid=(n,),
        in_specs=[
            pl.BlockSpec((TOP_K, tc), lambda i: (0, i), memory_space=pltpu.SMEM),
            pl.BlockSpec((TOP_K, tc), lambda i: (0, jnp.minimum(i + 1, n - 1)), memory_space=pltpu.SMEM),
            pl.BlockSpec((tc, d), lambda i: (i, 0)),
            pl.BlockSpec((1, SUBLANES, d), lambda i: (i // per_seq, 0, 0)),
            pl.BlockSpec((tc, LANES), lambda i: (i, 0)),
            pl.BlockSpec(ws_gate.shape, const),
            pl.BlockSpec(ws_up.shape, const),
            pl.BlockSpec(ws_down.shape, const),
            pl.BlockSpec(ln2_g.shape, const),
            pl.BlockSpec(ln2_b.shape, const),
            pl.BlockSpec(memory_space=pl.ANY),
        ],
        out_specs=pl.BlockSpec((tc, d), lambda i: (i, 0)),
        scratch_shapes=[
            pltpu.VMEM((2, tc * TOP_K, SLAB, LANES), BF16),
            pltpu.VMEM((TOP_K, tc, LANES), F32),
            pltpu.VMEM((tc * SLAB, LANES), F32),
            pltpu.SemaphoreType.DMA((2,)),
        ],
        compiler_params=pltpu.CompilerParams(
            dimension_semantics=("arbitrary",), vmem_limit_bytes=VMEM_LIMIT),
        name="combine",
    )(dest, dest, x1_flat, mod, wt, ws_gate, ws_up, ws_down, ln2_g, ln2_b, ys)


def _layer(x, c_pad, lw, alpha):
    bsz, seq, d = x.shape
    t = bsz * seq
    n_exp = lw["w_router"].shape[1]
    h_count = lw["w_if"].shape[1] // 2

    mod = _ada(c_pad, lw["w_ada"], lw["b_ada"].reshape(1, -1))[:bsz].reshape(bsz, 6, d)
    mod = jnp.pad(mod, ((0, 0), (0, SUBLANES - 6), (0, 0)))

    row2 = lambda a: a.reshape(1, -1)
    w_if = lw["w_if"]
    w_if_pad = jnp.zeros((w_if.shape[0], 2 * LANES), F32)
    w_if_pad = w_if_pad.at[:, :h_count].set(w_if[:, :h_count]).at[:, LANES:LANES + h_count].set(w_if[:, h_count:])
    b_if_pad = jnp.zeros((1, 2 * LANES), F32)
    b_if_pad = b_if_pad.at[0, :h_count].set(lw["b_if"][:h_count]).at[0, LANES:LANES + h_count].set(lw["b_if"][h_count:])
    w_rt = lw["w_router"].astype(F32).T
    wr_hi = w_rt.astype(BF16)
    wr_lo = (w_rt - wr_hi.astype(F32)).astype(BF16)
    p = {
        "w_in": lw["w_in"].astype(BF16), "conv_w": lw["conv_w"], "conv_b": row2(lw["conv_b"]),
        "w_q": lw["w_q"].astype(BF16), "w_k": lw["w_k"].astype(BF16), "w_v": lw["w_v"].astype(BF16),
        "w_if": w_if_pad.astype(BF16), "b_if": b_if_pad,
        "mh_g": row2(lw["mh_g"]), "skip": row2(lw["skip"]), "sg_g": row2(lw["sg_g"]), "sg_b": row2(lw["sg_b"]),
        "w_sp": lw["w_sp"], "b_sp_t": lw["b_sp"].T, "w_out": lw["w_out"].astype(BF16),
        "ln1_g": row2(lw["ln1_g"]), "ln1_b": row2(lw["ln1_b"]), "wr_hi": wr_hi, "wr_lo": wr_lo,
    }
    x1, h2, logt = _mixer(x, mod, p, alpha)

    tr = min(ROUTE_TOKENS, t)
    upper = (jnp.arange(tr)[:, None] < jnp.arange(tr)[None, :]).astype(BF16)
    idx, _, rank, wt, cnt = _route(logt, lw["e_bias"].astype(F32).reshape(n_exp, 1), upper)
    counts = cnt[:, 0].astype(I32)
    meta, n_used, pad_start, pad_first, pad_len, n_blocks_max = _expert_blocks(counts, t * TOP_K)
    dest = _dest(idx, rank, pad_start.reshape(n_exp, 1))

    xs = _dispatch(pad_first, pad_len, n_used, dest, h2, n_blocks_max * EXPERT_ROWS)
    ys = _experts(meta, xs, lw["w_gate"], lw["w_up"], lw["w_down"])
    out = _combine(dest, x1.reshape(t, d), mod, wt, lw["ws_gate"].astype(BF16), lw["ws_up"].astype(BF16),
                   lw["ws_down"].astype(BF16), row2(lw["ln2_g"]), row2(lw["ln2_b"]), ys, seq, alpha)
    return out.reshape(bsz, seq, d)


def kernel(x, c, w_ada, b_ada, w_in, conv_w, conv_b, w_q, w_k, w_v, w_if, b_if, mh_g, skip, sg_g, sg_b, w_sp, b_sp, w_out, ln1_g, ln1_b, w_router, e_bias, w_gate, w_up, w_down, ws_gate, ws_up, ws_down, ln2_g, ln2_b):
    stacked = dict(w_ada=w_ada, b_ada=b_ada, w_in=w_in, conv_w=conv_w, conv_b=conv_b, w_q=w_q, w_k=w_k,
                   w_v=w_v, w_if=w_if, b_if=b_if, mh_g=mh_g, skip=skip, sg_g=sg_g, sg_b=sg_b, w_sp=w_sp,
                   b_sp=b_sp, w_out=w_out, ln1_g=ln1_g, ln1_b=ln1_b, w_router=w_router, e_bias=e_bias,
                   w_gate=w_gate, w_up=w_up, w_down=w_down, ws_gate=ws_gate, ws_up=ws_up, ws_down=ws_down,
                   ln2_g=ln2_g, ln2_b=ln2_b)
    depth = w_ada.shape[0]
    alpha = float((2 * depth) ** 0.25)
    bsz = x.shape[0]
    c_pad = jnp.pad(c, ((0, -bsz % SUBLANES), (0, 0)))
    for l in range(depth):
        x = _layer(x, c_pad, {k: v[l] for k, v in stacked.items()}, alpha)
    return x
```

```python
import functools
import math

import jax
import jax.numpy as jnp
from jax import lax
from jax.experimental import pallas as pl
from jax.experimental.pallas import tpu as pltpu

F32 = jnp.float32
BF16 = jnp.bfloat16
I32 = jnp.int32

LN_EPS = 1e-5
M_HEADS = 4
G_GROUPS = 4
CHUNK = 128
N_GROUPS = 8
TOPK_GROUPS = 4
TOP_K = 8
ROUTE_SCALE = 2.5
LANES = 128
SUBLANES = 8

MIXER_TOKENS = 512
MIXER_CHUNK_GROUP = 4
ROUTE_TOKENS = 1024
DEST_TOKENS = 2048
DISPATCH_TOKENS = 2048
EXPERT_ROWS = 512
EXPERT_BLOCKS_PER_STEP = 2
COMBINE_TOKENS = 512
VMEM_LIMIT = 56 * 1024 * 1024

NEG_INF = float("-inf")


def _ln(x):
    mu = jnp.mean(x, axis=-1, keepdims=True)
    xc = x - mu
    var = jnp.mean(xc * xc, axis=-1, keepdims=True)
    return xc * lax.rsqrt(var + LN_EPS)


def _dot(a, b):
    return jnp.dot(a, b, preferred_element_type=F32)


def _dot_nt(a, b):
    return lax.dot_general(a, b, (((1,), (1,)), ((), ())), preferred_element_type=F32)


def _dot_exact(a, b):
    return jnp.dot(a, b, preferred_element_type=F32, precision=lax.Precision.HIGHEST)


def _silu(x):
    return x * jax.nn.sigmoid(x)


def _gelu(x):
    return 0.5 * x * (1.0 + lax.erf(x * math.sqrt(0.5)))


def _log_sigmoid(x):
    return jnp.minimum(x, 0.0) - jnp.log1p(jnp.exp(-jnp.abs(x)))


SLAB = SUBLANES


def _to_slabs(x, scratch):
    rows, d = x.shape
    assert d == SLAB * LANES
    for s in range(SLAB):
        scratch[pl.ds(s, rows, stride=SLAB), :] = x[:, s * LANES:(s + 1) * LANES]
    return scratch[0:rows * SLAB, :].reshape(rows, SLAB, LANES).astype(BF16)


def _slabs_to_rows(scratch, rows):
    return jnp.concatenate([scratch[pl.ds(s, rows, stride=SLAB), :] for s in range(SLAB)], axis=1)


def _from_slabs(x3, scratch):
    rows = x3.shape[0]
    scratch[0:rows * SLAB, :] = x3.astype(F32).reshape(rows * SLAB, LANES)
    return _slabs_to_rows(scratch, rows)


def _ada_kernel(c_ref, w_ref, b_ref, o_ref):
    o_ref[...] = _dot_exact(_silu(c_ref[...]), w_ref[...]) + b_ref[...]


def _ada(c_pad, w_ada, b_ada):
    rows, d = c_pad.shape
    n = w_ada.shape[1]
    return pl.pallas_call(
        _ada_kernel,
        out_shape=jax.ShapeDtypeStruct((rows, n), F32),
        grid=(n // d,),
        in_specs=[
            pl.BlockSpec((rows, d), lambda j: (0, 0)),
            pl.BlockSpec((d, d), lambda j: (0, j)),
            pl.BlockSpec((1, d), lambda j: (0, j)),
        ],
        out_specs=pl.BlockSpec((rows, d), lambda j: (0, j)),
        compiler_params=pltpu.CompilerParams(vmem_limit_bytes=VMEM_LIMIT),
        name="ada",
    )(c_pad, w_ada, b_ada)


def _mixer_kernel(x_ref, mod_ref, w_in_ref, conv_w_ref, conv_b_ref, wq_ref, wk_ref, wv_ref,
                  wif_ref, bif_ref, mhg_ref, skip_ref, sgg_ref, sgb_ref, wsp_ref, bspt_ref,
                  w_out_ref, ln1g_ref, ln1b_ref, wr_hi_ref, wr_lo_ref,
                  x1_ref, h2_ref, logt_ref,
                  state_ref, m_ref, xm_ref, cat_ref, slab_ref, *, alpha):
    tm = x_ref.shape[1]
    mw = conv_w_ref.shape[1]
    dh = mw // M_HEADS
    gw = sgg_ref.shape[1]
    gd = gw // G_GROUPS
    conv_k = conv_w_ref.shape[0]
    nch = tm // CHUNK
    L = CHUNK

    @pl.when(pl.program_id(1) == 0)
    def _():
        state_ref[...] = jnp.zeros_like(state_ref)
        m_ref[...] = jnp.zeros_like(m_ref)
        xm_ref[0:SUBLANES, :] = jnp.zeros((SUBLANES, mw), F32)

    x = x_ref[0]
    mod = mod_ref[0]
    sh_a, sc_a, gt_a = mod[0:1], mod[1:2], mod[2:3]
    sh_f, sc_f = mod[3:4], mod[4:5]

    h = _ln(x) * (1.0 + sc_a) + sh_a
    proj = _dot(h.astype(BF16), w_in_ref[...])
    xm = proj[:, :mw]
    z = proj[:, mw:2 * mw]
    u = proj[:, 2 * mw:2 * mw + gw]
    v = proj[:, 2 * mw + gw:]

    xm_ref[SUBLANES:SUBLANES + tm, :] = xm
    conv = jnp.broadcast_to(conv_b_ref[...], (tm, mw))
    for j in range(conv_k):
        off = SUBLANES - (conv_k - 1) + j
        conv = conv + conv_w_ref[j:j + 1, :] * xm_ref[off:off + tm, :]
    xm_ref[0:SUBLANES, :] = xm_ref[tm:tm + SUBLANES, :]
    xc = _silu(conv)

    scale = dh ** -0.5
    qs, ks, vs = [], [], []
    for hd in range(M_HEADS):
        sl = slice(hd * dh, (hd + 1) * dh)
        xch = xc[:, sl].astype(BF16)
        qs.append(_dot(xch, wq_ref[hd]))
        ks.append(_dot(xch, wk_ref[hd]) * scale)
        vs.append(_dot(xm[:, sl].astype(BF16), wv_ref[hd]))
    qkv = jnp.concatenate(qs + ks + vs, axis=1).astype(BF16)
    gate = _dot(qkv, wif_ref[...]) + bif_ref[...]
    gi = gate[:, :LANES]
    lf = _log_sigmoid(gate[:, LANES:])

    row = lax.broadcasted_iota(I32, (L, L), 0)
    col = lax.broadcasted_iota(I32, (L, L), 1)
    causal = col <= row
    tri = jnp.where(causal, 1.0, 0.0).astype(F32)
    ones_l = jnp.ones((L, dh), F32)
    wsg = [jnp.where(causal, wsp_ref[gg], 0.0).astype(BF16) for gg in range(G_GROUPS)]
    state = [state_ref[hd] for hd in range(M_HEADS)]
    m_run = [m_ref[hd:hd + 1, 0:1] for hd in range(M_HEADS)]

    heads = range(M_HEADS)
    hsl = [slice(hd * dh, (hd + 1) * dh) for hd in heads]
    groups = range(G_GROUPS)
    gsl = [slice(gg * gd, (gg + 1) * gd) for gg in groups]

    for c0 in range(0, nch, MIXER_CHUNK_GROUP):
        chunks = range(c0, min(c0 + MIXER_CHUNK_GROUP, nch))
        rsl = {c: slice(c * L, (c + 1) * L) for c in chunks}
        pairs = [(c, hd) for c in chunks for hd in heads]
        bmat = {c: _dot_exact(tri, lf[rsl[c]]) for c in chunks}
        rmat = {c: gi[rsl[c]] - bmat[c] for c in chunks}
        rmat_t = {c: rmat[c].T for c in chunks}
        bcol = {(c, hd): bmat[c][:, hd:hd + 1] for c, hd in pairs}
        rrow = {(c, hd): rmat_t[c][hd:hd + 1, :] for c, hd in pairs}
        g = {(c, hd): bmat[c][L - 1:L, hd:hd + 1] for c, hd in pairs}
        a = {(c, hd): g[c, hd] + rmat[c][:, hd:hd + 1] for c, hd in pairs}
        a_max = {pr: jnp.max(a[pr], axis=0, keepdims=True) for pr in pairs}

        m_in, m_out = {}, {}
        for c in chunks:
            for hd in heads:
                m_in[c, hd] = m_run[hd]
                m_out[c, hd] = jnp.maximum(g[c, hd] + m_run[hd], a_max[c, hd])
            m_run = [m_out[c, hd] for hd in heads]

        k = {(c, hd): ks[hd][rsl[c]] for c, hd in pairs}
        qb = {(c, hd): qs[hd][rsl[c]].astype(BF16) for c, hd in pairs}
        kb = {pr: k[pr].astype(BF16) for pr in pairs}
        vext = {(c, hd): jnp.concatenate([vs[hd][rsl[c]], ones_l], axis=1).astype(BF16)
                for c, hd in pairs}

        log_d = {pr: jnp.where(causal, bcol[pr] + rrow[pr], NEG_INF) for pr in pairs}
        log_inter = {pr: bcol[pr] + m_in[pr] for pr in pairs}
        m_t = {pr: jnp.maximum(log_inter[pr], jnp.max(log_d[pr], axis=1, keepdims=True)) for pr in pairs}
        qk = {pr: _dot_nt(qb[pr], kb[pr]) for pr in pairs}
        p = {pr: (jnp.exp(log_d[pr] - m_t[pr]) * qk[pr]).astype(BF16) for pr in pairs}
        w_inter = {pr: jnp.exp(log_inter[pr] - m_t[pr]) for pr in pairs}
        intra = {pr: _dot(p[pr], vext[pr]) for pr in pairs}
        decay = {pr: jnp.exp(g[pr] + m_in[pr] - m_out[pr]) for pr in pairs}
        kw_t = {pr: (k[pr] * jnp.exp(a[pr] - m_out[pr])).T.astype(BF16) for pr in pairs}
        update = {pr: _dot(kw_t[pr], vext[pr]) for pr in pairs}

        st_in = {}
        for c in chunks:
            for hd in heads:
                st_in[c, hd] = state[hd]
            state = [decay[c, hd] * state[hd] + update[c, hd] for hd in heads]

        inter = {pr: _dot(qb[pr], st_in[pr].astype(BF16)) for pr in pairs}
        out_ext = {pr: w_inter[pr] * inter[pr] + intra[pr] for pr in pairs}
        hh = {pr: out_ext[pr][:, :dh] / jnp.maximum(jnp.abs(out_ext[pr][:, dh:]), jnp.exp(-m_t[pr]))
              for pr in pairs}
        hc = {(c, hd): _ln(hh[c, hd]) * mhg_ref[:, hsl[hd]] for c, hd in pairs}
        for c, hd in pairs:
            hm = jax.nn.sigmoid(z[rsl[c], hsl[hd]]) * (hc[c, hd] + skip_ref[:, hsl[hd]] * xc[rsl[c], hsl[hd]])
            cat_ref[rsl[c], hsl[hd]] = hm.astype(BF16)

        for c in chunks:
            ug = _gelu(u[rsl[c]])
            vg = _gelu(v[rsl[c]])
            vn = [(_ln(vg[:, gsl[gg]]) * sgg_ref[:, gsl[gg]] + sgb_ref[:, gsl[gg]]).astype(BF16) for gg in groups]
            sp = [_dot(wsg[gg], vn[gg]) + bspt_ref[:, gg:gg + 1] for gg in groups]
            for gg in groups:
                cat_ref[rsl[c], mw + gg * gd:mw + (gg + 1) * gd] = (ug[:, gsl[gg]] * sp[gg]).astype(BF16)

    for hd in range(M_HEADS):
        state_ref[hd] = state[hd]
        m_ref[hd:hd + 1, :] = jnp.broadcast_to(m_run[hd], (1, LANES))

    mix = _dot(cat_ref[...], w_out_ref[...])
    x1 = _ln(alpha * x + (1.0 + gt_a) * mix) * ln1g_ref[...] + ln1b_ref[...]
    x1_ref[0] = x1
    h2 = _ln(x1) * (1.0 + sc_f) + sh_f
    h2_ref[...] = _to_slabs(h2, slab_ref)
    h_hi = h2.astype(BF16)
    h_lo = (h2 - h_hi.astype(F32)).astype(BF16)
    wr_hi = wr_hi_ref[...]
    logt_ref[...] = _dot_nt(wr_hi, h_hi) + _dot_nt(wr_hi, h_lo) + _dot_nt(wr_lo_ref[...], h_hi)


def _mixer(x, mod, p, alpha):
    bsz, seq, d = x.shape
    tm = min(MIXER_TOKENS, seq)
    nt = seq // tm
    mw = p["conv_w"].shape[1]
    gw = p["sg_g"].shape[1]
    dh = mw // M_HEADS
    n_exp = p["wr_hi"].shape[0]
    const2 = lambda b, i: (0, 0)
    const3 = lambda b, i: (0, 0, 0)
    full = lambda a: pl.BlockSpec(a.shape, const2 if a.ndim == 2 else const3)
    names = ["w_in", "conv_w", "conv_b", "w_q", "w_k", "w_v", "w_if", "b_if", "mh_g", "skip", "sg_g",
             "sg_b", "w_sp", "b_sp_t", "w_out", "ln1_g", "ln1_b", "wr_hi", "wr_lo"]
    weights = [p[n] for n in names]
    return pl.pallas_call(
        functools.partial(_mixer_kernel, alpha=alpha),
        out_shape=(
            jax.ShapeDtypeStruct((bsz, seq, d), F32),
            jax.ShapeDtypeStruct((bsz * seq, SLAB, LANES), BF16),
            jax.ShapeDtypeStruct((n_exp, bsz * seq), F32),
        ),
        grid=(bsz, nt),
        in_specs=[
            pl.BlockSpec((1, tm, d), lambda b, i: (b, i, 0)),
            pl.BlockSpec((1, SUBLANES, d), lambda b, i: (b, 0, 0)),
        ] + [full(w) for w in weights],
        out_specs=(
            pl.BlockSpec((1, tm, d), lambda b, i: (b, i, 0)),
            pl.BlockSpec((tm, SLAB, LANES), lambda b, i: (b * nt + i, 0, 0)),
            pl.BlockSpec((n_exp, tm), lambda b, i: (0, b * nt + i)),
        ),
        scratch_shapes=[
            pltpu.VMEM((M_HEADS, dh, 2 * dh), F32),
            pltpu.VMEM((SUBLANES, LANES), F32),
            pltpu.VMEM((tm + SUBLANES, mw), F32),
            pltpu.VMEM((tm, mw + gw), BF16),
            pltpu.VMEM((tm * SLAB, LANES), F32),
        ],
        compiler_params=pltpu.CompilerParams(
            dimension_semantics=("arbitrary", "arbitrary"), vmem_limit_bytes=VMEM_LIMIT),
        name="mixer",
    )(x, mod, *weights)


def _route_kernel(logt_ref, bias_ref, upper_ref, idx_ref, w_ref, rank_ref, wt_ref, cnt_ref, carry_ref):
    n_exp, tr = logt_ref.shape
    epg = n_exp // N_GROUPS

    @pl.when(pl.program_id(0) == 0)
    def _():
        carry_ref[...] = jnp.zeros_like(carry_ref)

    scores = jax.nn.sigmoid(logt_ref[...])
    sel = scores + bias_ref[...]

    sel3 = sel.reshape(N_GROUPS, epg, tr)
    io3 = lax.broadcasted_iota(I32, (N_GROUPS, epg, tr), 1)
    m1 = jnp.max(sel3, axis=1, keepdims=True)
    first = jnp.min(jnp.where(sel3 == m1, io3, epg), axis=1, keepdims=True)
    m2 = jnp.max(jnp.where(io3 == first, NEG_INF, sel3), axis=1, keepdims=True)
    gs = (m1 + m2).reshape(N_GROUPS, tr)

    gio = lax.broadcasted_iota(I32, (N_GROUPS, tr), 0)
    gmask = jnp.zeros((N_GROUPS, tr), F32)
    for _ in range(TOPK_GROUPS):
        m = jnp.max(gs, axis=0, keepdims=True)
        gi = jnp.min(jnp.where(gs == m, gio, N_GROUPS), axis=0, keepdims=True)
        hit = gio == gi
        gmask = jnp.where(hit, 1.0, gmask)
        gs = jnp.where(hit, NEG_INF, gs)
    emask = jnp.broadcast_to(gmask.reshape(N_GROUPS, 1, tr), (N_GROUPS, epg, tr)).reshape(n_exp, tr)
    selm = jnp.where(emask > 0.0, sel, NEG_INF)

    eio = lax.broadcasted_iota(I32, (n_exp, tr), 0)
    chosen = jnp.zeros((n_exp, tr), F32)
    idx_rows, w_rows = [], []
    for _ in range(TOP_K):
        m = jnp.max(selm, axis=0, keepdims=True)
        ei = jnp.min(jnp.where(selm == m, eio, n_exp), axis=0, keepdims=True)
        hit = eio == ei
        w_rows.append(jnp.sum(jnp.where(hit, scores, 0.0), axis=0, keepdims=True))
        idx_rows.append(ei)
        selm = jnp.where(hit, NEG_INF, selm)
        chosen = jnp.where(hit, 1.0, chosen)

    chosen_b = chosen.astype(BF16)
    carry = carry_ref[...]
    ranks = carry[:, 0:1] + _dot(chosen_b, upper_ref[...])
    carry_new = carry + _dot(chosen_b, jnp.ones((tr, LANES), BF16))
    carry_ref[...] = carry_new
    cnt_ref[...] = carry_new
    rank_rows = [jnp.sum(jnp.where(eio == ei, ranks, 0.0), axis=0, keepdims=True) for ei in idx_rows]

    wsum = w_rows[0]
    for wk in w_rows[1:]:
        wsum = wsum + wk
    w8 = jnp.concatenate([wk / wsum * ROUTE_SCALE for wk in w_rows], axis=0)
    idx_ref[...] = jnp.concatenate(idx_rows, axis=0)
    rank_ref[...] = jnp.concatenate(rank_rows, axis=0).astype(I32)
    w_ref[...] = w8
    wpad = jnp.concatenate([w8, jnp.zeros((LANES - TOP_K, tr), F32)], axis=0)
    wt_ref[...] = wpad.T


def _route(logt, e_bias_col, upper):
    n_exp, t = logt.shape
    tr = upper.shape[0]
    return pl.pallas_call(
        _route_kernel,
        out_shape=(
            jax.ShapeDtypeStruct((TOP_K, t), I32),
            jax.ShapeDtypeStruct((TOP_K, t), F32),
            jax.ShapeDtypeStruct((TOP_K, t), I32),
            jax.ShapeDtypeStruct((t, LANES), F32),
            jax.ShapeDtypeStruct((n_exp, LANES), F32),
        ),
        grid=(t // tr,),
        in_specs=[
            pl.BlockSpec((n_exp, tr), lambda i: (0, i)),
            pl.BlockSpec((n_exp, 1), lambda i: (0, 0)),
            pl.BlockSpec((tr, tr), lambda i: (0, 0)),
        ],
        out_specs=(
            pl.BlockSpec((TOP_K, tr), lambda i: (0, i)),
            pl.BlockSpec((TOP_K, tr), lambda i: (0, i)),
            pl.BlockSpec((TOP_K, tr), lambda i: (0, i)),
            pl.BlockSpec((tr, LANES), lambda i: (i, 0)),
            pl.BlockSpec((n_exp, LANES), lambda i: (0, 0)),
        ),
        scratch_shapes=[pltpu.VMEM((n_exp, LANES), F32)],
        compiler_params=pltpu.CompilerParams(
            dimension_semantics=("arbitrary",), vmem_limit_bytes=VMEM_LIMIT),
        name="route",
    )(logt, e_bias_col, upper)


def _dest_kernel(idx_ref, rank_ref, start_ref, dest_ref):
    n_exp = start_ref.shape[0]
    tt = idx_ref.shape[1]
    eio = lax.broadcasted_iota(I32, (n_exp, tt), 0)
    start = start_ref[...]
    rows = []
    for k in range(TOP_K):
        hit = eio == idx_ref[k:k + 1, :]
        rows.append(jnp.sum(jnp.where(hit, start, 0), axis=0, keepdims=True) + rank_ref[k:k + 1, :])
    dest_ref[...] = jnp.concatenate(rows, axis=0)


def _dest(idx, rank, start_col):
    t = idx.shape[1]
    tt = min(DEST_TOKENS, t)
    n_exp = start_col.shape[0]
    return pl.pallas_call(
        _dest_kernel,
        out_shape=jax.ShapeDtypeStruct((TOP_K, t), I32),
        grid=(t // tt,),
        in_specs=[
            pl.BlockSpec((TOP_K, tt), lambda i: (0, i)),
            pl.BlockSpec((TOP_K, tt), lambda i: (0, i)),
            pl.BlockSpec((n_exp, 1), lambda i: (0, 0)),
        ],
        out_specs=pl.BlockSpec((TOP_K, tt), lambda i: (0, i)),
        compiler_params=pltpu.CompilerParams(vmem_limit_bytes=VMEM_LIMIT),
        name="dest",
    )(idx, rank, start_col)


def _slabs_copy(hbm, sem, n_slabs):
    return pltpu.make_async_copy(hbm.at[pl.ds(0, n_slabs)], hbm.at[pl.ds(0, n_slabs)], sem)


def _dispatch_kernel(pad_first, pad_len, n_used, dest_ref, h_ref, xs_hbm, zeros_ref, sem):
    td = dest_ref.shape[1]
    n_exp = pad_first.shape[0]
    n_blocks_max = xs_hbm.shape[0] // EXPERT_ROWS
    step = pl.program_id(0)

    def zero_fill(act):
        def zeros_to(first_row, n_rows):
            act(pltpu.make_async_copy(zeros_ref.at[pl.ds(0, n_rows)], xs_hbm.at[pl.ds(first_row, n_rows)],
                                      sem.at[1]))

        def expert_padding(e, carry):
            row = pad_first[e]
            for bit in reversed(range(EXPERT_ROWS.bit_length() - 1)):
                take = (pad_len[e] >> bit) & 1
                pl.when(take == 1)(functools.partial(zeros_to, row, 1 << bit))
                row = row + (take << bit)
            return carry

        def unused_block(b, carry):
            zeros_to(b * EXPERT_ROWS, EXPERT_ROWS)
            return carry

        lax.fori_loop(0, n_exp, expert_padding, 0)
        lax.fori_loop(n_used[0], n_blocks_max, unused_block, 0)

    @pl.when(step == 0)
    def _():
        zeros_ref[...] = jnp.zeros_like(zeros_ref)
        zero_fill(lambda cp: cp.start())

    def body(t, carry):
        for k in range(TOP_K):
            pltpu.make_async_copy(h_ref.at[t], xs_hbm.at[dest_ref[k, t]], sem.at[0]).start(priority=k % 2)
        return carry

    lax.fori_loop(0, td, body, 0, unroll=2)
    _slabs_copy(xs_hbm, sem.at[0], td * TOP_K).wait()

    @pl.when(step == pl.num_programs(0) - 1)
    def _():
        zero_fill(lambda cp: cp.wait())


def _dispatch(pad_first, pad_len, n_used, dest, h_slabs, n_rows):
    t = h_slabs.shape[0]
    td = min(DISPATCH_TOKENS, t)
    grid_spec = pltpu.PrefetchScalarGridSpec(
        num_scalar_prefetch=3,
        grid=(t // td,),
        in_specs=[
            pl.BlockSpec((TOP_K, td), lambda i, pf, pn, nu: (0, i), memory_space=pltpu.SMEM),
            pl.BlockSpec((td, SLAB, LANES), lambda i, pf, pn, nu: (i, 0, 0)),
        ],
        out_specs=pl.BlockSpec(memory_space=pl.ANY),
        scratch_shapes=[
            pltpu.VMEM((EXPERT_ROWS, SLAB, LANES), h_slabs.dtype),
            pltpu.SemaphoreType.DMA((2,)),
        ],
    )
    return pl.pallas_call(
        _dispatch_kernel,
        out_shape=jax.ShapeDtypeStruct((n_rows, SLAB, LANES), h_slabs.dtype),
        grid_spec=grid_spec,
        compiler_params=pltpu.CompilerParams(
            dimension_semantics=("arbitrary",), vmem_limit_bytes=VMEM_LIMIT),
        name="dispatch",
    )(pad_first, pad_len, n_used, dest, h_slabs)


def _experts_kernel(block_expert, block_new_expert, block_slot, block_next_expert, n_blocks,
                    xs_ref, wg_hbm, wu_hbm, wd_hbm, ys_ref,
                    wg_f, wu_f, wd_f, wg_b, wu_b, wd_b, slab_ref, sem):
    step = pl.program_id(0)

    def weight_copies(e, slot):
        return [pltpu.make_async_copy(hbm.at[e], buf.at[slot], sem.at[slot])
                for hbm, buf in ((wg_hbm, wg_f), (wu_hbm, wu_f), (wd_hbm, wd_f))]

    def block(i, rows):
        @pl.when(block_new_expert[i] == 1)
        def _():
            slot = block_slot[i]

            @pl.when(i == 0)
            def _():
                for cp in weight_copies(block_expert[i], slot):
                    cp.start()

            for cp in weight_copies(block_expert[i], slot):
                cp.wait()
            wg_b[...] = wg_f[slot].astype(BF16)
            wu_b[...] = wu_f[slot].astype(BF16)
            wd_b[...] = wd_f[slot].astype(BF16)

            @pl.when(block_next_expert[i] >= 0)
            def _():
                for cp in weight_copies(block_next_expert[i], 1 - slot):
                    cp.start()

        xb = _from_slabs(xs_ref[rows], slab_ref).astype(BF16)
        g = _dot(xb, wg_b[...])
        u = _dot(xb, wu_b[...])
        y = _dot((_silu(g) * u).astype(BF16), wd_b[...])
        ys_ref[rows] = _to_slabs(y, slab_ref)

    for j in range(EXPERT_BLOCKS_PER_STEP):
        i = step * EXPERT_BLOCKS_PER_STEP + j
        rows = slice(j * EXPERT_ROWS, (j + 1) * EXPERT_ROWS)
        pl.when(i < n_blocks[0])(functools.partial(block, i, rows))

        @pl.when((i >= n_blocks[0]) & (step * EXPERT_BLOCKS_PER_STEP < n_blocks[0]))
        def _():
            ys_ref[rows] = xs_ref[rows]


def _experts(meta, xs, w_gate, w_up, w_down):
    n_scalar = len(meta)
    _, d, de = w_gate.shape
    n_max = meta[0].shape[0]
    per_step = EXPERT_BLOCKS_PER_STEP
    step_rows = per_step * EXPERT_ROWS
    blk = lambda i, be, ne, sl, nx, nb: (jnp.minimum(i, (nb[0] - 1) // per_step), 0, 0)
    grid_spec = pltpu.PrefetchScalarGridSpec(
        num_scalar_prefetch=n_scalar,
        grid=(n_max // per_step,),
        in_specs=[
            pl.BlockSpec((step_rows, SLAB, LANES), blk),
            pl.BlockSpec(memory_space=pl.ANY),
            pl.BlockSpec(memory_space=pl.ANY),
            pl.BlockSpec(memory_space=pl.ANY),
        ],
        out_specs=pl.BlockSpec((step_rows, SLAB, LANES), blk),
        scratch_shapes=[
            pltpu.VMEM((2, d, de), F32),
            pltpu.VMEM((2, d, de), F32),
            pltpu.VMEM((2, de, d), F32),
            pltpu.VMEM((d, de), BF16),
            pltpu.VMEM((d, de), BF16),
            pltpu.VMEM((de, d), BF16),
            pltpu.VMEM((EXPERT_ROWS * SLAB, LANES), F32),
            pltpu.SemaphoreType.DMA((2,)),
        ],
    )
    return pl.pallas_call(
        _experts_kernel,
        out_shape=jax.ShapeDtypeStruct(xs.shape, xs.dtype),
        grid_spec=grid_spec,
        input_output_aliases={n_scalar: 0},
        compiler_params=pltpu.CompilerParams(
            dimension_semantics=("arbitrary",), vmem_limit_bytes=VMEM_LIMIT),
        name="experts",
    )(*meta, xs, w_gate, w_up, w_down)


def _expert_blocks(counts, n_assign):
    n_exp = counts.shape[0]
    n_max = (n_assign + n_exp * (EXPERT_ROWS - 1)) // EXPERT_ROWS + 1
    n_max += -n_max % EXPERT_BLOCKS_PER_STEP
    padded = (counts + EXPERT_ROWS - 1) // EXPERT_ROWS * EXPERT_ROWS
    pad_end = jnp.cumsum(padded)
    pad_start = pad_end - padded
    n_used = pad_end[-1] // EXPERT_ROWS
    first_row = jnp.minimum(jnp.arange(n_max, dtype=I32), n_used - 1) * EXPERT_ROWS
    e_of = jnp.sum((first_row[:, None] >= pad_end[None, :]).astype(I32), axis=1)
    e_of = jnp.minimum(e_of, n_exp - 1).astype(I32)
    prev_e = jnp.concatenate([jnp.full((1,), -1, I32), e_of[:-1]])
    new_expert = (e_of != prev_e).astype(I32)
    slot = ((jnp.cumsum(new_expert) - 1) & 1).astype(I32)
    ids = jnp.arange(n_exp, dtype=I32)
    later_used = (ids[None, :] > ids[:, None]) & (padded[None, :] > 0)
    next_used = jnp.min(jnp.where(later_used, ids[None, :], n_exp), axis=1)
    next_used = jnp.where(next_used < n_exp, next_used, -1)
    next_expert = jnp.sum(jnp.where(e_of[:, None] == ids[None, :], next_used[None, :], 0), axis=1).astype(I32)
    n_used = n_used.reshape(1).astype(I32)
    meta = (e_of, new_expert, slot, next_expert, n_used)
    return (meta, n_used, pad_start.astype(I32), (pad_start + counts).astype(I32),
            (padded - counts).astype(I32), n_max)


def _combine_kernel(dest_cur, dest_nxt, x1_ref, mod_ref, wt_ref, wsg_ref, wsu_ref, wsd_ref,
                    ln2g_ref, ln2b_ref, ys_hbm, out_ref, buf, wb_ref, acc_ref, sem, *, alpha):
    tc = x1_ref.shape[0]
    i = pl.program_id(0)
    n = pl.num_programs(0)
    slot = i % 2

    def request(dest_ref, s, t):
        for k in range(TOP_K):
            pltpu.make_async_copy(ys_hbm.at[dest_ref[k, t]], buf.at[s, t * TOP_K + k],
                                  sem.at[s]).start(priority=k % 2)

    @pl.when(i == 0)
    def _():
        def body(t, carry):
            request(dest_cur, 0, t)
            return carry
        lax.fori_loop(0, tc, body, 0)

    pltpu.make_async_copy(ys_hbm.at[pl.ds(0, tc * TOP_K)], buf.at[slot], sem.at[slot]).wait()

    wt = wt_ref[...]
    for k in range(TOP_K):
        wb_ref[k] = jnp.broadcast_to(wt[:, k:k + 1], (tc, LANES))

    def token(t, carry, prefetch):
        if prefetch:
            request(dest_nxt, 1 - slot, t)
        terms = [jnp.broadcast_to(wb_ref[k, pl.ds(t, 1), :], (SLAB, LANES))
                 * buf[slot, t * TOP_K + k].astype(F32)
                 for k in range(TOP_K)]
        while len(terms) > 1:
            terms = [a + b for a, b in zip(terms[0::2], terms[1::2])]
        acc_ref[pl.ds(pl.multiple_of(t * SLAB, SLAB), SLAB), :] = terms[0]
        return carry

    @pl.when(i + 1 < n)
    def _():
        lax.fori_loop(0, tc, functools.partial(token, prefetch=True), 0, unroll=8)

    @pl.when(i + 1 == n)
    def _():
        lax.fori_loop(0, tc, functools.partial(token, prefetch=False), 0, unroll=8)

    mod = mod_ref[0]
    sh_f, sc_f, gt_f = mod[3:4], mod[4:5], mod[5:6]
    x1 = x1_ref[...]
    hb = (_ln(x1) * (1.0 + sc_f) + sh_f).astype(BF16)
    mid = _silu(_dot(hb, wsg_ref[...])) * _dot(hb, wsu_ref[...])
    ffn = _dot(mid.astype(BF16), wsd_ref[...])
    ffn = _slabs_to_rows(acc_ref, tc) + ffn
    out_ref[...] = _ln(alpha * x1 + (1.0 + gt_f) * ffn) * ln2g_ref[...] + ln2b_ref[...]


def _combine(dest, x1_flat, mod, wt, ws_gate, ws_up, ws_down, ln2_g, ln2_b, ys, seq, alpha):
    t, d = x1_flat.shape
    tc = min(COMBINE_TOKENS, seq)
    n = t // tc
    per_seq = seq // tc
    const = lambda i: (0, 0)
    return pl.pallas_call(
        functools.partial(_combine_kernel, alpha=alpha),
        out_shape=jax.ShapeDtypeStruct((t, d), F32),
        grid=(n,),
        in_specs=[
            pl.BlockSpec((TOP_K, tc), lambda i: (0, i), memory_space=pltpu.SMEM),
            pl.BlockSpec((TOP_K, tc), lambda i: (0, jnp.minimum(i + 1, n - 1)), memory_space=pltpu.SMEM),
            pl.BlockSpec((tc, d), lambda i: (i, 0)),
            pl.BlockSpec((1, SUBLANES, d), lambda i: (i // per_seq, 0, 0)),
            pl.BlockSpec((tc, LANES), lambda i: (i, 0)),
            pl.BlockSpec(ws_gate.shape, const),
            pl.BlockSpec(ws_up.shape, const),
            pl.BlockSpec(ws_down.shape, const),
            pl.BlockSpec(ln2_g.shape, const),
            pl.BlockSpec(ln2_b.shape, const),
            pl.BlockSpec(memory_space=pl.ANY),
        ],
        out_specs=pl.BlockSpec((tc, d), lambda i: (i, 0)),
        scratch_shapes=[
            pltpu.VMEM((2, tc * TOP_K, SLAB, LANES), BF16),
            pltpu.VMEM((TOP_K, tc, LANES), F32),
            pltpu.VMEM((tc * SLAB, LANES), F32),
            pltpu.SemaphoreType.DMA((2,)),
        ],
        compiler_params=pltpu.CompilerParams(
            dimension_semantics=("arbitrary",), vmem_limit_bytes=VMEM_LIMIT),
        name="combine",
    )(dest, dest, x1_flat, mod, wt, ws_gate, ws_up, ws_down, ln2_g, ln2_b, ys)


def _layer(x, c_pad, lw, alpha):
    bsz, seq, d = x.shape
    t = bsz * seq
    n_exp = lw["w_router"].shape[1]
    h_count = lw["w_if"].shape[1] // 2

    mod = _ada(c_pad, lw["w_ada"], lw["b_ada"].reshape(1, -1))[:bsz].reshape(bsz, 6, d)
    mod = jnp.pad(mod, ((0, 0), (0, SUBLANES - 6), (0, 0)))

    row2 = lambda a: a.reshape(1, -1)
    w_if = lw["w_if"]
    w_if_pad = jnp.zeros((w_if.shape[0], 2 * LANES), F32)
    w_if_pad = w_if_pad.at[:, :h_count].set(w_if[:, :h_count]).at[:, LANES:LANES + h_count].set(w_if[:, h_count:])
    b_if_pad = jnp.zeros((1, 2 * LANES), F32)
    b_if_pad = b_if_pad.at[0, :h_count].set(lw["b_if"][:h_count]).at[0, LANES:LANES + h_count].set(lw["b_if"][h_count:])
    w_rt = lw["w_router"].astype(F32).T
    wr_hi = w_rt.astype(BF16)
    wr_lo = (w_rt - wr_hi.astype(F32)).astype(BF16)
    p = {
        "w_in": lw["w_in"].astype(BF16), "conv_w": lw["conv_w"], "conv_b": row2(lw["conv_b"]),
        "w_q": lw["w_q"].astype(BF16), "w_k": lw["w_k"].astype(BF16), "w_v": lw["w_v"].astype(BF16),
        "w_if": w_if_pad.astype(BF16), "b_if": b_if_pad,
        "mh_g": row2(lw["mh_g"]), "skip": row2(lw["skip"]), "sg_g": row2(lw["sg_g"]), "sg_b": row2(lw["sg_b"]),
        "w_sp": lw["w_sp"], "b_sp_t": lw["b_sp"].T, "w_out": lw["w_out"].astype(BF16),
        "ln1_g": row2(lw["ln1_g"]), "ln1_b": row2(lw["ln1_b"]), "wr_hi": wr_hi, "wr_lo": wr_lo,
    }
    x1, h2, logt = _mixer(x, mod, p, alpha)

    tr = min(ROUTE_TOKENS, t)
    upper = (jnp.arange(tr)[:, None] < jnp.arange(tr)[None, :]).astype(BF16)
    idx, _, rank, wt, cnt = _route(logt, lw["e_bias"].astype(F32).reshape(n_exp, 1), upper)
    counts = cnt[:, 0].astype(I32)
    meta, n_used, pad_start, pad_first, pad_len, n_blocks_max = _expert_blocks(counts, t * TOP_K)
    dest = _dest(idx, rank, pad_start.reshape(n_exp, 1))

    xs = _dispatch(pad_first, pad_len, n_used, dest, h2, n_blocks_max * EXPERT_ROWS)
    ys = _experts(meta, xs, lw["w_gate"], lw["w_up"], lw["w_down"])
    out = _combine(dest, x1.reshape(t, d), mod, wt, lw["ws_gate"].astype(BF16), lw["ws_up"].astype(BF16),
                   lw["ws_down"].astype(BF16), row2(lw["ln2_g"]), row2(lw["ln2_b"]), ys, seq, alpha)
    return out.reshape(bsz, seq, d)


def kernel(x, c, w_ada, b_ada, w_in, conv_w, conv_b, w_q, w_k, w_v, w_if, b_if, mh_g, skip, sg_g, sg_b, w_sp, b_sp, w_out, ln1_g, ln1_b, w_router, e_bias, w_gate, w_up, w_down, ws_gate, ws_up, ws_down, ln2_g, ln2_b):
    stacked = dict(w_ada=w_ada, b_ada=b_ada, w_in=w_in, conv_w=conv_w, conv_b=conv_b, w_q=w_q, w_k=w_k,
                   w_v=w_v, w_if=w_if, b_if=b_if, mh_g=mh_g, skip=skip, sg_g=sg_g, sg_b=sg_b, w_sp=w_sp,
                   b_sp=b_sp, w_out=w_out, ln1_g=ln1_g, ln1_b=ln1_b, w_router=w_router, e_bias=e_bias,
                   w_gate=w_gate, w_up=w_up, w_down=w_down, ws_gate=ws_gate, ws_up=ws_up, ws_down=ws_down,
                   ln2_g=ln2_g, ln2_b=ln2_b)
    depth = w_ada.shape[0]
    alpha = float((2 * depth) ** 0.25)
    bsz = x.shape[0]
    c_pad = jnp.pad(c, ((0, -bsz % SUBLANES), (0, 0)))
    for l in range(depth):
        x = _layer(x, c_pad, {k: v[l] for k, v in stacked.items()}, alpha)
    return x
```

```python
import functools
import math

import jax
import jax.numpy as jnp
from jax import lax
from jax.experimental import pallas as pl
from jax.experimental.pallas import tpu as pltpu

F32 = jnp.float32
BF16 = jnp.bfloat16
I32 = jnp.int32

LN_EPS = 1e-5
M_HEADS = 4
G_GROUPS = 4
CHUNK = 128
N_GROUPS = 8
TOPK_GROUPS = 4
TOP_K = 8
ROUTE_SCALE = 2.5
LANES = 128
SUBLANES = 8

MIXER_TOKENS = 512
MIXER_CHUNK_GROUP = 4
ROUTE_TOKENS = 1024
DEST_TOKENS = 2048
DISPATCH_TOKENS = 2048
EXPERT_ROWS = 512
EXPERT_BLOCKS_PER_STEP = 4
COMBINE_TOKENS = 512
VMEM_LIMIT = 56 * 1024 * 1024

NEG_INF = float("-inf")


def _ln(x):
    mu = jnp.mean(x, axis=-1, keepdims=True)
    xc = x - mu
    var = jnp.mean(xc * xc, axis=-1, keepdims=True)
    return xc * lax.rsqrt(var + LN_EPS)


def _dot(a, b):
    return jnp.dot(a, b, preferred_element_type=F32)


def _dot_nt(a, b):
    return lax.dot_general(a, b, (((1,), (1,)), ((), ())), preferred_element_type=F32)


def _dot_exact(a, b):
    return jnp.dot(a, b, preferred_element_type=F32, precision=lax.Precision.HIGHEST)


def _silu(x):
    return x * jax.nn.sigmoid(x)


def _gelu(x):
    return 0.5 * x * (1.0 + lax.erf(x * math.sqrt(0.5)))


def _log_sigmoid(x):
    return jnp.minimum(x, 0.0) - jnp.log1p(jnp.exp(-jnp.abs(x)))


SLAB = SUBLANES


def _to_slabs(x, scratch):
    rows, d = x.shape
    assert d == SLAB * LANES
    for s in range(SLAB):
        scratch[pl.ds(s, rows, stride=SLAB), :] = x[:, s * LANES:(s + 1) * LANES]
    return scratch[0:rows * SLAB, :].reshape(rows, SLAB, LANES).astype(BF16)


def _slabs_to_rows(scratch, rows):
    return jnp.concatenate([scratch[pl.ds(s, rows, stride=SLAB), :] for s in range(SLAB)], axis=1)


def _from_slabs(x3, scratch):
    rows = x3.shape[0]
    scratch[0:rows * SLAB, :] = x3.astype(F32).reshape(rows * SLAB, LANES)
    return _slabs_to_rows(scratch, rows)


def _ada_kernel(c_ref, w_ref, b_ref, o_ref):
    o_ref[...] = _dot_exact(_silu(c_ref[...]), w_ref[...]) + b_ref[...]


def _ada(c_pad, w_ada, b_ada):
    rows, d = c_pad.shape
    n = w_ada.shape[1]
    return pl.pallas_call(
        _ada_kernel,
        out_shape=jax.ShapeDtypeStruct((rows, n), F32),
        grid=(n // d,),
        in_specs=[
            pl.BlockSpec((rows, d), lambda j: (0, 0)),
            pl.BlockSpec((d, d), lambda j: (0, j)),
            pl.BlockSpec((1, d), lambda j: (0, j)),
        ],
        out_specs=pl.BlockSpec((rows, d), lambda j: (0, j)),
        compiler_params=pltpu.CompilerParams(vmem_limit_bytes=VMEM_LIMIT),
        name="ada",
    )(c_pad, w_ada, b_ada)


def _mixer_kernel(x_ref, mod_ref, w_in_ref, conv_w_ref, conv_b_ref, wq_ref, wk_ref, wv_ref,
                  wif_ref, bif_ref, mhg_ref, skip_ref, sgg_ref, sgb_ref, wsp_ref, bspt_ref,
                  w_out_ref, ln1g_ref, ln1b_ref, wr_hi_ref, wr_lo_ref,
                  x1_ref, h2_ref, logt_ref,
                  state_ref, m_ref, xm_ref, cat_ref, slab_ref, *, alpha):
    tm = x_ref.shape[1]
    mw = conv_w_ref.shape[1]
    dh = mw // M_HEADS
    gw = sgg_ref.shape[1]
    gd = gw // G_GROUPS
    conv_k = conv_w_ref.shape[0]
    nch = tm // CHUNK
    L = CHUNK

    @pl.when(pl.program_id(1) == 0)
    def _():
        state_ref[...] = jnp.zeros_like(state_ref)
        m_ref[...] = jnp.zeros_like(m_ref)
        xm_ref[0:SUBLANES, :] = jnp.zeros((SUBLANES, mw), F32)

    x = x_ref[0]
    mod = mod_ref[0]
    sh_a, sc_a, gt_a = mod[0:1], mod[1:2], mod[2:3]
    sh_f, sc_f = mod[3:4], mod[4:5]

    h = _ln(x) * (1.0 + sc_a) + sh_a
    proj = _dot(h.astype(BF16), w_in_ref[...])
    xm = proj[:, :mw]
    z = proj[:, mw:2 * mw]
    u = proj[:, 2 * mw:2 * mw + gw]
    v = proj[:, 2 * mw + gw:]

    xm_ref[SUBLANES:SUBLANES + tm, :] = xm
    conv = jnp.broadcast_to(conv_b_ref[...], (tm, mw))
    for j in range(conv_k):
        off = SUBLANES - (conv_k - 1) + j
        conv = conv + conv_w_ref[j:j + 1, :] * xm_ref[off:off + tm, :]
    xm_ref[0:SUBLANES, :] = xm_ref[tm:tm + SUBLANES, :]
    xc = _silu(conv)

    scale = dh ** -0.5
    qs, ks, vs = [], [], []
    for hd in range(M_HEADS):
        sl = slice(hd * dh, (hd + 1) * dh)
        xch = xc[:, sl].astype(BF16)
        qs.append(_dot(xch, wq_ref[hd]))
        ks.append(_dot(xch, wk_ref[hd]) * scale)
        vs.append(_dot(xm[:, sl].astype(BF16), wv_ref[hd]))
    qkv = jnp.concatenate(qs + ks + vs, axis=1).astype(BF16)
    gate = _dot(qkv, wif_ref[...]) + bif_ref[...]
    gi = gate[:, :LANES]
    lf = _log_sigmoid(gate[:, LANES:])

    row = lax.broadcasted_iota(I32, (L, L), 0)
    col = lax.broadcasted_iota(I32, (L, L), 1)
    causal = col <= row
    tri = jnp.where(causal, 1.0, 0.0).astype(F32)
    ones_l = jnp.ones((L, dh), F32)
    wsg = [jnp.where(causal, wsp_ref[gg], 0.0).astype(BF16) for gg in range(G_GROUPS)]
    state = [state_ref[hd] for hd in range(M_HEADS)]
    m_run = [m_ref[hd:hd + 1, 0:1] for hd in range(M_HEADS)]

    heads = range(M_HEADS)
    hsl = [slice(hd * dh, (hd + 1) * dh) for hd in heads]
    groups = range(G_GROUPS)
    gsl = [slice(gg * gd, (gg + 1) * gd) for gg in groups]

    for c0 in range(0, nch, MIXER_CHUNK_GROUP):
        chunks = range(c0, min(c0 + MIXER_CHUNK_GROUP, nch))
        rsl = {c: slice(c * L, (c + 1) * L) for c in chunks}
        pairs = [(c, hd) for c in chunks for hd in heads]
        bmat = {c: _dot_exact(tri, lf[rsl[c]]) for c in chunks}
        rmat = {c: gi[rsl[c]] - bmat[c] for c in chunks}
        rmat_t = {c: rmat[c].T for c in chunks}
        bcol = {(c, hd): bmat[c][:, hd:hd + 1] for c, hd in pairs}
        rrow = {(c, hd): rmat_t[c][hd:hd + 1, :] for c, hd in pairs}
        g = {(c, hd): bmat[c][L - 1:L, hd:hd + 1] for c, hd in pairs}
        a = {(c, hd): g[c, hd] + rmat[c][:, hd:hd + 1] for c, hd in pairs}
        a_max = {pr: jnp.max(a[pr], axis=0, keepdims=True) for pr in pairs}

        m_in, m_out = {}, {}
        for c in chunks:
            for hd in heads:
                m_in[c, hd] = m_run[hd]
                m_out[c, hd] = jnp.maximum(g[c, hd] + m_run[hd], a_max[c, hd])
            m_run = [m_out[c, hd] for hd in heads]

        k = {(c, hd): ks[hd][rsl[c]] for c, hd in pairs}
        qb = {(c, hd): qs[hd][rsl[c]].astype(BF16) for c, hd in pairs}
        kb = {pr: k[pr].astype(BF16) for pr in pairs}
        vext = {(c, hd): jnp.concatenate([vs[hd][rsl[c]], ones_l], axis=1).astype(BF16)
                for c, hd in pairs}

        log_d = {pr: jnp.where(causal, bcol[pr] + rrow[pr], NEG_INF) for pr in pairs}
        log_inter = {pr: bcol[pr] + m_in[pr] for pr in pairs}
        m_t = {pr: jnp.maximum(log_inter[pr], jnp.max(log_d[pr], axis=1, keepdims=True)) for pr in pairs}
        qk = {pr: _dot_nt(qb[pr], kb[pr]) for pr in pairs}
        p = {pr: (jnp.exp(log_d[pr] - m_t[pr]) * qk[pr]).astype(BF16) for pr in pairs}
        w_inter = {pr: jnp.exp(log_inter[pr] - m_t[pr]) for pr in pairs}
        intra = {pr: _dot(p[pr], vext[pr]) for pr in pairs}
        decay = {pr: jnp.exp(g[pr] + m_in[pr] - m_out[pr]) for pr in pairs}
        kw_t = {pr: (k[pr] * jnp.exp(a[pr] - m_out[pr])).T.astype(BF16) for pr in pairs}
        update = {pr: _dot(kw_t[pr], vext[pr]) for pr in pairs}

        st_in = {}
        for c in chunks:
            for hd in heads:
                st_in[c, hd] = state[hd]
            state = [decay[c, hd] * state[hd] + update[c, hd] for hd in heads]

        inter = {pr: _dot(qb[pr], st_in[pr].astype(BF16)) for pr in pairs}
        out_ext = {pr: w_inter[pr] * inter[pr] + intra[pr] for pr in pairs}
        hh = {pr: out_ext[pr][:, :dh] / jnp.maximum(jnp.abs(out_ext[pr][:, dh:]), jnp.exp(-m_t[pr]))
              for pr in pairs}
        hc = {(c, hd): _ln(hh[c, hd]) * mhg_ref[:, hsl[hd]] for c, hd in pairs}
        for c, hd in pairs:
            hm = jax.nn.sigmoid(z[rsl[c], hsl[hd]]) * (hc[c, hd] + skip_ref[:, hsl[hd]] * xc[rsl[c], hsl[hd]])
            cat_ref[rsl[c], hsl[hd]] = hm.astype(BF16)

        for c in chunks:
            ug = _gelu(u[rsl[c]])
            vg = _gelu(v[rsl[c]])
            vn = [(_ln(vg[:, gsl[gg]]) * sgg_ref[:, gsl[gg]] + sgb_ref[:, gsl[gg]]).astype(BF16) for gg in groups]
            sp = [_dot(wsg[gg], vn[gg]) + bspt_ref[:, gg:gg + 1] for gg in groups]
            for gg in groups:
                cat_ref[rsl[c], mw + gg * gd:mw + (gg + 1) * gd] = (ug[:, gsl[gg]] * sp[gg]).astype(BF16)

    for hd in range(M_HEADS):
        state_ref[hd] = state[hd]
        m_ref[hd:hd + 1, :] = jnp.broadcast_to(m_run[hd], (1, LANES))

    mix = _dot(cat_ref[...], w_out_ref[...])
    x1 = _ln(alpha * x + (1.0 + gt_a) * mix) * ln1g_ref[...] + ln1b_ref[...]
    x1_ref[0] = x1
    h2 = _ln(x1) * (1.0 + sc_f) + sh_f
    h2_ref[...] = _to_slabs(h2, slab_ref)
    h_hi = h2.astype(BF16)
    h_lo = (h2 - h_hi.astype(F32)).astype(BF16)
    wr_hi = wr_hi_ref[...]
    logt_ref[...] = _dot_nt(wr_hi, h_hi) + _dot_nt(wr_hi, h_lo) + _dot_nt(wr_lo_ref[...], h_hi)


def _mixer(x, mod, p, alpha):
    bsz, seq, d = x.shape
    tm = min(MIXER_TOKENS, seq)
    nt = seq // tm
    mw = p["conv_w"].shape[1]
    gw = p["sg_g"].shape[1]
    dh = mw // M_HEADS
    n_exp = p["wr_hi"].shape[0]
    const2 = lambda b, i: (0, 0)
    const3 = lambda b, i: (0, 0, 0)
    full = lambda a: pl.BlockSpec(a.shape, const2 if a.ndim == 2 else const3)
    names = ["w_in", "conv_w", "conv_b", "w_q", "w_k", "w_v", "w_if", "b_if", "mh_g", "skip", "sg_g",
             "sg_b", "w_sp", "b_sp_t", "w_out", "ln1_g", "ln1_b", "wr_hi", "wr_lo"]
    weights = [p[n] for n in names]
    return pl.pallas_call(
        functools.partial(_mixer_kernel, alpha=alpha),
        out_shape=(
            jax.ShapeDtypeStruct((bsz, seq, d), F32),
            jax.ShapeDtypeStruct((bsz * seq, SLAB, LANES), BF16),
            jax.ShapeDtypeStruct((n_exp, bsz * seq), F32),
        ),
        grid=(bsz, nt),
        in_specs=[
            pl.BlockSpec((1, tm, d), lambda b, i: (b, i, 0)),
            pl.BlockSpec((1, SUBLANES, d), lambda b, i: (b, 0, 0)),
        ] + [full(w) for w in weights],
        out_specs=(
            pl.BlockSpec((1, tm, d), lambda b, i: (b, i, 0)),
            pl.BlockSpec((tm, SLAB, LANES), lambda b, i: (b * nt + i, 0, 0)),
            pl.BlockSpec((n_exp, tm), lambda b, i: (0, b * nt + i)),
        ),
        scratch_shapes=[
            pltpu.VMEM((M_HEADS, dh, 2 * dh), F32),
            pltpu.VMEM((SUBLANES, LANES), F32),
            pltpu.VMEM((tm + SUBLANES, mw), F32),
            pltpu.VMEM((tm, mw + gw), BF16),
            pltpu.VMEM((tm * SLAB, LANES), F32),
        ],
        compiler_params=pltpu.CompilerParams(
            dimension_semantics=("arbitrary", "arbitrary"), vmem_limit_bytes=VMEM_LIMIT),
        name="mixer",
    )(x, mod, *weights)


def _route_kernel(logt_ref, bias_ref, upper_ref, idx_ref, w_ref, rank_ref, wt_ref, cnt_ref, carry_ref):
    n_exp, tr = logt_ref.shape
    epg = n_exp // N_GROUPS

    @pl.when(pl.program_id(0) == 0)
    def _():
        carry_ref[...] = jnp.zeros_like(carry_ref)

    scores = jax.nn.sigmoid(logt_ref[...])
    sel = scores + bias_ref[...]

    sel3 = sel.reshape(N_GROUPS, epg, tr)
    io3 = lax.broadcasted_iota(I32, (N_GROUPS, epg, tr), 1)
    m1 = jnp.max(sel3, axis=1, keepdims=True)
    first = jnp.min(jnp.where(sel3 == m1, io3, epg), axis=1, keepdims=True)
    m2 = jnp.max(jnp.where(io3 == first, NEG_INF, sel3), axis=1, keepdims=True)
    gs = (m1 + m2).reshape(N_GROUPS, tr)

    gio = lax.broadcasted_iota(I32, (N_GROUPS, tr), 0)
    gmask = jnp.zeros((N_GROUPS, tr), F32)
    for _ in range(TOPK_GROUPS):
        m = jnp.max(gs, axis=0, keepdims=True)
        gi = jnp.min(jnp.where(gs == m, gio, N_GROUPS), axis=0, keepdims=True)
        hit = gio == gi
        gmask = jnp.where(hit, 1.0, gmask)
        gs = jnp.where(hit, NEG_INF, gs)
    emask = jnp.broadcast_to(gmask.reshape(N_GROUPS, 1, tr), (N_GROUPS, epg, tr)).reshape(n_exp, tr)
    selm = jnp.where(emask > 0.0, sel, NEG_INF)

    eio = lax.broadcasted_iota(I32, (n_exp, tr), 0)
    chosen = jnp.zeros((n_exp, tr), F32)
    idx_rows, w_rows = [], []
    for _ in range(TOP_K):
        m = jnp.max(selm, axis=0, keepdims=True)
        ei = jnp.min(jnp.where(selm == m, eio, n_exp), axis=0, keepdims=True)
        hit = eio == ei
        w_rows.append(jnp.sum(jnp.where(hit, scores, 0.0), axis=0, keepdims=True))
        idx_rows.append(ei)
        selm = jnp.where(hit, NEG_INF, selm)
        chosen = jnp.where(hit, 1.0, chosen)

    chosen_b = chosen.astype(BF16)
    carry = carry_ref[...]
    ranks = carry[:, 0:1] + _dot(chosen_b, upper_ref[...])
    carry_new = carry + _dot(chosen_b, jnp.ones((tr, LANES), BF16))
    carry_ref[...] = carry_new
    cnt_ref[...] = carry_new
    rank_rows = [jnp.sum(jnp.where(eio == ei, ranks, 0.0), axis=0, keepdims=True) for ei in idx_rows]

    wsum = w_rows[0]
    for wk in w_rows[1:]:
        wsum = wsum + wk
    w8 = jnp.concatenate([wk / wsum * ROUTE_SCALE for wk in w_rows], axis=0)
    idx_ref[...] = jnp.concatenate(idx_rows, axis=0)
    rank_ref[...] = jnp.concatenate(rank_rows, axis=0).astype(I32)
    w_ref[...] = w8
    wpad = jnp.concatenate([w8, jnp.zeros((LANES - TOP_K, tr), F32)], axis=0)
    wt_ref[...] = wpad.T


def _route(logt, e_bias_col, upper):
    n_exp, t = logt.shape
    tr = upper.shape[0]
    return pl.pallas_call(
        _route_kernel,
        out_shape=(
            jax.ShapeDtypeStruct((TOP_K, t), I32),
            jax.ShapeDtypeStruct((TOP_K, t), F32),
            jax.ShapeDtypeStruct((TOP_K, t), I32),
            jax.ShapeDtypeStruct((t, LANES), F32),
            jax.ShapeDtypeStruct((n_exp, LANES), F32),
        ),
        grid=(t // tr,),
        in_specs=[
            pl.BlockSpec((n_exp, tr), lambda i: (0, i)),
            pl.BlockSpec((n_exp, 1), lambda i: (0, 0)),
            pl.BlockSpec((tr, tr), lambda i: (0, 0)),
        ],
        out_specs=(
            pl.BlockSpec((TOP_K, tr), lambda i: (0, i)),
            pl.BlockSpec((TOP_K, tr), lambda i: (0, i)),
            pl.BlockSpec((TOP_K, tr), lambda i: (0, i)),
            pl.BlockSpec((tr, LANES), lambda i: (i, 0)),
            pl.BlockSpec((n_exp, LANES), lambda i: (0, 0)),
        ),
        scratch_shapes=[pltpu.VMEM((n_exp, LANES), F32)],
        compiler_params=pltpu.CompilerParams(
            dimension_semantics=("arbitrary",), vmem_limit_bytes=VMEM_LIMIT),
        name="route",
    )(logt, e_bias_col, upper)


def _dest_kernel(idx_ref, rank_ref, start_ref, dest_ref):
    n_exp = start_ref.shape[0]
    tt = idx_ref.shape[1]
    eio = lax.broadcasted_iota(I32, (n_exp, tt), 0)
    start = start_ref[...]
    rows = []
    for k in range(TOP_K):
        hit = eio == idx_ref[k:k + 1, :]
        rows.append(jnp.sum(jnp.where(hit, start, 0), axis=0, keepdims=True) + rank_ref[k:k + 1, :])
    dest_ref[...] = jnp.concatenate(rows, axis=0)


def _dest(idx, rank, start_col):
    t = idx.shape[1]
    tt = min(DEST_TOKENS, t)
    n_exp = start_col.shape[0]
    return pl.pallas_call(
        _dest_kernel,
        out_shape=jax.ShapeDtypeStruct((TOP_K, t), I32),
        grid=(t // tt,),
        in_specs=[
            pl.BlockSpec((TOP_K, tt), lambda i: (0, i)),
            pl.BlockSpec((TOP_K, tt), lambda i: (0, i)),
            pl.BlockSpec((n_exp, 1), lambda i: (0, 0)),
        ],
        out_specs=pl.BlockSpec((TOP_K, tt), lambda i: (0, i)),
        compiler_params=pltpu.CompilerParams(vmem_limit_bytes=VMEM_LIMIT),
        name="dest",
    )(idx, rank, start_col)


def _slabs_copy(hbm, sem, n_slabs):
    return pltpu.make_async_copy(hbm.at[pl.ds(0, n_slabs)], hbm.at[pl.ds(0, n_slabs)], sem)


def _dispatch_kernel(pad_first, pad_len, n_used, dest_ref, h_ref, xs_hbm, zeros_ref, sem):
    td = dest_ref.shape[1]
    n_exp = pad_first.shape[0]
    n_blocks_max = xs_hbm.shape[0] // EXPERT_ROWS
    step = pl.program_id(0)

    def zero_fill(act):
        def zeros_to(first_row, n_rows):
            act(pltpu.make_async_copy(zeros_ref.at[pl.ds(0, n_rows)], xs_hbm.at[pl.ds(first_row, n_rows)],
                                      sem.at[1]))

        def expert_padding(e, carry):
            row = pad_first[e]
            for bit in reversed(range(EXPERT_ROWS.bit_length() - 1)):
                take = (pad_len[e] >> bit) & 1
                pl.when(take == 1)(functools.partial(zeros_to, row, 1 << bit))
                row = row + (take << bit)
            return carry

        def unused_block(b, carry):
            zeros_to(b * EXPERT_ROWS, EXPERT_ROWS)
            return carry

        lax.fori_loop(0, n_exp, expert_padding, 0)
        lax.fori_loop(n_used[0], n_blocks_max, unused_block, 0)

    @pl.when(step == 0)
    def _():
        zeros_ref[...] = jnp.zeros_like(zeros_ref)
        zero_fill(lambda cp: cp.start())

    def body(t, carry):
        for k in range(TOP_K):
            pltpu.make_async_copy(h_ref.at[t], xs_hbm.at[dest_ref[k, t]], sem.at[0]).start(priority=k % 2)
        return carry

    lax.fori_loop(0, td, body, 0, unroll=2)
    _slabs_copy(xs_hbm, sem.at[0], td * TOP_K).wait()

    @pl.when(step == pl.num_programs(0) - 1)
    def _():
        zero_fill(lambda cp: cp.wait())


def _dispatch(pad_first, pad_len, n_used, dest, h_slabs, n_rows):
    t = h_slabs.shape[0]
    td = min(DISPATCH_TOKENS, t)
    grid_spec = pltpu.PrefetchScalarGridSpec(
        num_scalar_prefetch=3,
        grid=(t // td,),
        in_specs=[
            pl.BlockSpec((TOP_K, td), lambda i, pf, pn, nu: (0, i), memory_space=pltpu.SMEM),
            pl.BlockSpec((td, SLAB, LANES), lambda i, pf, pn, nu: (i, 0, 0)),
        ],
        out_specs=pl.BlockSpec(memory_space=pl.ANY),
        scratch_shapes=[
            pltpu.VMEM((EXPERT_ROWS, SLAB, LANES), h_slabs.dtype),
            pltpu.SemaphoreType.DMA((2,)),
        ],
    )
    return pl.pallas_call(
        _dispatch_kernel,
        out_shape=jax.ShapeDtypeStruct((n_rows, SLAB, LANES), h_slabs.dtype),
        grid_spec=grid_spec,
        compiler_params=pltpu.CompilerParams(
            dimension_semantics=("arbitrary",), vmem_limit_bytes=VMEM_LIMIT),
        name="dispatch",
    )(pad_first, pad_len, n_used, dest, h_slabs)


def _experts_kernel(block_expert, block_new_expert, block_slot, block_next_expert, n_blocks,
                    xs_ref, wg_hbm, wu_hbm, wd_hbm, ys_ref,
                    wg_f, wu_f, wd_f, wg_b, wu_b, wd_b, slab_ref, sem):
    step = pl.program_id(0)

    def weight_copies(e, slot):
        return [pltpu.make_async_copy(hbm.at[e], buf.at[slot], sem.at[slot])
                for hbm, buf in ((wg_hbm, wg_f), (wu_hbm, wu_f), (wd_hbm, wd_f))]

    def block(i, rows):
        @pl.when(block_new_expert[i] == 1)
        def _():
            slot = block_slot[i]

            @pl.when(i == 0)
            def _():
                for cp in weight_copies(block_expert[i], slot):
                    cp.start()

            for cp in weight_copies(block_expert[i], slot):
                cp.wait()
            wg_b[...] = wg_f[slot].astype(BF16)
            wu_b[...] = wu_f[slot].astype(BF16)
            wd_b[...] = wd_f[slot].astype(BF16)

            @pl.when(block_next_expert[i] >= 0)
            def _():
                for cp in weight_copies(block_next_expert[i], 1 - slot):
                    cp.start()

        xb = _from_slabs(xs_ref[rows], slab_ref).astype(BF16)
        g = _dot(xb, wg_b[...])
        u = _dot(xb, wu_b[...])
        y = _dot((_silu(g) * u).astype(BF16), wd_b[...])
        ys_ref[rows] = _to_slabs(y, slab_ref)

    for j in range(EXPERT_BLOCKS_PER_STEP):
        i = step * EXPERT_BLOCKS_PER_STEP + j
        rows = slice(j * EXPERT_ROWS, (j + 1) * EXPERT_ROWS)
        pl.when(i < n_blocks[0])(functools.partial(block, i, rows))

        @pl.when((i >= n_blocks[0]) & (step * EXPERT_BLOCKS_PER_STEP < n_blocks[0]))
        def _():
            ys_ref[rows] = xs_ref[rows]


def _experts(meta, xs, w_gate, w_up, w_down):
    n_scalar = len(meta)
    _, d, de = w_gate.shape
    n_max = meta[0].shape[0]
    per_step = EXPERT_BLOCKS_PER_STEP
    step_rows = per_step * EXPERT_ROWS
    blk = lambda i, be, ne, sl, nx, nb: (jnp.minimum(i, (nb[0] - 1) // per_step), 0, 0)
    grid_spec = pltpu.PrefetchScalarGridSpec(
        num_scalar_prefetch=n_scalar,
        grid=(n_max // per_step,),
        in_specs=[
            pl.BlockSpec((step_rows, SLAB, LANES), blk),
            pl.BlockSpec(memory_space=pl.ANY),
            pl.BlockSpec(memory_space=pl.ANY),
            pl.BlockSpec(memory_space=pl.ANY),
        ],
        out_specs=pl.BlockSpec((step_rows, SLAB, LANES), blk),
        scratch_shapes=[
            pltpu.VMEM((2, d, de), F32),
            pltpu.VMEM((2, d, de), F32),
            pltpu.VMEM((2, de, d), F32),
            pltpu.VMEM((d, de), BF16),
            pltpu.VMEM((d, de), BF16),
            pltpu.VMEM((de, d), BF16),
            pltpu.VMEM((EXPERT_ROWS * SLAB, LANES), F32),
            pltpu.SemaphoreType.DMA((2,)),
        ],
    )
    return pl.pallas_call(
        _experts_kernel,
        out_shape=jax.ShapeDtypeStruct(xs.shape, xs.dtype),
        grid_spec=grid_spec,
        input_output_aliases={n_scalar: 0},
        compiler_params=pltpu.CompilerParams(
            dimension_semantics=("arbitrary",), vmem_limit_bytes=VMEM_LIMIT),
        name="experts",
    )(*meta, xs, w_gate, w_up, w_down)


def _expert_blocks(counts, n_assign):
    n_exp = counts.shape[0]
    n_max = (n_assign + n_exp * (EXPERT_ROWS - 1)) // EXPERT_ROWS + 1
    n_max += -n_max % EXPERT_BLOCKS_PER_STEP
    padded = (counts + EXPERT_ROWS - 1) // EXPERT_ROWS * EXPERT_ROWS
    pad_end = jnp.cumsum(padded)
    pad_start = pad_end - padded
    n_used = pad_end[-1] // EXPERT_ROWS
    first_row = jnp.minimum(jnp.arange(n_max, dtype=I32), n_used - 1) * EXPERT_ROWS
    e_of = jnp.sum((first_row[:, None] >= pad_end[None, :]).astype(I32), axis=1)
    e_of = jnp.minimum(e_of, n_exp - 1).astype(I32)
    prev_e = jnp.concatenate([jnp.full((1,), -1, I32), e_of[:-1]])
    new_expert = (e_of != prev_e).astype(I32)
    slot = ((jnp.cumsum(new_expert) - 1) & 1).astype(I32)
    ids = jnp.arange(n_exp, dtype=I32)
    later_used = (ids[None, :] > ids[:, None]) & (padded[None, :] > 0)
    next_used = jnp.min(jnp.where(later_used, ids[None, :], n_exp), axis=1)
    next_used = jnp.where(next_used < n_exp, next_used, -1)
    next_expert = jnp.sum(jnp.where(e_of[:, None] == ids[None, :], next_used[None, :], 0), axis=1).astype(I32)
    n_used = n_used.reshape(1).astype(I32)
    meta = (e_of, new_expert, slot, next_expert, n_used)
    return (meta, n_used, pad_start.astype(I32), (pad_start + counts).astype(I32),
            (padded - counts).astype(I32), n_max)


def _combine_kernel(dest_cur, dest_nxt, x1_ref, mod_ref, wt_ref, wsg_ref, wsu_ref, wsd_ref,
                    ln2g_ref, ln2b_ref, ys_hbm, out_ref, buf, wb_ref, acc_ref, sem, *, alpha):
    tc = x1_ref.shape[0]
    i = pl.program_id(0)
    n = pl.num_programs(0)
    slot = i % 2

    def request(dest_ref, s, t):
        for k in range(TOP_K):
            pltpu.make_async_copy(ys_hbm.at[dest_ref[k, t]], buf.at[s, t * TOP_K + k],
                                  sem.at[s]).start(priority=k % 2)

    @pl.when(i == 0)
    def _():
        def body(t, carry):
            request(dest_cur, 0, t)
            return carry
        lax.fori_loop(0, tc, body, 0)

    pltpu.make_async_copy(ys_hbm.at[pl.ds(0, tc * TOP_K)], buf.at[slot], sem.at[slot]).wait()

    wt = wt_ref[...]
    for k in range(TOP_K):
        wb_ref[k] = jnp.broadcast_to(wt[:, k:k + 1], (tc, LANES))

    def token(t, carry, prefetch):
        if prefetch:
            request(dest_nxt, 1 - slot, t)
        terms = [jnp.broadcast_to(wb_ref[k, pl.ds(t, 1), :], (SLAB, LANES))
                 * buf[slot, t * TOP_K + k].astype(F32)
                 for k in range(TOP_K)]
        while len(terms) > 1:
            terms = [a + b for a, b in zip(terms[0::2], terms[1::2])]
        acc_ref[pl.ds(pl.multiple_of(t * SLAB, SLAB), SLAB), :] = terms[0]
        return carry

    @pl.when(i + 1 < n)
    def _():
        lax.fori_loop(0, tc, functools.partial(token, prefetch=True), 0, unroll=8)

    @pl.when(i + 1 == n)
    def _():
        lax.fori_loop(0, tc, functools.partial(token, prefetch=False), 0, unroll=8)

    mod = mod_ref[0]
    sh_f, sc_f, gt_f = mod[3:4], mod[4:5], mod[5:6]
    x1 = x1_ref[...]
    hb = (_ln(x1) * (1.0 + sc_f) + sh_f).astype(BF16)
    mid = _silu(_dot(hb, wsg_ref[...])) * _dot(hb, wsu_ref[...])
    ffn = _dot(mid.astype(BF16), wsd_ref[...])
    ffn = _slabs_to_rows(acc_ref, tc) + ffn
    out_ref[...] = _ln(alpha * x1 + (1.0 + gt_f) * ffn) * ln2g_ref[...] + ln2b_ref[...]


def _combine(dest, x1_flat, mod, wt, ws_gate, ws_up, ws_down, ln2_g, ln2_b, ys, seq, alpha):
    t, d = x1_flat.shape
    tc = min(COMBINE_TOKENS, seq)
    n = t // tc
    per_seq = seq // tc
    const = lambda i: (0, 0)
    return pl.pallas_call(
        functools.partial(_combine_kernel, alpha=alpha),
        out_shape=jax.ShapeDtypeStruct((t, d), F32),
        grid=(n,),
        in_specs=[
            pl.BlockSpec((TOP_K, tc), lambda i: (0, i), memory_space=pltpu.SMEM),
            pl.BlockSpec((TOP_K, tc), lambda i: (0, jnp.minimum(i + 1, n - 1)), memory_space=pltpu.SMEM),
            pl.BlockSpec((tc, d), lambda i: (i, 0)),
            pl.BlockSpec((1, SUBLANES, d), lambda i: (i // per_seq, 0, 0)),
            pl.BlockSpec((tc, LANES), lambda i: (i, 0)),
            pl.BlockSpec(ws_gate.shape, const),
            pl.BlockSpec(ws_up.shape, const),
            pl.BlockSpec(ws_down.shape, const),
            pl.BlockSpec(ln2_g.shape, const),
            pl.BlockSpec(ln2_b.shape, const),
            pl.BlockSpec(memory_space=pl.ANY),
        ],
        out_specs=pl.BlockSpec((tc, d), lambda i: (i, 0)),
        scratch_shapes=[
            pltpu.VMEM((2, tc * TOP_K, SLAB, LANES), BF16),
            pltpu.VMEM((TOP_K, tc, LANES), F32),
            pltpu.VMEM((tc * SLAB, LANES), F32),
            pltpu.SemaphoreType.DMA((2,)),
        ],
        compiler_params=pltpu.CompilerParams(
            dimension_semantics=("arbitrary",), vmem_limit_bytes=VMEM_LIMIT),
        name="combine",
    )(dest, dest, x1_flat, mod, wt, ws_gate, ws_up, ws_down, ln2_g, ln2_b, ys)


def _layer(x, c_pad, lw, alpha):
    bsz, seq, d = x.shape
    t = bsz * seq
    n_exp = lw["w_router"].shape[1]
    h_count = lw["w_if"].shape[1] // 2

    mod = _ada(c_pad, lw["w_ada"], lw["b_ada"].reshape(1, -1))[:bsz].reshape(bsz, 6, d)
    mod = jnp.pad(mod, ((0, 0), (0, SUBLANES - 6), (0, 0)))

    row2 = lambda a: a.reshape(1, -1)
    w_if = lw["w_if"]
    w_if_pad = jnp.zeros((w_if.shape[0], 2 * LANES), F32)
    w_if_pad = w_if_pad.at[:, :h_count].set(w_if[:, :h_count]).at[:, LANES:LANES + h_count].set(w_if[:, h_count:])
    b_if_pad = jnp.zeros((1, 2 * LANES), F32)
    b_if_pad = b_if_pad.at[0, :h_count].set(lw["b_if"][:h_count]).at[0, LANES:LANES + h_count].set(lw["b_if"][h_count:])
    w_rt = lw["w_router"].astype(F32).T
    wr_hi = w_rt.astype(BF16)
    wr_lo = (w_rt - wr_hi.astype(F32)).astype(BF16)
    p = {
        "w_in": lw["w_in"].astype(BF16), "conv_w": lw["conv_w"], "conv_b": row2(lw["conv_b"]),
        "w_q": lw["w_q"].astype(BF16), "w_k": lw["w_k"].astype(BF16), "w_v": lw["w_v"].astype(BF16),
        "w_if": w_if_pad.astype(BF16), "b_if": b_if_pad,
        "mh_g": row2(lw["mh_g"]), "skip": row2(lw["skip"]), "sg_g": row2(lw["sg_g"]), "sg_b": row2(lw["sg_b"]),
        "w_sp": lw["w_sp"], "b_sp_t": lw["b_sp"].T, "w_out": lw["w_out"].astype(BF16),
        "ln1_g": row2(lw["ln1_g"]), "ln1_b": row2(lw["ln1_b"]), "wr_hi": wr_hi, "wr_lo": wr_lo,
    }
    x1, h2, logt = _mixer(x, mod, p, alpha)

    tr = min(ROUTE_TOKENS, t)
    upper = (jnp.arange(tr)[:, None] < jnp.arange(tr)[None, :]).astype(BF16)
    idx, _, rank, wt, cnt = _route(logt, lw["e_bias"].astype(F32).reshape(n_exp, 1), upper)
    counts = cnt[:, 0].astype(I32)
    meta, n_used, pad_start, pad_first, pad_len, n_blocks_max = _expert_blocks(counts, t * TOP_K)
    dest = _dest(idx, rank, pad_start.reshape(n_exp, 1))

    xs = _dispatch(pad_first, pad_len, n_used, dest, h2, n_blocks_max * EXPERT_ROWS)
    ys = _experts(meta, xs, lw["w_gate"], lw["w_up"], lw["w_down"])
    out = _combine(dest, x1.reshape(t, d), mod, wt, lw["ws_gate"].astype(BF16), lw["ws_up"].astype(BF16),
                   lw["ws_down"].astype(BF16), row2(lw["ln2_g"]), row2(lw["ln2_b"]), ys, seq, alpha)
    return out.reshape(bsz, seq, d)


def kernel(x, c, w_ada, b_ada, w_in, conv_w, conv_b, w_q, w_k, w_v, w_if, b_if, mh_g, skip, sg_g, sg_b, w_sp, b_sp, w_out, ln1_g, ln1_b, w_router, e_bias, w_gate, w_up, w_down, ws_gate, ws_up, ws_down, ln2_g, ln2_b):
    stacked = dict(w_ada=w_ada, b_ada=b_ada, w_in=w_in, conv_w=conv_w, conv_b=conv_b, w_q=w_q, w_k=w_k,
                   w_v=w_v, w_if=w_if, b_if=b_if, mh_g=mh_g, skip=skip, sg_g=sg_g, sg_b=sg_b, w_sp=w_sp,
                   b_sp=b_sp, w_out=w_out, ln1_g=ln1_g, ln1_b=ln1_b, w_router=w_router, e_bias=e_bias,
                   w_gate=w_gate, w_up=w_up, w_down=w_down, ws_gate=ws_gate, ws_up=ws_up, ws_down=ws_down,
                   ln2_g=ln2_g, ln2_b=ln2_b)
    depth = w_ada.shape[0]
    alpha = float((2 * depth) ** 0.25)
    bsz = x.shape[0]
    c_pad = jnp.pad(c, ((0, -bsz % SUBLANES), (0, 0)))
    for l in range(depth):
        x = _layer(x, c_pad, {k: v[l] for k, v in stacked.items()}, alpha)
    return x
```

```python
import functools
import math

import jax
import jax.numpy as jnp
from jax import lax
from jax.experimental import pallas as pl
from jax.experimental.pallas import tpu as pltpu

F32 = jnp.float32
BF16 = jnp.bfloat16
I32 = jnp.int32

LN_EPS = 1e-5
M_HEADS = 4
G_GROUPS = 4
CHUNK = 128
N_GROUPS = 8
TOPK_GROUPS = 4
TOP_K = 8
ROUTE_SCALE = 2.5
LANES = 128
SUBLANES = 8

MIXER_TOKENS = 512
MIXER_CHUNK_GROUP = 4
ROUTE_TOKENS = 1024
DEST_TOKENS = 2048
DISPATCH_TOKENS = 2048
EXPERT_ROWS = 512
EXPERT_BLOCKS_PER_STEP = 2
COMBINE_TOKENS = 512
VMEM_LIMIT = 56 * 1024 * 1024

NEG_INF = float("-inf")


def _ln(x):
    mu = jnp.mean(x, axis=-1, keepdims=True)
    xc = x - mu
    var = jnp.mean(xc * xc, axis=-1, keepdims=True)
    return xc * lax.rsqrt(var + LN_EPS)


def _dot(a, b):
    return jnp.dot(a, b, preferred_element_type=F32)


def _dot_nt(a, b):
    return lax.dot_general(a, b, (((1,), (1,)), ((), ())), preferred_element_type=F32)


def _dot_exact(a, b):
    return jnp.dot(a, b, preferred_element_type=F32, precision=lax.Precision.HIGHEST)


def _silu(x):
    return x * jax.nn.sigmoid(x)


def _gelu(x):
    return 0.5 * x * (1.0 + lax.erf(x * math.sqrt(0.5)))


def _log_sigmoid(x):
    return jnp.minimum(x, 0.0) - jnp.log1p(jnp.exp(-jnp.abs(x)))


SLAB = SUBLANES


def _to_slabs(x, scratch):
    rows, d = x.shape
    assert d == SLAB * LANES
    for s in range(SLAB):
        scratch[pl.ds(s, rows, stride=SLAB), :] = x[:, s * LANES:(s + 1) * LANES]
    return scratch[0:rows * SLAB, :].reshape(rows, SLAB, LANES).astype(BF16)


def _slabs_to_rows(scratch, rows):
    return jnp.concatenate([scratch[pl.ds(s, rows, stride=SLAB), :] for s in range(SLAB)], axis=1)


def _from_slabs(x3, scratch):
    rows = x3.shape[0]
    scratch[0:rows * SLAB, :] = x3.astype(F32).reshape(rows * SLAB, LANES)
    return _slabs_to_rows(scratch, rows)


def _ada_kernel(c_ref, w_ref, b_ref, o_ref):
    o_ref[...] = _dot_exact(_silu(c_ref[...]), w_ref[...]) + b_ref[...]


def _ada(c_pad, w_ada, b_ada):
    rows, d = c_pad.shape
    n = w_ada.shape[1]
    return pl.pallas_call(
        _ada_kernel,
        out_shape=jax.ShapeDtypeStruct((rows, n), F32),
        grid=(n // d,),
        in_specs=[
            pl.BlockSpec((rows, d), lambda j: (0, 0)),
            pl.BlockSpec((d, d), lambda j: (0, j)),
            pl.BlockSpec((1, d), lambda j: (0, j)),
        ],
        out_specs=pl.BlockSpec((rows, d), lambda j: (0, j)),
        compiler_params=pltpu.CompilerParams(vmem_limit_bytes=VMEM_LIMIT),
        name="ada",
    )(c_pad, w_ada, b_ada)


def _mixer_kernel(x_ref, mod_ref, w_in_ref, conv_w_ref, conv_b_ref, wq_ref, wk_ref, wv_ref,
                  wif_ref, bif_ref, mhg_ref, skip_ref, sgg_ref, sgb_ref, wsp_ref, bspt_ref,
                  w_out_ref, ln1g_ref, ln1b_ref, wr_hi_ref, wr_lo_ref,
                  x1_ref, h2_ref, logt_ref,
                  state_ref, m_ref, xm_ref, cat_ref, slab_ref, qkv_ref, k32_ref, *, alpha):
    tm = x_ref.shape[1]
    mw = conv_w_ref.shape[1]
    dh = mw // M_HEADS
    gw = sgg_ref.shape[1]
    gd = gw // G_GROUPS
    conv_k = conv_w_ref.shape[0]
    nch = tm // CHUNK
    L = CHUNK

    @pl.when(pl.program_id(1) == 0)
    def _():
        state_ref[...] = jnp.zeros_like(state_ref)
        m_ref[...] = jnp.zeros_like(m_ref)
        xm_ref[0:SUBLANES, :] = jnp.zeros((SUBLANES, mw), F32)

    x = x_ref[0]
    mod = mod_ref[0]
    sh_a, sc_a, gt_a = mod[0:1], mod[1:2], mod[2:3]
    sh_f, sc_f = mod[3:4], mod[4:5]

    h = _ln(x) * (1.0 + sc_a) + sh_a
    proj = _dot(h.astype(BF16), w_in_ref[...])
    xm = proj[:, :mw]
    z = proj[:, mw:2 * mw]
    u = proj[:, 2 * mw:2 * mw + gw]
    v = proj[:, 2 * mw + gw:]

    xm_ref[SUBLANES:SUBLANES + tm, :] = xm
    conv = jnp.broadcast_to(conv_b_ref[...], (tm, mw))
    for j in range(conv_k):
        off = SUBLANES - (conv_k - 1) + j
        conv = conv + conv_w_ref[j:j + 1, :] * xm_ref[off:off + tm, :]
    xm_ref[0:SUBLANES, :] = xm_ref[tm:tm + SUBLANES, :]
    xc = _silu(conv)

    scale = dh ** -0.5
    for hd in range(M_HEADS):
        sl = slice(hd * dh, (hd + 1) * dh)
        xch = xc[:, sl].astype(BF16)
        kf = _dot(xch, wk_ref[hd]) * scale
        qkv_ref[:, sl] = _dot(xch, wq_ref[hd]).astype(BF16)
        qkv_ref[:, mw + hd * dh:mw + (hd + 1) * dh] = kf.astype(BF16)
        qkv_ref[:, 2 * mw + hd * dh:2 * mw + (hd + 1) * dh] = _dot(xm[:, sl].astype(BF16), wv_ref[hd]).astype(BF16)
        k32_ref[:, sl] = kf
    gate = _dot(qkv_ref[...], wif_ref[...]) + bif_ref[...]
    gi = gate[:, :LANES]
    lf = _log_sigmoid(gate[:, LANES:])

    row = lax.broadcasted_iota(I32, (L, L), 0)
    col = lax.broadcasted_iota(I32, (L, L), 1)
    causal = col <= row
    tri = jnp.where(causal, 1.0, 0.0).astype(F32)
    ones_l = jnp.ones((L, dh), BF16)
    wsg = [jnp.where(causal, wsp_ref[gg], 0.0).astype(BF16) for gg in range(G_GROUPS)]
    state = [state_ref[hd] for hd in range(M_HEADS)]
    m_run = [m_ref[hd:hd + 1, 0:1] for hd in range(M_HEADS)]

    heads = range(M_HEADS)
    hsl = [slice(hd * dh, (hd + 1) * dh) for hd in heads]
    groups = range(G_GROUPS)
    gsl = [slice(gg * gd, (gg + 1) * gd) for gg in groups]

    for c0 in range(0, nch, MIXER_CHUNK_GROUP):
        chunks = range(c0, min(c0 + MIXER_CHUNK_GROUP, nch))
        rsl = {c: slice(c * L, (c + 1) * L) for c in chunks}
        pairs = [(c, hd) for c in chunks for hd in heads]
        bmat = {c: _dot_exact(tri, lf[rsl[c]]) for c in chunks}
        rmat = {c: gi[rsl[c]] - bmat[c] for c in chunks}
        rmat_t = {c: rmat[c].T for c in chunks}
        bcol = {(c, hd): bmat[c][:, hd:hd + 1] for c, hd in pairs}
        rrow = {(c, hd): rmat_t[c][hd:hd + 1, :] for c, hd in pairs}
        g = {(c, hd): bmat[c][L - 1:L, hd:hd + 1] for c, hd in pairs}
        a = {(c, hd): g[c, hd] + rmat[c][:, hd:hd + 1] for c, hd in pairs}
        a_max = {pr: jnp.max(a[pr], axis=0, keepdims=True) for pr in pairs}

        m_in, m_out = {}, {}
        for c in chunks:
            for hd in heads:
                m_in[c, hd] = m_run[hd]
                m_out[c, hd] = jnp.maximum(g[c, hd] + m_run[hd], a_max[c, hd])
            m_run = [m_out[c, hd] for hd in heads]

        k = {(c, hd): k32_ref[rsl[c], hsl[hd]] for c, hd in pairs}
        qb = {(c, hd): qkv_ref[rsl[c], hsl[hd]] for c, hd in pairs}
        kb = {(c, hd): qkv_ref[rsl[c], mw + hd * dh:mw + (hd + 1) * dh] for c, hd in pairs}
        vext = {(c, hd): jnp.concatenate([qkv_ref[rsl[c], 2 * mw + hd * dh:2 * mw + (hd + 1) * dh], ones_l],
                                         axis=1) for c, hd in pairs}

        log_d = {pr: jnp.where(causal, bcol[pr] + rrow[pr], NEG_INF) for pr in pairs}
        log_inter = {pr: bcol[pr] + m_in[pr] for pr in pairs}
        m_t = {pr: jnp.maximum(log_inter[pr], jnp.max(log_d[pr], axis=1, keepdims=True)) for pr in pairs}
        qk = {pr: _dot_nt(qb[pr], kb[pr]) for pr in pairs}
        p = {pr: (jnp.exp(log_d[pr] - m_t[pr]) * qk[pr]).astype(BF16) for pr in pairs}
        w_inter = {pr: jnp.exp(log_inter[pr] - m_t[pr]) for pr in pairs}
        intra = {pr: _dot(p[pr], vext[pr]) for pr in pairs}
        decay = {pr: jnp.exp(g[pr] + m_in[pr] - m_out[pr]) for pr in pairs}
        kw_t = {pr: (k[pr] * jnp.exp(a[pr] - m_out[pr])).T.astype(BF16) for pr in pairs}
        update = {pr: _dot(kw_t[pr], vext[pr]) for pr in pairs}

        st_in = {}
        for c in chunks:
            for hd in heads:
                st_in[c, hd] = state[hd]
            state = [decay[c, hd] * state[hd] + update[c, hd] for hd in heads]

        inter = {pr: _dot(qb[pr], st_in[pr].astype(BF16)) for pr in pairs}
        out_ext = {pr: w_inter[pr] * inter[pr] + intra[pr] for pr in pairs}
        hh = {pr: out_ext[pr][:, :dh] / jnp.maximum(jnp.abs(out_ext[pr][:, dh:]), jnp.exp(-m_t[pr]))
              for pr in pairs}
        hc = {(c, hd): _ln(hh[c, hd]) * mhg_ref[:, hsl[hd]] for c, hd in pairs}
        for c, hd in pairs:
            hm = jax.nn.sigmoid(z[rsl[c], hsl[hd]]) * (hc[c, hd] + skip_ref[:, hsl[hd]] * xc[rsl[c], hsl[hd]])
            cat_ref[rsl[c], hsl[hd]] = hm.astype(BF16)

        for c in chunks:
            ug = _gelu(u[rsl[c]])
            vg = _gelu(v[rsl[c]])
            vn = [(_ln(vg[:, gsl[gg]]) * sgg_ref[:, gsl[gg]] + sgb_ref[:, gsl[gg]]).astype(BF16) for gg in groups]
            sp = [_dot(wsg[gg], vn[gg]) + bspt_ref[:, gg:gg + 1] for gg in groups]
            for gg in groups:
                cat_ref[rsl[c], mw + gg * gd:mw + (gg + 1) * gd] = (ug[:, gsl[gg]] * sp[gg]).astype(BF16)

    for hd in range(M_HEADS):
        state_ref[hd] = state[hd]
        m_ref[hd:hd + 1, :] = jnp.broadcast_to(m_run[hd], (1, LANES))

    mix = _dot(cat_ref[...], w_out_ref[...])
    x1 = _ln(alpha * x + (1.0 + gt_a) * mix) * ln1g_ref[...] + ln1b_ref[...]
    x1_ref[0] = x1
    h2 = _ln(x1) * (1.0 + sc_f) + sh_f
    h2_ref[...] = _to_slabs(h2, slab_ref)
    h_hi = h2.astype(BF16)
    h_lo = (h2 - h_hi.astype(F32)).astype(BF16)
    wr_hi = wr_hi_ref[...]
    logt_ref[...] = _dot_nt(wr_hi, h_hi) + _dot_nt(wr_hi, h_lo) + _dot_nt(wr_lo_ref[...], h_hi)


def _mixer(x, mod, p, alpha):
    bsz, seq, d = x.shape
    tm = min(MIXER_TOKENS, seq)
    nt = seq // tm
    mw = p["conv_w"].shape[1]
    gw = p["sg_g"].shape[1]
    dh = mw // M_HEADS
    n_exp = p["wr_hi"].shape[0]
    const2 = lambda b, i: (0, 0)
    const3 = lambda b, i: (0, 0, 0)
    full = lambda a: pl.BlockSpec(a.shape, const2 if a.ndim == 2 else const3)
    names = ["w_in", "conv_w", "conv_b", "w_q", "w_k", "w_v", "w_if", "b_if", "mh_g", "skip", "sg_g",
             "sg_b", "w_sp", "b_sp_t", "w_out", "ln1_g", "ln1_b", "wr_hi", "wr_lo"]
    weights = [p[n] for n in names]
    return pl.pallas_call(
        functools.partial(_mixer_kernel, alpha=alpha),
        out_shape=(
            jax.ShapeDtypeStruct((bsz, seq, d), F32),
            jax.ShapeDtypeStruct((bsz * seq, SLAB, LANES), BF16),
            jax.ShapeDtypeStruct((n_exp, bsz * seq), F32),
        ),
        grid=(bsz, nt),
        in_specs=[
            pl.BlockSpec((1, tm, d), lambda b, i: (b, i, 0)),
            pl.BlockSpec((1, SUBLANES, d), lambda b, i: (b, 0, 0)),
        ] + [full(w) for w in weights],
        out_specs=(
            pl.BlockSpec((1, tm, d), lambda b, i: (b, i, 0)),
            pl.BlockSpec((tm, SLAB, LANES), lambda b, i: (b * nt + i, 0, 0)),
            pl.BlockSpec((n_exp, tm), lambda b, i: (0, b * nt + i)),
        ),
        scratch_shapes=[
            pltpu.VMEM((M_HEADS, dh, 2 * dh), F32),
            pltpu.VMEM((SUBLANES, LANES), F32),
            pltpu.VMEM((tm + SUBLANES, mw), F32),
            pltpu.VMEM((tm, mw + gw), BF16),
            pltpu.VMEM((tm * SLAB, LANES), F32),
            pltpu.VMEM((tm, 3 * mw), BF16),
            pltpu.VMEM((tm, mw), F32),
        ],
        compiler_params=pltpu.CompilerParams(
            dimension_semantics=("arbitrary", "arbitrary"), vmem_limit_bytes=VMEM_LIMIT),
        name="mixer",
    )(x, mod, *weights)


def _route_kernel(logt_ref, bias_ref, upper_ref, idx_ref, w_ref, rank_ref, wt_ref, cnt_ref, carry_ref):
    n_exp, tr = logt_ref.shape
    epg = n_exp // N_GROUPS

    @pl.when(pl.program_id(0) == 0)
    def _():
        carry_ref[...] = jnp.zeros_like(carry_ref)

    scores = jax.nn.sigmoid(logt_ref[...])
    sel = scores + bias_ref[...]

    sel3 = sel.reshape(N_GROUPS, epg, tr)
    io3 = lax.broadcasted_iota(I32, (N_GROUPS, epg, tr), 1)
    m1 = jnp.max(sel3, axis=1, keepdims=True)
    first = jnp.min(jnp.where(sel3 == m1, io3, epg), axis=1, keepdims=True)
    m2 = jnp.max(jnp.where(io3 == first, NEG_INF, sel3), axis=1, keepdims=True)
    gs = (m1 + m2).reshape(N_GROUPS, tr)

    gio = lax.broadcasted_iota(I32, (N_GROUPS, tr), 0)
    gmask = jnp.zeros((N_GROUPS, tr), F32)
    for _ in range(TOPK_GROUPS):
        m = jnp.max(gs, axis=0, keepdims=True)
        gi = jnp.min(jnp.where(gs == m, gio, N_GROUPS), axis=0, keepdims=True)
        hit = gio == gi
        gmask = jnp.where(hit, 1.0, gmask)
        gs = jnp.where(hit, NEG_INF, gs)
    emask = jnp.broadcast_to(gmask.reshape(N_GROUPS, 1, tr), (N_GROUPS, epg, tr)).reshape(n_exp, tr)
    selm = jnp.where(emask > 0.0, sel, NEG_INF)

    eio = lax.broadcasted_iota(I32, (n_exp, tr), 0)
    chosen = jnp.zeros((n_exp, tr), F32)
    idx_rows, w_rows = [], []
    for _ in range(TOP_K):
        m = jnp.max(selm, axis=0, keepdims=True)
        ei = jnp.min(jnp.where(selm == m, eio, n_exp), axis=0, keepdims=True)
        hit = eio == ei
        w_rows.append(jnp.sum(jnp.where(hit, scores, 0.0), axis=0, keepdims=True))
        idx_rows.append(ei)
        selm = jnp.where(hit, NEG_INF, selm)
        chosen = jnp.where(hit, 1.0, chosen)

    chosen_b = chosen.astype(BF16)
    carry = carry_ref[...]
    ranks = carry[:, 0:1] + _dot(chosen_b, upper_ref[...])
    carry_new = carry + _dot(chosen_b, jnp.ones((tr, LANES), BF16))
    carry_ref[...] = carry_new
    cnt_ref[...] = carry_new
    rank_rows = [jnp.sum(jnp.where(eio == ei, ranks, 0.0), axis=0, keepdims=True) for ei in idx_rows]

    wsum = w_rows[0]
    for wk in w_rows[1:]:
        wsum = wsum + wk
    w8 = jnp.concatenate([wk / wsum * ROUTE_SCALE for wk in w_rows], axis=0)
    idx_ref[...] = jnp.concatenate(idx_rows, axis=0)
    rank_ref[...] = jnp.concatenate(rank_rows, axis=0).astype(I32)
    w_ref[...] = w8
    wpad = jnp.concatenate([w8, jnp.zeros((LANES - TOP_K, tr), F32)], axis=0)
    wt_ref[...] = wpad.T


def _route(logt, e_bias_col, upper):
    n_exp, t = logt.shape
    tr = upper.shape[0]
    return pl.pallas_call(
        _route_kernel,
        out_shape=(
            jax.ShapeDtypeStruct((TOP_K, t), I32),
            jax.ShapeDtypeStruct((TOP_K, t), F32),
            jax.ShapeDtypeStruct((TOP_K, t), I32),
            jax.ShapeDtypeStruct((t, LANES), F32),
            jax.ShapeDtypeStruct((n_exp, LANES), F32),
        ),
        grid=(t // tr,),
        in_specs=[
            pl.BlockSpec((n_exp, tr), lambda i: (0, i)),
            pl.BlockSpec((n_exp, 1), lambda i: (0, 0)),
            pl.BlockSpec((tr, tr), lambda i: (0, 0)),
        ],
        out_specs=(
            pl.BlockSpec((TOP_K, tr), lambda i: (0, i)),
            pl.BlockSpec((TOP_K, tr), lambda i: (0, i)),
            pl.BlockSpec((TOP_K, tr), lambda i: (0, i)),
            pl.BlockSpec((tr, LANES), lambda i: (i, 0)),
            pl.BlockSpec((n_exp, LANES), lambda i: (0, 0)),
        ),
        scratch_shapes=[pltpu.VMEM((n_exp, LANES), F32)],
        compiler_params=pltpu.CompilerParams(
            dimension_semantics=("arbitrary",), vmem_limit_bytes=VMEM_LIMIT),
        name="route",
    )(logt, e_bias_col, upper)


def _dest_kernel(idx_ref, rank_ref, start_ref, dest_ref):
    n_exp = start_ref.shape[0]
    tt = idx_ref.shape[1]
    eio = lax.broadcasted_iota(I32, (n_exp, tt), 0)
    start = start_ref[...]
    rows = []
    for k in range(TOP_K):
        hit = eio == idx_ref[k:k + 1, :]
        rows.append(jnp.sum(jnp.where(hit, start, 0), axis=0, keepdims=True) + rank_ref[k:k + 1, :])
    dest_ref[...] = jnp.concatenate(rows, axis=0)


def _dest(idx, rank, start_col):
    t = idx.shape[1]
    tt = min(DEST_TOKENS, t)
    n_exp = start_col.shape[0]
    return pl.pallas_call(
        _dest_kernel,
        out_shape=jax.ShapeDtypeStruct((TOP_K, t), I32),
        grid=(t // tt,),
        in_specs=[
            pl.BlockSpec((TOP_K, tt), lambda i: (0, i)),
            pl.BlockSpec((TOP_K, tt), lambda i: (0, i)),
            pl.BlockSpec((n_exp, 1), lambda i: (0, 0)),
        ],
        out_specs=pl.BlockSpec((TOP_K, tt), lambda i: (0, i)),
        compiler_params=pltpu.CompilerParams(vmem_limit_bytes=VMEM_LIMIT),
        name="dest",
    )(idx, rank, start_col)


def _slabs_copy(hbm, sem, n_slabs):
    return pltpu.make_async_copy(hbm.at[pl.ds(0, n_slabs)], hbm.at[pl.ds(0, n_slabs)], sem)


def _dispatch_kernel(pad_first, pad_len, n_used, dest_ref, h_ref, xs_hbm, zeros_ref, sem):
    td = dest_ref.shape[1]
    n_exp = pad_first.shape[0]
    n_blocks_max = xs_hbm.shape[0] // EXPERT_ROWS
    step = pl.program_id(0)

    def zero_fill(act):
        def zeros_to(first_row, n_rows):
            act(pltpu.make_async_copy(zeros_ref.at[pl.ds(0, n_rows)], xs_hbm.at[pl.ds(first_row, n_rows)],
                                      sem.at[1]))

        def expert_padding(e, carry):
            row = pad_first[e]
            for bit in reversed(range(EXPERT_ROWS.bit_length() - 1)):
                take = (pad_len[e] >> bit) & 1
                pl.when(take == 1)(functools.partial(zeros_to, row, 1 << bit))
                row = row + (take << bit)
            return carry

        def unused_block(b, carry):
            zeros_to(b * EXPERT_ROWS, EXPERT_ROWS)
            return carry

        lax.fori_loop(0, n_exp, expert_padding, 0)
        lax.fori_loop(n_used[0], n_blocks_max, unused_block, 0)

    @pl.when(step == 0)
    def _():
        zeros_ref[...] = jnp.zeros_like(zeros_ref)
        zero_fill(lambda cp: cp.start())

    def body(t, carry):
        for k in range(TOP_K):
            pltpu.make_async_copy(h_ref.at[t], xs_hbm.at[dest_ref[k, t]], sem.at[0]).start(priority=k % 2)
        return carry

    lax.fori_loop(0, td, body, 0, unroll=2)
    _slabs_copy(xs_hbm, sem.at[0], td * TOP_K).wait()

    @pl.when(step == pl.num_programs(0) - 1)
    def _():
        zero_fill(lambda cp: cp.wait())


def _dispatch(pad_first, pad_len, n_used, dest, h_slabs, n_rows):
    t = h_slabs.shape[0]
    td = min(DISPATCH_TOKENS, t)
    grid_spec = pltpu.PrefetchScalarGridSpec(
        num_scalar_prefetch=3,
        grid=(t // td,),
        in_specs=[
            pl.BlockSpec((TOP_K, td), lambda i, pf, pn, nu: (0, i), memory_space=pltpu.SMEM),
            pl.BlockSpec((td, SLAB, LANES), lambda i, pf, pn, nu: (i, 0, 0)),
        ],
        out_specs=pl.BlockSpec(memory_space=pl.ANY),
        scratch_shapes=[
            pltpu.VMEM((EXPERT_ROWS, SLAB, LANES), h_slabs.dtype),
            pltpu.SemaphoreType.DMA((2,)),
        ],
    )
    return pl.pallas_call(
        _dispatch_kernel,
        out_shape=jax.ShapeDtypeStruct((n_rows, SLAB, LANES), h_slabs.dtype),
        grid_spec=grid_spec,
        compiler_params=pltpu.CompilerParams(
            dimension_semantics=("arbitrary",), vmem_limit_bytes=VMEM_LIMIT),
        name="dispatch",
    )(pad_first, pad_len, n_used, dest, h_slabs)


def _experts_kernel(block_expert, block_new_expert, block_slot, block_next_expert, n_blocks,
                    xs_ref, wg_hbm, wu_hbm, wd_hbm, ys_ref,
                    wg_f, wu_f, wd_f, wg_b, wu_b, wd_b, slab_ref, sem):
    step = pl.program_id(0)

    def weight_copies(e, slot):
        return [pltpu.make_async_copy(hbm.at[e], buf.at[slot], sem.at[slot])
                for hbm, buf in ((wg_hbm, wg_f), (wu_hbm, wu_f), (wd_hbm, wd_f))]

    def block(i, rows):
        @pl.when(block_new_expert[i] == 1)
        def _():
            slot = block_slot[i]

            @pl.when(i == 0)
            def _():
                for cp in weight_copies(block_expert[i], slot):
                    cp.start()

            for cp in weight_copies(block_expert[i], slot):
                cp.wait()
            wg_b[...] = wg_f[slot].astype(BF16)
            wu_b[...] = wu_f[slot].astype(BF16)
            wd_b[...] = wd_f[slot].astype(BF16)

            @pl.when(block_next_expert[i] >= 0)
            def _():
                for cp in weight_copies(block_next_expert[i], 1 - slot):
                    cp.start()

        xb = _from_slabs(xs_ref[rows], slab_ref).astype(BF16)
        g = _dot(xb, wg_b[...])
        u = _dot(xb, wu_b[...])
        y = _dot((_silu(g) * u).astype(BF16), wd_b[...])
        ys_ref[rows] = _to_slabs(y, slab_ref)

    for j in range(EXPERT_BLOCKS_PER_STEP):
        i = step * EXPERT_BLOCKS_PER_STEP + j
        rows = slice(j * EXPERT_ROWS, (j + 1) * EXPERT_ROWS)
        pl.when(i < n_blocks[0])(functools.partial(block, i, rows))

        @pl.when((i >= n_blocks[0]) & (step * EXPERT_BLOCKS_PER_STEP < n_blocks[0]))
        def _():
            ys_ref[rows] = xs_ref[rows]


def _experts(meta, xs, w_gate, w_up, w_down):
    n_scalar = len(meta)
    _, d, de = w_gate.shape
    n_max = meta[0].shape[0]
    per_step = EXPERT_BLOCKS_PER_STEP
    step_rows = per_step * EXPERT_ROWS
    blk = lambda i, be, ne, sl, nx, nb: (jnp.minimum(i, (nb[0] - 1) // per_step), 0, 0)
    grid_spec = pltpu.PrefetchScalarGridSpec(
        num_scalar_prefetch=n_scalar,
        grid=(n_max // per_step,),
        in_specs=[
            pl.BlockSpec((step_rows, SLAB, LANES), blk),
            pl.BlockSpec(memory_space=pl.ANY),
            pl.BlockSpec(memory_space=pl.ANY),
            pl.BlockSpec(memory_space=pl.ANY),
        ],
        out_specs=pl.BlockSpec((step_rows, SLAB, LANES), blk),
        scratch_shapes=[
            pltpu.VMEM((2, d, de), F32),
            pltpu.VMEM((2, d, de), F32),
            pltpu.VMEM((2, de, d), F32),
            pltpu.VMEM((d, de), BF16),
            pltpu.VMEM((d, de), BF16),
            pltpu.VMEM((de, d), BF16),
            pltpu.VMEM((EXPERT_ROWS * SLAB, LANES), F32),
            pltpu.SemaphoreType.DMA((2,)),
        ],
    )
    return pl.pallas_call(
        _experts_kernel,
        out_shape=jax.ShapeDtypeStruct(xs.shape, xs.dtype),
        grid_spec=grid_spec,
        input_output_aliases={n_scalar: 0},
        compiler_params=pltpu.CompilerParams(
            dimension_semantics=("arbitrary",), vmem_limit_bytes=VMEM_LIMIT),
        name="experts",
    )(*meta, xs, w_gate, w_up, w_down)


def _expert_blocks(counts, n_assign):
    n_exp = counts.shape[0]
    n_max = (n_assign + n_exp * (EXPERT_ROWS - 1)) // EXPERT_ROWS + 1
    n_max += -n_max % EXPERT_BLOCKS_PER_STEP
    padded = (counts + EXPERT_ROWS - 1) // EXPERT_ROWS * EXPERT_ROWS
    pad_end = jnp.cumsum(padded)
    pad_start = pad_end - padded
    n_used = pad_end[-1] // EXPERT_ROWS
    first_row = jnp.minimum(jnp.arange(n_max, dtype=I32), n_used - 1) * EXPERT_ROWS
    e_of = jnp.sum((first_row[:, None] >= pad_end[None, :]).astype(I32), axis=1)
    e_of = jnp.minimum(e_of, n_exp - 1).astype(I32)
    prev_e = jnp.concatenate([jnp.full((1,), -1, I32), e_of[:-1]])
    new_expert = (e_of != prev_e).astype(I32)
    slot = ((jnp.cumsum(new_expert) - 1) & 1).astype(I32)
    ids = jnp.arange(n_exp, dtype=I32)
    later_used = (ids[None, :] > ids[:, None]) & (padded[None, :] > 0)
    next_used = jnp.min(jnp.where(later_used, ids[None, :], n_exp), axis=1)
    next_used = jnp.where(next_used < n_exp, next_used, -1)
    next_expert = jnp.sum(jnp.where(e_of[:, None] == ids[None, :], next_used[None, :], 0), axis=1).astype(I32)
    n_used = n_used.reshape(1).astype(I32)
    meta = (e_of, new_expert, slot, next_expert, n_used)
    return (meta, n_used, pad_start.astype(I32), (pad_start + counts).astype(I32),
            (padded - counts).astype(I32), n_max)


def _combine_kernel(dest_cur, dest_nxt, x1_ref, mod_ref, wt_ref, wsg_ref, wsu_ref, wsd_ref,
                    ln2g_ref, ln2b_ref, ys_hbm, out_ref, buf, wb_ref, acc_ref, sem, *, alpha):
    tc = x1_ref.shape[0]
    i = pl.program_id(0)
    n = pl.num_programs(0)
    slot = i % 2

    def request(dest_ref, s, t):
        for k in range(TOP_K):
            pltpu.make_async_copy(ys_hbm.at[dest_ref[k, t]], buf.at[s, t * TOP_K + k],
                                  sem.at[s]).start(priority=k % 2)

    @pl.when(i == 0)
    def _():
        def body(t, carry):
            request(dest_cur, 0, t)
            return carry
        lax.fori_loop(0, tc, body, 0)

    pltpu.make_async_copy(ys_hbm.at[pl.ds(0, tc * TOP_K)], buf.at[slot], sem.at[slot]).wait()

    wt = wt_ref[...]
    for k in range(TOP_K):
        wb_ref[k] = jnp.broadcast_to(wt[:, k:k + 1], (tc, LANES))

    def token(t, carry, prefetch):
        if prefetch:
            request(dest_nxt, 1 - slot, t)
        terms = [jnp.broadcast_to(wb_ref[k, pl.ds(t, 1), :], (SLAB, LANES))
                 * buf[slot, t * TOP_K + k].astype(F32)
                 for k in range(TOP_K)]
        while len(terms) > 1:
            terms = [a + b for a, b in zip(terms[0::2], terms[1::2])]
        acc_ref[pl.ds(pl.multiple_of(t * SLAB, SLAB), SLAB), :] = terms[0]
        return carry

    @pl.when(i + 1 < n)
    def _():
        lax.fori_loop(0, tc, functools.partial(token, prefetch=True), 0, unroll=8)

    @pl.when(i + 1 == n)
    def _():
        lax.fori_loop(0, tc, functools.partial(token, prefetch=False), 0, unroll=8)

    mod = mod_ref[0]
    sh_f, sc_f, gt_f = mod[3:4], mod[4:5], mod[5:6]
    x1 = x1_ref[...]
    hb = (_ln(x1) * (1.0 + sc_f) + sh_f).astype(BF16)
    mid = _silu(_dot(hb, wsg_ref[...])) * _dot(hb, wsu_ref[...])
    ffn = _dot(mid.astype(BF16), wsd_ref[...])
    ffn = _slabs_to_rows(acc_ref, tc) + ffn
    out_ref[...] = _ln(alpha * x1 + (1.0 + gt_f) * ffn) * ln2g_ref[...] + ln2b_ref[...]


def _combine(dest, x1_flat, mod, wt, ws_gate, ws_up, ws_down, ln2_g, ln2_b, ys, seq, alpha):
    t, d = x1_flat.shape
    tc = min(COMBINE_TOKENS, seq)
    n = t // tc
    per_seq = seq // tc
    const = lambda i: (0, 0)
    return pl.pallas_call(
        functools.partial(_combine_kernel, alpha=alpha),
        out_shape=jax.ShapeDtypeStruct((t, d), F32),
        grid=(n,),
        in_specs=[
            pl.BlockSpec((TOP_K, tc), lambda i: (0, i), memory_space=pltpu.SMEM),
            pl.BlockSpec((TOP_K, tc), lambda i: (0, jnp.minimum(i + 1, n - 1)), memory_space=pltpu.SMEM),
            pl.BlockSpec((tc, d), lambda i: (i, 0)),
            pl.BlockSpec((1, SUBLANES, d), lambda i: (i // per_seq, 0, 0)),
            pl.BlockSpec((tc, LANES), lambda i: (i, 0)),
            pl.BlockSpec(ws_gate.shape, const),
            pl.BlockSpec(ws_up.shape, const),
            pl.BlockSpec(ws_down.shape, const),
            pl.BlockSpec(ln2_g.shape, const),
            pl.BlockSpec(ln2_b.shape, const),
            pl.BlockSpec(memory_space=pl.ANY),
        ],
        out_specs=pl.BlockSpec((tc, d), lambda i: (i, 0)),
        scratch_shapes=[
            pltpu.VMEM((2, tc * TOP_K, SLAB, LANES), BF16),
            pltpu.VMEM((TOP_K, tc, LANES), F32),
            pltpu.VMEM((tc * SLAB, LANES), F32),
            pltpu.SemaphoreType.DMA((2,)),
        ],
        compiler_params=pltpu.CompilerParams(
            dimension_semantics=("arbitrary",), vmem_limit_bytes=VMEM_LIMIT),
        name="combine",
    )(dest, dest, x1_flat, mod, wt, ws_gate, ws_up, ws_down, ln2_g, ln2_b, ys)


def _layer(x, c_pad, lw, alpha):
    bsz, seq, d = x.shape
    t = bsz * seq
    n_exp = lw["w_router"].shape[1]
    h_count = lw["w_if"].shape[1] // 2

    mod = _ada(c_pad, lw["w_ada"], lw["b_ada"].reshape(1, -1))[:bsz].reshape(bsz, 6, d)
    mod = jnp.pad(mod, ((0, 0), (0, SUBLANES - 6), (0, 0)))

    row2 = lambda a: a.reshape(1, -1)
    w_if = lw["w_if"]
    w_if_pad = jnp.zeros((w_if.shape[0], 2 * LANES), F32)
    w_if_pad = w_if_pad.at[:, :h_count].set(w_if[:, :h_count]).at[:, LANES:LANES + h_count].set(w_if[:, h_count:])
    b_if_pad = jnp.zeros((1, 2 * LANES), F32)
    b_if_pad = b_if_pad.at[0, :h_count].set(lw["b_if"][:h_count]).at[0, LANES:LANES + h_count].set(lw["b_if"][h_count:])
    w_rt = lw["w_router"].astype(F32).T
    wr_hi = w_rt.astype(BF16)
    wr_lo = (w_rt - wr_hi.astype(F32)).astype(BF16)
    p = {
        "w_in": lw["w_in"].astype(BF16), "conv_w": lw["conv_w"], "conv_b": row2(lw["conv_b"]),
        "w_q": lw["w_q"].astype(BF16), "w_k": lw["w_k"].astype(BF16), "w_v": lw["w_v"].astype(BF16),
        "w_if": w_if_pad.astype(BF16), "b_if": b_if_pad,
        "mh_g": row2(lw["mh_g"]), "skip": row2(lw["skip"]), "sg_g": row2(lw["sg_g"]), "sg_b": row2(lw["sg_b"]),
        "w_sp": lw["w_sp"], "b_sp_t": lw["b_sp"].T, "w_out": lw["w_out"].astype(BF16),
        "ln1_g": row2(lw["ln1_g"]), "ln1_b": row2(lw["ln1_b"]), "wr_hi": wr_hi, "wr_lo": wr_lo,
    }
    x1, h2, logt = _mixer(x, mod, p, alpha)

    tr = min(ROUTE_TOKENS, t)
    upper = (jnp.arange(tr)[:, None] < jnp.arange(tr)[None, :]).astype(BF16)
    idx, _, rank, wt, cnt = _route(logt, lw["e_bias"].astype(F32).reshape(n_exp, 1), upper)
    counts = cnt[:, 0].astype(I32)
    meta, n_used, pad_start, pad_first, pad_len, n_blocks_max = _expert_blocks(counts, t * TOP_K)
    dest = _dest(idx, rank, pad_start.reshape(n_exp, 1))

    xs = _dispatch(pad_first, pad_len, n_used, dest, h2, n_blocks_max * EXPERT_ROWS)
    ys = _experts(meta, xs, lw["w_gate"], lw["w_up"], lw["w_down"])
    out = _combine(dest, x1.reshape(t, d), mod, wt, lw["ws_gate"].astype(BF16), lw["ws_up"].astype(BF16),
                   lw["ws_down"].astype(BF16), row2(lw["ln2_g"]), row2(lw["ln2_b"]), ys, seq, alpha)
    return out.reshape(bsz, seq, d)


def kernel(x, c, w_ada, b_ada, w_in, conv_w, conv_b, w_q, w_k, w_v, w_if, b_if, mh_g, skip, sg_g, sg_b, w_sp, b_sp, w_out, ln1_g, ln1_b, w_router, e_bias, w_gate, w_up, w_down, ws_gate, ws_up, ws_down, ln2_g, ln2_b):
    stacked = dict(w_ada=w_ada, b_ada=b_ada, w_in=w_in, conv_w=conv_w, conv_b=conv_b, w_q=w_q, w_k=w_k,
                   w_v=w_v, w_if=w_if, b_if=b_if, mh_g=mh_g, skip=skip, sg_g=sg_g, sg_b=sg_b, w_sp=w_sp,
                   b_sp=b_sp, w_out=w_out, ln1_g=ln1_g, ln1_b=ln1_b, w_router=w_router, e_bias=e_bias,
                   w_gate=w_gate, w_up=w_up, w_down=w_down, ws_gate=ws_gate, ws_up=ws_up, ws_down=ws_down,
                   ln2_g=ln2_g, ln2_b=ln2_b)
    depth = w_ada.shape[0]
    alpha = float((2 * depth) ** 0.25)
    bsz = x.shape[0]
    c_pad = jnp.pad(c, ((0, -bsz % SUBLANES), (0, 0)))
    for l in range(depth):
        x = _layer(x, c_pad, {k: v[l] for k, v in stacked.items()}, alpha)
    return x
```
